```python
import math
import jax
import jax.numpy as jnp
from jax import lax
import numpy as np

D_MODEL = 1024
BATCH = 32
SEQ = 2048
DEPTH = 2

GRID_W = 64
CTX_LEN = 256

N_GROUPS = 4
HEAD_GROUP = D_MODEL // N_GROUPS
MIX_WIDTH = N_GROUPS * HEAD_GROUP

N_ATTN_HEADS = 4
ATTN_V_DIM = HEAD_GROUP // N_ATTN_HEADS
ATTN_QK_DIM = ATTN_V_DIM // 2
QUERY_BLOCK = 128
ROPE_BASE = 10000.0
SUBLN_EPS = 1e-5

HYENA_ORDER = 2
HYENA_SHORT = 3
HYENA_EMB = 33
HYENA_FFN = 64
HYENA_TARGET = 1e-2
HYENA_FAST_PCT = 0.3
HYENA_SLOW_PCT = 1.5

POOL_WINDOWS = (2, 4, 8, 16)
POOL_GROUP = HEAD_GROUP // len(POOL_WINDOWS)

CONV_KERNEL = 31

N_EXPERTS = 32
TOP_K = 4
EXPERT_FF = D_MODEL
SWIGLU_ALPHA = 1.702
SWIGLU_LIMIT = 7.0

OFF_HYENA = 3 * HEAD_GROUP
OFF_POOL = OFF_HYENA + 3 * HEAD_GROUP
OFF_CONV = OFF_POOL + HEAD_GROUP
IN_WIDTH = OFF_CONV + 2 * HEAD_GROUP

NORM_EPS = 1e-6

kernel_name = "hybrid_dit_diffattn_hyena_pool_conformer_moe"


def rms_norm(x, g, eps=NORM_EPS):
    xf = x.astype(jnp.float32)
    y = xf * lax.rsqrt(jnp.mean(xf * xf, axis=-1, keepdims=True) + eps)
    return (y * g.astype(jnp.float32)).astype(x.dtype)


def layer_norm(x, g, b, eps=1e-5):
    xf = x.astype(jnp.float32)
    mu = jnp.mean(xf, axis=-1, keepdims=True)
    var = jnp.mean(jnp.square(xf - mu), axis=-1, keepdims=True)
    y = (xf - mu) * lax.rsqrt(var + eps) * g.astype(jnp.float32) + b.astype(jnp.float32)
    return y.astype(x.dtype)


def adaln_params(cvec, w_mod, b_mod):
    m = jax.nn.silu(cvec) @ w_mod + b_mod
    return jnp.split(m, 6, axis=-1)


def modulate(x, g, shift, scale):
    return rms_norm(x, g) * (1.0 + scale) + shift


def depthwise_conv(u, w, b):
    k = w.shape[0]
    pad = (k - 1) // 2
    y = lax.conv_general_dilated(
        u, w[:, None, :], window_strides=(1,), padding=[(pad, k - 1 - pad)],
        dimension_numbers=("NWC", "WIO", "NWC"), feature_group_count=u.shape[-1])
    return y + b


def axial_angles(n_tokens):
    rows = n_tokens // GRID_W
    row = jnp.repeat(jnp.arange(rows, dtype=jnp.float32), GRID_W)
    col = jnp.tile(jnp.arange(GRID_W, dtype=jnp.float32), rows)
    n_freq = ATTN_QK_DIM // 4
    inv = ROPE_BASE ** (-jnp.arange(n_freq, dtype=jnp.float32) / n_freq)
    row_ang = (row[:, None] * inv)[:, None, None, :]
    col_ang = (col[:, None] * inv)[:, None, None, :]
    return row_ang, col_ang


def rope_pairs(x, ang):
    x1, x2 = jnp.split(x, 2, axis=-1)
    c = jnp.cos(ang).astype(x.dtype)
    s = jnp.sin(ang).astype(x.dtype)
    return jnp.concatenate([x1 * c - x2 * s, x2 * c + x1 * s], axis=-1)


def apply_axial_rope(x, row_ang, col_ang):
    xr, xc = jnp.split(x, 2, axis=-1)
    return jnp.concatenate([rope_pairs(xr, row_ang), rope_pairs(xc, col_ang)], axis=-1)


def split_qk(t):
    return t.reshape(*t.shape[:-1], N_ATTN_HEADS, 2, ATTN_QK_DIM)


def split_v(t):
    return t.reshape(*t.shape[:-1], N_ATTN_HEADS, ATTN_V_DIM)


def diff_lambda(lp, lam_init):
    lp = lp.astype(jnp.float32)
    return jnp.exp(jnp.sum(lp[0] * lp[1])) - jnp.exp(jnp.sum(lp[2] * lp[3])) + lam_init


def diff_attention(q, k, v, lam):
    s = jnp.einsum("bqhmd,bkhmd->bhmqk", q, k).astype(jnp.float32) * (ATTN_QK_DIM ** -0.5)
    p = jax.nn.softmax(s, axis=-1)
    a = p[:, :, 0] - lam * p[:, :, 1]
    return jnp.einsum("bhqk,bkhd->bqhd", a.astype(v.dtype), v)


def latent_diff_attention(q, k_all, v_all, lam):
    b, n = q.shape[:2]
    nb = n // QUERY_BLOCK
    qb = jnp.moveaxis(q.reshape(b, nb, QUERY_BLOCK, *q.shape[2:]), 1, 0)
    ob = lax.map(lambda qi: diff_attention(qi, k_all, v_all, lam), qb)
    return jnp.moveaxis(ob, 0, 1).reshape(b, n, N_ATTN_HEADS, ATTN_V_DIM)


def attn_group_out(o, subln, lam_init):
    o = rms_norm(o, subln, eps=SUBLN_EPS) * (1.0 - lam_init)
    return o.reshape(*o.shape[:2], HEAD_GROUP)


def hyena_filters(n_tokens, w1, b1, w2, b2, w3, freq):
    f32 = jnp.float32
    bands = (HYENA_EMB - 1) // 2
    t = jnp.linspace(0.0, 1.0, n_tokens, dtype=f32)[:, None]
    w = 2.0 * math.pi * jnp.arange(n_tokens, dtype=f32)[:, None] / n_tokens
    f = jnp.linspace(1e-4, bands - 1, bands, dtype=f32)[None, :]
    feat = jnp.concatenate([t, jnp.cos(f * w), -jnp.sin(f * w)], axis=-1)
    fr = freq.astype(f32)
    z = jnp.sin(fr * (feat @ w1.astype(f32) + b1.astype(f32)))
    z = jnp.sin(fr * (z @ w2.astype(f32) + b2.astype(f32)))
    h = (z @ w3.astype(f32)).reshape(n_tokens, HYENA_ORDER, 2, HEAD_GROUP)
    min_decay = math.log(HYENA_TARGET) / HYENA_SLOW_PCT
    max_decay = math.log(HYENA_TARGET) / HYENA_FAST_PCT
    deltas = jnp.abs(jnp.linspace(min_decay, max_decay, HEAD_GROUP, dtype=f32))
    decay = jnp.exp(-t * deltas)
    return h * decay[:, None, None, :]


def bidir_long_conv(u, h_fwd, h_bwd, bias):
    n = u.shape[1]
    filt_full = jnp.concatenate([h_fwd, jnp.zeros_like(h_fwd[:1]), h_bwd[:0:-1]], axis=0)
    uf = jnp.fft.rfft(u.astype(jnp.float32), n=2 * n, axis=1)
    hf = jnp.fft.rfft(filt_full, n=2 * n, axis=0)
    y = jnp.fft.irfft(uf * hf[None], n=2 * n, axis=1)[:, :n]
    return (y + u.astype(jnp.float32) * bias.astype(jnp.float32)).astype(u.dtype)


def hyena_mixer(p, short_w, short_b, filt, bias):
    u = depthwise_conv(p, short_w, short_b)
    v, x1, x2 = jnp.split(u, 3, axis=-1)
    z = x1 * bidir_long_conv(v, filt[:, 0, 0], filt[:, 0, 1], bias[0])
    return x2 * bidir_long_conv(z, filt[:, 1, 0], filt[:, 1, 1], bias[1])


def pool_mixer(u, w_pool, scale):
    b, n, c = u.shape
    cs = jnp.concatenate([jnp.zeros((b, 1, c), jnp.float32), jnp.cumsum(u.astype(jnp.float32), axis=1)], axis=1)
    t = jnp.arange(n)
    means = []
    for g, win in enumerate(POOL_WINDOWS):
        lo = jnp.clip(t - win // 2, 0, n)
        hi = jnp.clip(t + win - win // 2, 0, n)
        csg = cs[..., g * POOL_GROUP:(g + 1) * POOL_GROUP]
        means.append((csg[:, hi] - csg[:, lo]) / (hi - lo).astype(jnp.float32)[:, None])
    pooled = jnp.concatenate(means, axis=-1).astype(u.dtype) - u
    y = jnp.einsum("blgc,gcd->blgd", pooled.reshape(b, n, len(POOL_WINDOWS), POOL_GROUP), w_pool)
    return y.reshape(b, n, c) * scale


def conformer_mixer(p, dw_w, dw_b, ln_g, ln_b, pw_w, pw_b):
    a, g = jnp.split(p, 2, axis=-1)
    u = depthwise_conv(a * jax.nn.sigmoid(g), dw_w, dw_b)
    u = jax.nn.silu(layer_norm(u, ln_g, ln_b))
    return u @ pw_w + pw_b


def local_groups(p, filt, short_w, short_b, hy_bias, w_pool, pool_scale, dw_w, dw_b, ln_g, ln_b, pw_w, pw_b):
    g = HEAD_GROUP
    o_hy = hyena_mixer(p[..., :3 * g], short_w, short_b, filt, hy_bias)
    o_pool = pool_mixer(p[..., 3 * g:4 * g], w_pool, pool_scale)
    o_conv = conformer_mixer(p[..., 4 * g:], dw_w, dw_b, ln_g, ln_b, pw_w, pw_b)
    return jnp.concatenate([o_hy, o_pool, o_conv], axis=-1)


def moe_ffn(h, router_w, router_b, w1, b1, w2, b2):
    logits = (h @ router_w + router_b).astype(jnp.float32)
    top_val, top_idx = lax.top_k(logits, TOP_K)
    gates = jax.nn.softmax(top_val, axis=-1)
    combine = jnp.einsum("...k,...ke->...e", gates,
                         jax.nn.one_hot(top_idx, N_EXPERTS, dtype=jnp.float32)).astype(h.dtype)
    y = jnp.zeros_like(h)
    for e in range(N_EXPERTS):
        hid = h @ w1[e] + b1[e]
        gate, up = jnp.split(hid, 2, axis=-1)
        gate = jnp.minimum(gate, SWIGLU_LIMIT)
        up = jnp.clip(up, -SWIGLU_LIMIT, SWIGLU_LIMIT)
        act = gate * jax.nn.sigmoid(SWIGLU_ALPHA * gate) * (up + 1.0)
        y = y + combine[..., e:e + 1] * (act @ w2[e] + b2[e])
    return y


def setup_inputs(seed: int = 0) -> dict:
    key = jax.random.key(seed)
    ks = iter(jax.random.split(key, 40))

    def nrm(shape, scale):
        return scale * jax.random.normal(next(ks), shape, jnp.float32)

    def gain(shape):
        return 1.0 + nrm(shape, 0.02)

    g = HEAD_GROUP
    return {
        "x": nrm((BATCH, SEQ, D_MODEL), 1.0),
        "c": nrm((BATCH, D_MODEL), 1.0),
        "ctx": nrm((BATCH, CTX_LEN, D_MODEL), 1.0),
        "c_ctx": nrm((D_MODEL,), 1.0),
        "norm1": gain((DEPTH, D_MODEL)),
        "norm2": gain((DEPTH, D_MODEL)),
        "w_mod": nrm((DEPTH, D_MODEL, 6 * D_MODEL), 0.5 * D_MODEL ** -0.5),
        "b_mod": nrm((DEPTH, 6 * D_MODEL), 0.02),
        "w_in": nrm((DEPTH, D_MODEL, IN_WIDTH), D_MODEL ** -0.5),
        "w_out": nrm((DEPTH, MIX_WIDTH, D_MODEL), MIX_WIDTH ** -0.5),
        "attn_lambda": nrm((DEPTH, 4, ATTN_QK_DIM), 0.1),
        "attn_subln": gain((DEPTH, ATTN_V_DIM)),
        "hyena_short_w": nrm((DEPTH, HYENA_SHORT, 3 * g), HYENA_SHORT ** -0.5),
        "hyena_short_b": nrm((DEPTH, 3 * g), 0.02),
        "hyena_w1": nrm((DEPTH, HYENA_EMB, HYENA_FFN), HYENA_EMB ** -0.5),
        "hyena_b1": nrm((DEPTH, HYENA_FFN), 0.1),
        "hyena_w2": nrm((DEPTH, HYENA_FFN, HYENA_FFN), HYENA_FFN ** -0.5),
        "hyena_b2": nrm((DEPTH, HYENA_FFN), 0.1),
        "hyena_w3": nrm((DEPTH, HYENA_FFN, HYENA_ORDER * 2 * g), 0.07 * HYENA_FFN ** -0.5),
        "hyena_freq": gain((DEPTH, HYENA_FFN)),
        "hyena_bias": nrm((DEPTH, HYENA_ORDER, g), 0.1),
        "pool_w": nrm((DEPTH, len(POOL_WINDOWS), POOL_GROUP, POOL_GROUP), POOL_GROUP ** -0.5),
        "pool_scale": gain((DEPTH, g)),
        "conv_dw_w": nrm((DEPTH, CONV_KERNEL, g), CONV_KERNEL ** -0.5),
        "conv_dw_b": nrm((DEPTH, g), 0.02),
        "conv_ln_g": gain((DEPTH, g)),
        "conv_ln_b": nrm((DEPTH, g), 0.02),
        "conv_pw_w": nrm((DEPTH, g, g), g ** -0.5),
        "conv_pw_b": nrm((DEPTH, g), 0.02),
        "router_w": nrm((DEPTH, D_MODEL, N_EXPERTS), D_MODEL ** -0.5),
        "router_b": nrm((DEPTH, N_EXPERTS), 0.01),
        "moe_w1": nrm((DEPTH, N_EXPERTS, D_MODEL, 2 * EXPERT_FF), D_MODEL ** -0.5),
        "moe_b1": nrm((DEPTH, N_EXPERTS, 2 * EXPERT_FF), 0.02),
        "moe_w2": nrm((DEPTH, N_EXPERTS, EXPERT_FF, D_MODEL), EXPERT_FF ** -0.5),
        "moe_b2": nrm((DEPTH, N_EXPERTS, D_MODEL), 0.02),
        "final_norm": gain((D_MODEL,)),
    }


def reference(x, c, ctx, c_ctx, norm1, norm2, w_mod, b_mod, w_in, w_out, attn_lambda, attn_subln,
              hyena_short_w, hyena_short_b, hyena_w1, hyena_b1, hyena_w2, hyena_b2, hyena_w3, hyena_freq,
              hyena_bias, pool_w, pool_scale, conv_dw_w, conv_dw_b, conv_ln_g, conv_ln_b, conv_pw_w,
              conv_pw_b, router_w, router_b, moe_w1, moe_b1, moe_w2, moe_b2, final_norm):
    n_lat = x.shape[1]
    row_ang, col_ang = axial_angles(n_lat)
    xc = ctx
    for i in range(DEPTH):
        last = i == DEPTH - 1
        sh1, sc1, gt1, sh2, sc2, gt2 = [m[:, None, :] for m in adaln_params(c, w_mod[i], b_mod[i])]
        csh1, csc1, cgt1, csh2, csc2, cgt2 = adaln_params(c_ctx, w_mod[i], b_mod[i])
        lam_init = 0.8 - 0.6 * math.exp(-0.3 * i)
        lam = diff_lambda(attn_lambda[i], lam_init)
        filt_params = (hyena_w1[i], hyena_b1[i], hyena_w2[i], hyena_b2[i], hyena_w3[i], hyena_freq[i])
        local_params = (hyena_short_w[i], hyena_short_b[i], hyena_bias[i], pool_w[i], pool_scale[i],
                        conv_dw_w[i], conv_dw_b[i], conv_ln_g[i], conv_ln_b[i], conv_pw_w[i], conv_pw_b[i])
        moe_params = (router_w[i], router_b[i], moe_w1[i], moe_b1[i], moe_w2[i], moe_b2[i])

        h = modulate(x, norm1[i], sh1, sc1)
        hc = modulate(xc, norm1[i], csh1, csc1)
        p = h @ w_in[i]
        q, k, v = jnp.split(p[..., :OFF_HYENA], 3, axis=-1)
        if last:
            kc, vc = jnp.split(hc @ w_in[i][:, HEAD_GROUP:OFF_HYENA], 2, axis=-1)
        else:
            pc = hc @ w_in[i]
            qc, kc, vc = jnp.split(pc[..., :OFF_HYENA], 3, axis=-1)
        kc = split_qk(kc)
        vc = split_v(vc)
        q = apply_axial_rope(split_qk(q), row_ang, col_ang)
        k = apply_axial_rope(split_qk(k), row_ang, col_ang)
        k_all = jnp.concatenate([kc, k], axis=1)
        v_all = jnp.concatenate([vc, split_v(v)], axis=1)
        o_attn = attn_group_out(latent_diff_attention(q, k_all, v_all, lam), attn_subln[i], lam_init)
        o_local = local_groups(p[..., OFF_HYENA:], hyena_filters(n_lat, *filt_params), *local_params)
        x = x + gt1 * (jnp.concatenate([o_attn, o_local], axis=-1) @ w_out[i])
        if not last:
            oc_attn = attn_group_out(diff_attention(split_qk(qc), kc, vc, lam), attn_subln[i], lam_init)
            oc_local = local_groups(pc[..., OFF_HYENA:], hyena_filters(xc.shape[1], *filt_params), *local_params)
            xc = xc + cgt1 * (jnp.concatenate([oc_attn, oc_local], axis=-1) @ w_out[i])

        x = x + gt2 * moe_ffn(modulate(x, norm2[i], sh2, sc2), *moe_params)
        if not last:
            xc = xc + cgt2 * moe_ffn(modulate(xc, norm2[i], csh2, csc2), *moe_params)
    return rms_norm(x, final_norm)
```

```python
import functools
import math

import jax
import jax.numpy as jnp
from jax import lax
from jax.experimental import pallas as pl
from jax.experimental.pallas import tpu as pltpu

F32 = jnp.float32
BF16 = jnp.bfloat16

GRID_W = 64
N_HEADS = 4
HEAD_V = 64
HEAD_QK = 32
GROUP = 256
ROPE_BASE = 10000.0
SUBLN_EPS = 1e-5
NORM_EPS = 1e-6
LN_EPS = 1e-5
POOL_WINDOWS = (2, 4, 8, 16)
CONV_KERNEL = 31
HYENA_SHORT = 3
HYENA_EMB = 33
HYENA_TARGET = 1e-2
HYENA_FAST_PCT = 0.3
HYENA_SLOW_PCT = 1.5
N_EXPERTS = 32
TOP_K = 4
SWIGLU_ALPHA = 1.702
SWIGLU_LIMIT = 7.0

LANES = 128
SUBLANES = 8
VMEM_LIMIT = 56 * 1024 * 1024
HALO = 16


def _cparams(*sem):
    return pltpu.CompilerParams(dimension_semantics=sem, vmem_limit_bytes=VMEM_LIMIT)


def _split_bf16(a):
    hi = a.astype(BF16)
    lo = (a - hi.astype(F32)).astype(BF16)
    return hi, lo


def _dot(a, b):
    return jnp.dot(a, b, preferred_element_type=F32)


def _dot3(a, b):
    ah, al = _split_bf16(a)
    bh, bl = _split_bf16(b)
    return _dot(ah, bh) + (_dot(ah, bl) + _dot(al, bh))


def _sigmoid(x):
    return 1.0 / (1.0 + jnp.exp(-x))


def _adaln_kernel(c_ref, w_ref, b_ref, o_ref):
    cv = c_ref[...]
    s = cv * _sigmoid(cv)
    o_ref[0] = _dot3(s, w_ref[0]) + b_ref[0]


def adaln_all(cvec, w_mod, b_mod):
    depth, d, n = w_mod.shape
    r = cvec.shape[0]
    tn = 1536
    return pl.pallas_call(
        _adaln_kernel,
        grid=(depth, n // tn),
        in_specs=[pl.BlockSpec((r, d), lambda i, j: (0, 0)),
                  pl.BlockSpec((1, d, tn), lambda i, j: (i, 0, j)),
                  pl.BlockSpec((1, 1, tn), lambda i, j: (i, 0, j))],
        out_specs=pl.BlockSpec((1, r, tn), lambda i, j: (i, 0, j)),
        out_shape=jax.ShapeDtypeStruct((depth, r, n), F32),
        compiler_params=_cparams("arbitrary", "arbitrary"),
        name="adaln",
    )(cvec, w_mod, b_mod.reshape(depth, 1, n))


def _filter_kernel(feat_ref, w1_ref, b1_ref, w2_ref, b2_ref, w3_ref, fr_ref, dec_ref, o_ref):
    fr = fr_ref[...]
    z = jnp.sin(fr * (_dot3(feat_ref[...], w1_ref[...]) + b1_ref[...]))
    z = jnp.sin(fr * (_dot3(z, w2_ref[...]) + b2_ref[...]))
    o_ref[...] = _dot3(z, w3_ref[...]) * dec_ref[...]


def hyena_filters(n_tokens, w1, b1, w2, b2, w3, freq):
    bands = (HYENA_EMB - 1) // 2
    t = jnp.linspace(0.0, 1.0, n_tokens, dtype=F32)[:, None]
    w = 2.0 * math.pi * jnp.arange(n_tokens, dtype=F32)[:, None] / n_tokens
    f = jnp.linspace(1e-4, bands - 1, bands, dtype=F32)[None, :]
    feat = jnp.concatenate([t, jnp.cos(f * w), -jnp.sin(f * w)], axis=-1)
    kpad = 40
    feat = jnp.pad(feat, ((0, 0), (0, kpad - HYENA_EMB)))
    w1p = jnp.pad(w1, ((0, kpad - HYENA_EMB), (0, 0)))
    min_decay = math.log(HYENA_TARGET) / HYENA_SLOW_PCT
    max_decay = math.log(HYENA_TARGET) / HYENA_FAST_PCT
    deltas = jnp.abs(jnp.linspace(min_decay, max_decay, GROUP, dtype=F32))
    decay = jnp.exp(-t * deltas)
    nf = w3.shape[1]
    decay = jnp.tile(decay, (1, nf // GROUP))
    ffn = w2.shape[0]
    tl = min(n_tokens, 512)
    full = lambda shape: pl.BlockSpec(shape, lambda i: (0, 0))
    return pl.pallas_call(
        _filter_kernel,
        grid=(n_tokens // tl,),
        in_specs=[pl.BlockSpec((tl, kpad), lambda i: (i, 0)),
                  full((kpad, ffn)), full((1, ffn)), full((ffn, ffn)), full((1, ffn)),
                  full((ffn, nf)), full((1, ffn)),
                  pl.BlockSpec((tl, nf), lambda i: (i, 0))],
        out_specs=pl.BlockSpec((tl, nf), lambda i: (i, 0)),
        out_shape=jax.ShapeDtypeStruct((n_tokens, nf), F32),
        compiler_params=_cparams("arbitrary"),
        name="hyena_filter",
    )(feat, w1p, b1[None, :], w2, b2[None, :], w3, freq[None, :], decay)


def reversed_filter_rows(filt, n_tokens):
    order = filt.shape[1] // (2 * GROUP)
    f4 = filt.reshape(n_tokens, order, 2, GROUP)
    hf = f4[:, :, 0]
    hb = f4[:, :, 1]
    rows = jnp.concatenate([hf[::-1], hb[1:], jnp.zeros_like(hf[:1])], axis=0)
    return jnp.transpose(rows, (1, 2, 0))[:, :, None, :]


def _proj_kernel(x_ref, sh_ref, sc_ref, g_ref, w_ref, cos_ref, sin_ref, *out_refs, segs):
    x = x_ref[0]
    ms = jnp.mean(x * x, axis=-1, keepdims=True)
    h = (x * lax.rsqrt(ms + NORM_EPS)) * g_ref[...] * (1.0 + sc_ref[0]) + sh_ref[0]
    hb = h.astype(BF16)
    for o_ref, (start, width, swap_start, scale) in zip(out_refs, segs):
        p = _dot(hb, w_ref[:, start:start + width])
        if swap_start is not None:
            ps = _dot(hb, w_ref[:, swap_start:swap_start + width])
            p = p * cos_ref[...] + ps * sin_ref[...]
        if scale != 1.0:
            p = p * scale
        o_ref[0] = p.astype(o_ref.dtype)


def modulated_projection(x, shift, scale, g, w, segs, out_dtypes, cos_t, sin_t):
    b, l, d = x.shape
    tm = min(l, 512)
    nw = w.shape[1]
    rw = cos_t.shape[1]
    outs = [jax.ShapeDtypeStruct((b, l, s[1]), dt) for s, dt in zip(segs, out_dtypes)]
    return pl.pallas_call(
        functools.partial(_proj_kernel, segs=tuple(segs)),
        grid=(b, l // tm),
        in_specs=[pl.BlockSpec((1, tm, d), lambda i, j: (i, j, 0)),
                  pl.BlockSpec((1, 1, d), lambda i, j: (i, 0, 0)),
                  pl.BlockSpec((1, 1, d), lambda i, j: (i, 0, 0)),
                  pl.BlockSpec((1, d), lambda i, j: (0, 0)),
                  pl.BlockSpec((d, nw), lambda i, j: (0, 0)),
                  pl.BlockSpec((tm, rw), lambda i, j: (j, 0)),
                  pl.BlockSpec((tm, rw), lambda i, j: (j, 0))],
        out_specs=[pl.BlockSpec((1, tm, s[1]), lambda i, j: (i, j, 0)) for s in segs],
        out_shape=outs,
        compiler_params=_cparams("arbitrary", "arbitrary"),
        name="mod_proj",
    )(x, shift, scale, g, w, cos_t, sin_t)


def rope_tables(n_tokens):
    rows = n_tokens // GRID_W
    row = jnp.repeat(jnp.arange(rows, dtype=F32), GRID_W)
    col = jnp.tile(jnp.arange(GRID_W, dtype=F32), rows)
    n_freq = HEAD_QK // 4
    inv = ROPE_BASE ** (-jnp.arange(n_freq, dtype=F32) / n_freq)
    lane = jnp.arange(GROUP)
    d = lane % HEAD_QK
    pos = jnp.where((d < 2 * n_freq)[None, :], row[:, None], col[:, None])
    ang = pos * inv[d % n_freq][None, :]
    first = (d % (2 * n_freq)) < n_freq
    cos_t = jnp.cos(ang)
    sin_t = jnp.where(first[None, :], -jnp.sin(ang), jnp.sin(ang))
    partner = jnp.where(first, lane + n_freq, lane - n_freq)
    return cos_t, sin_t, partner


def _attn_kernel(lam_ref, q_ref, kt_ref, v_ref, g_ref, o_ref, *, out_scale):
    q = q_ref[0]
    kt = kt_ref[0]
    v = v_ref[0]
    lam = lam_ref[0]
    tq = q.shape[0]
    lane = lax.broadcasted_iota(jnp.int32, (1, GROUP), 1)
    map_of_lane = lane // HEAD_QK
    head_of_lane = lane // HEAD_V
    acc = jnp.zeros((tq, GROUP), F32)
    for h in range(N_HEADS):
        maps = []
        for m in range(2):
            keep = jnp.where(map_of_lane == 2 * h + m, 1.0, 0.0).astype(BF16)
            s = _dot(q * keep, kt)
            mx = jnp.max(s, axis=-1, keepdims=True)
            e = jnp.exp(s - mx)
            maps.append((e, 1.0 / jnp.sum(e, axis=-1, keepdims=True)))
        (e0, r0), (e1, r1) = maps
        a = e0 * r0 - e1 * (lam * r1)
        o = _dot(a.astype(BF16), v)
        acc = jnp.where(head_of_lane == h, o, acc)
    sq = acc * acc
    ms = jnp.zeros((tq, GROUP), F32)
    for h in range(N_HEADS):
        hsel = head_of_lane == h
        mh = jnp.sum(jnp.where(hsel, sq, 0.0), axis=-1, keepdims=True) * (1.0 / HEAD_V)
        ms = jnp.where(hsel, mh, ms)
    y = acc * lax.rsqrt(ms + SUBLN_EPS) * g_ref[...] * out_scale
    o_ref[0] = y.astype(o_ref.dtype)


def diff_attention(q, kt, v, lam, subln, out_scale):
    b, nq, _ = q.shape
    nk = v.shape[1]
    tq = min(nq, 256)
    g = jnp.tile(subln, N_HEADS)[None, :]
    return pl.pallas_call(
        functools.partial(_attn_kernel, out_scale=out_scale),
        grid=(b, nq // tq),
        in_specs=[pl.BlockSpec(memory_space=pltpu.SMEM),
                  pl.BlockSpec((1, tq, GROUP), lambda i, j: (i, j, 0)),
                  pl.BlockSpec((1, GROUP, nk), lambda i, j: (i, 0, 0)),
                  pl.BlockSpec((1, nk, GROUP), lambda i, j: (i, 0, 0)),
                  pl.BlockSpec((1, GROUP), lambda i, j: (0, 0))],
        out_specs=pl.BlockSpec((1, tq, GROUP), lambda i, j: (i, j, 0)),
        out_shape=jax.ShapeDtypeStruct((b, nq, GROUP), F32),
        compiler_params=_cparams("arbitrary", "arbitrary"),
        name="diff_attn",
    )(lam.reshape(1), q, kt, v, g)


def _fill_padded(pad_ref, n, write_rows):
    zeros = jnp.zeros((HALO, pad_ref.shape[1]), F32)
    pad_ref[0:HALO, :] = zeros
    pad_ref[HALO + n:HALO + n + HALO, :] = zeros
    write_rows()


def _shifted_rows(pad_ref, r0, ch, offsets):
    rows = ch + 2 * HALO
    blk = pad_ref[pl.ds(r0, rows), :]
    rolled = {0: blk}
    out = {}
    for k in offsets:
        o = HALO + k
        r = o % SUBLANES
        if r not in rolled:
            rolled[r] = pltpu.roll(blk, rows - r, 0)
        out[k] = rolled[r][o - r:o - r + ch]
    return out


def _pool_kernel(u_ref, w_ref, sc_ref, o_ref, pad_ref, *, n, ch):
    def copy(i, c):
        r0 = pl.multiple_of(i * ch, ch)
        pad_ref[pl.ds(HALO + r0, ch), :] = u_ref[0, pl.ds(r0, ch), :]
        return c
    _fill_padded(pad_ref, n, lambda: lax.fori_loop(0, n // ch, copy, 0))
    lane = lax.broadcasted_iota(jnp.int32, (1, GROUP), 1)
    pg = GROUP // len(POOL_WINDOWS)
    reach = POOL_WINDOWS[-1] // 2

    def body(i, c):
        r0 = pl.multiple_of(i * ch, ch)
        taps = _shifted_rows(pad_ref, r0, ch, range(-reach, reach))
        ld = lambda k: taps[k]
        t = r0 + lax.broadcasted_iota(jnp.int32, (ch, 1), 0)
        u = ld(0)
        sums = []
        s = u + ld(-1)
        sums.append(s)
        half = 1
        for _ in POOL_WINDOWS[1:]:
            for k in range(half, 2 * half):
                s = s + ld(k) + ld(-k - 1)
            half *= 2
            sums.append(s)
        pooled = None
        for gi, win in reversed(list(enumerate(POOL_WINDOWS))):
            hi = jnp.minimum(t + (win - win // 2), n)
            lo = jnp.maximum(t - win // 2, 0)
            mean = sums[gi] / (hi - lo).astype(F32)
            pooled = mean if pooled is None else jnp.where(lane < (gi + 1) * pg, mean, pooled)
        pooled = pooled - u
        y = _dot(pooled.astype(BF16), w_ref[...]) * sc_ref[...]
        o_ref[0, pl.ds(r0, ch), :] = y.astype(o_ref.dtype)
        return c
    lax.fori_loop(0, n // ch, body, 0)


def pool_mixer(u, w_pool, scale):
    b, n, c = u.shape
    ng, pg, _ = w_pool.shape
    wbd = jnp.zeros((c, c), F32)
    for gi in range(ng):
        wbd = wbd.at[gi * pg:(gi + 1) * pg, gi * pg:(gi + 1) * pg].set(w_pool[gi])
    ch = min(n, 256)
    return pl.pallas_call(
        functools.partial(_pool_kernel, n=n, ch=ch),
        grid=(b,),
        in_specs=[pl.BlockSpec((1, n, c), lambda i: (i, 0, 0)),
                  pl.BlockSpec((c, c), lambda i: (0, 0)),
                  pl.BlockSpec((1, c), lambda i: (0, 0))],
        out_specs=pl.BlockSpec((1, n, c), lambda i: (i, 0, 0)),
        out_shape=jax.ShapeDtypeStruct((b, n, c), F32),
        scratch_shapes=[pltpu.VMEM((n + 2 * HALO, c), F32)],
        compiler_params=_cparams("arbitrary"),
        name="pool_mixer",
    )(u, wbd.astype(BF16), scale[None, :])


def _conformer_kernel(p_ref, dw_ref, dwb_ref, lg_ref, lb_ref, pw_ref, pwb_ref, o_ref, pad_ref, *, n, ch):
    c = GROUP

    def glu(i, carry):
        r0 = pl.multiple_of(i * ch, ch)
        a = p_ref[0, pl.ds(r0, ch), 0:c]
        g = p_ref[0, pl.ds(r0, ch), c:2 * c]
        pad_ref[pl.ds(HALO + r0, ch), :] = a * _sigmoid(g)
        return carry
    _fill_padded(pad_ref, n, lambda: lax.fori_loop(0, n // ch, glu, 0))
    half = (CONV_KERNEL - 1) // 2

    def body(i, carry):
        r0 = pl.multiple_of(i * ch, ch)
        acc = jnp.zeros((ch, c), F32) + dwb_ref[...]
        taps = _shifted_rows(pad_ref, r0, ch, range(-half, half + 1))
        for k in range(CONV_KERNEL):
            acc = acc + taps[k - half] * dw_ref[k:k + 1, :]
        mu = jnp.mean(acc, axis=-1, keepdims=True)
        xc = acc - mu
        var = jnp.mean(xc * xc, axis=-1, keepdims=True)
        y = xc * lax.rsqrt(var + LN_EPS) * lg_ref[...] + lb_ref[...]
        y = y * _sigmoid(y)
        out = _dot(y.astype(BF16), pw_ref[...]) + pwb_ref[...]
        o_ref[0, pl.ds(r0, ch), :] = out.astype(o_ref.dtype)
        return carry
    lax.fori_loop(0, n // ch, body, 0)


def conformer_mixer(p, dw_w, dw_b, ln_g, ln_b, pw_w, pw_b):
    b, n, c2 = p.shape
    c = c2 // 2
    ch = min(n, 128)
    kp = 32
    dw = jnp.pad(dw_w, ((0, kp - CONV_KERNEL), (0, 0)))
    vec = lambda: pl.BlockSpec((1, c), lambda i: (0, 0))
    return pl.pallas_call(
        functools.partial(_conformer_kernel, n=n, ch=ch),
        grid=(b,),
        in_specs=[pl.BlockSpec((1, n, c2), lambda i: (i, 0, 0)),
                  pl.BlockSpec((kp, c), lambda i: (0, 0)),
                  vec(), vec(), vec(),
                  pl.BlockSpec((c, c), lambda i: (0, 0)),
                  vec()],
        out_specs=pl.BlockSpec((1, n, c), lambda i: (i, 0, 0)),
        out_shape=jax.ShapeDtypeStruct((b, n, c), F32),
        scratch_shapes=[pltpu.VMEM((n + 2 * HALO, c), F32)],
        compiler_params=_cparams("arbitrary"),
        name="conformer_mixer",
    )(p, dw, dw_b[None, :], ln_g[None, :], ln_b[None, :], pw_w.astype(BF16), pw_b[None, :])


def _short_conv_kernel(p_ref, w_ref, b_ref, o_ref, pad_ref, *, n, ch):
    def copy(i, carry):
        r0 = pl.multiple_of(i * ch, ch)
        pad_ref[pl.ds(HALO + r0, ch), :] = p_ref[0, pl.ds(r0, ch), :]
        return carry
    _fill_padded(pad_ref, n, lambda: lax.fori_loop(0, n // ch, copy, 0))
    half = (HYENA_SHORT - 1) // 2

    def body(i, carry):
        r0 = pl.multiple_of(i * ch, ch)
        acc = jnp.zeros((ch, pad_ref.shape[1]), F32) + b_ref[...]
        taps = _shifted_rows(pad_ref, r0, ch, range(-half, half + 1))
        for k in range(HYENA_SHORT):
            acc = acc + taps[k - half] * w_ref[k:k + 1, :]
        o_ref[0, pl.ds(r0, ch), :] = acc.astype(o_ref.dtype)
        return carry
    lax.fori_loop(0, n // ch, body, 0)


def hyena_short_conv(p, w, bias):
    b, n, c = p.shape
    ch = min(n, 128)
    wp = jnp.pad(w, ((0, SUBLANES - HYENA_SHORT), (0, 0)))
    return pl.pallas_call(
        functools.partial(_short_conv_kernel, n=n, ch=ch),
        grid=(b,),
        in_specs=[pl.BlockSpec((1, n, c), lambda i: (i, 0, 0)),
                  pl.BlockSpec((SUBLANES, c), lambda i: (0, 0)),
                  pl.BlockSpec((1, c), lambda i: (0, 0))],
        out_specs=pl.BlockSpec((1, n, c), lambda i: (i, 0, 0)),
        out_shape=jax.ShapeDtypeStruct((b, n, c), BF16),
        scratch_shapes=[pltpu.VMEM((n + 2 * HALO, c), F32)],
        compiler_params=_cparams("arbitrary"),
        name="hyena_short_conv",
    )(p, wp, bias[None, :])


def _toeplitz_conv(g_ref, w, nb, tb, bsz):
    nbb = nb * bsz
    lane = lax.broadcasted_iota(jnp.int32, (1, nbb), 1)
    y = jnp.zeros((tb, nbb), F32)
    for d in range(-(nb - 1), nb):
        start = tb * (nb - 1 - d)
        win = jnp.broadcast_to(g_ref[:, start:start + 2 * tb], (tb, 2 * tb))
        rolled = pltpu.roll(win, 1, 1, stride=1, stride_axis=0)
        blk = rolled[:, tb:2 * tb].astype(BF16)
        o = _dot(blk, w)
        if d != 0:
            o = pltpu.roll(o, (d * bsz) % nbb, 1)
            keep = lane >= d * bsz if d > 0 else lane < (nb + d) * bsz
            o = jnp.where(keep, o, 0.0)
        y = y + o
    return y


def _hyena_kernel(bias_ref, g_ref, v_ref, x1_ref, x2_ref, o_ref, *, nb, tb, bsz, cb):
    base = pl.program_id(0) * cb

    def body(ci, carry):
        v = v_ref[ci]
        y1 = _toeplitz_conv(g_ref.at[0, ci], v, nb, tb, bsz)
        z = x1_ref[ci].astype(F32) * (y1 + bias_ref[0, base + ci] * v.astype(F32))
        y2 = _toeplitz_conv(g_ref.at[1, ci], z.astype(BF16), nb, tb, bsz)
        out = x2_ref[ci].astype(F32) * (y2 + bias_ref[1, base + ci] * z)
        o_ref[ci] = out.astype(o_ref.dtype)
        return carry
    lax.fori_loop(0, cb, body, 0)


def hyena_long_conv(u, grev, bias):
    b, n, _ = u.shape
    tb = min(n, 256)
    nb = n // tb
    nbb = nb * b
    ut = jnp.transpose(u.reshape(b, nb, tb, 3, GROUP), (3, 4, 2, 1, 0)).reshape(3, GROUP, tb, nbb)
    cb = 8
    blk = lambda part: pl.BlockSpec((None, cb, tb, nbb), lambda i, part=part: (part, i, 0, 0))
    out = pl.pallas_call(
        functools.partial(_hyena_kernel, nb=nb, tb=tb, bsz=b, cb=cb),
        grid=(GROUP // cb,),
        in_specs=[pl.BlockSpec(memory_space=pltpu.SMEM),
                  pl.BlockSpec((2, cb, 1, 2 * n), lambda i: (0, i, 0, 0)),
                  blk(0), blk(1), blk(2)],
        out_specs=pl.BlockSpec((cb, tb, nbb), lambda i: (i, 0, 0)),
        out_shape=jax.ShapeDtypeStruct((GROUP, tb, nbb), F32),
        compiler_params=_cparams("arbitrary"),
        name="hyena_long_conv",
    )(bias, grev, ut, ut, ut)
    return jnp.transpose(out.reshape(GROUP, tb, nb, b), (3, 2, 1, 0)).reshape(b, n, GROUP)


def _out_proj_kernel(x_ref, gt_ref, a_ref, h_ref, p_ref, c_ref, w_ref, o_ref):
    acc = _dot(a_ref[0].astype(BF16), w_ref[0:GROUP, :])
    acc = acc + _dot(h_ref[0].astype(BF16), w_ref[GROUP:2 * GROUP, :])
    acc = acc + _dot(p_ref[0].astype(BF16), w_ref[2 * GROUP:3 * GROUP, :])
    acc = acc + _dot(c_ref[0].astype(BF16), w_ref[3 * GROUP:4 * GROUP, :])
    o_ref[0] = x_ref[0] + gt_ref[0] * acc


def out_projection(x, gate, parts, w_out):
    b, l, d = x.shape
    tm = min(l, 512)
    part = lambda: pl.BlockSpec((1, tm, GROUP), lambda i, j: (i, j, 0))
    return pl.pallas_call(
        _out_proj_kernel,
        grid=(b, l // tm),
        in_specs=[pl.BlockSpec((1, tm, d), lambda i, j: (i, j, 0)),
                  pl.BlockSpec((1, 1, d), lambda i, j: (i, 0, 0)),
                  part(), part(), part(), part(),
                  pl.BlockSpec(w_out.shape, lambda i, j: (0, 0))],
        out_specs=pl.BlockSpec((1, tm, d), lambda i, j: (i, j, 0)),
        out_shape=jax.ShapeDtypeStruct((b, l, d), F32),
        compiler_params=_cparams("arbitrary", "arbitrary"),
        name="out_proj",
    )(x, gate, *parts, w_out.astype(BF16))


def _router_kernel(x_ref, sh_ref, sc_ref, g_ref, wh_ref, wl_ref, rb_ref, h_ref, idx_ref, gate_ref):
    x = x_ref[0]
    ms = jnp.mean(x * x, axis=-1, keepdims=True)
    h = (x * lax.rsqrt(ms + NORM_EPS)) * g_ref[...] * (1.0 + sc_ref[0]) + sh_ref[0]
    h_ref[0] = h
    hh, hl = _split_bf16(h)
    logits = _dot(hh, wh_ref[...]) + (_dot(hh, wl_ref[...]) + _dot(hl, wh_ref[...])) + rb_ref[...]
    tm = x.shape[0]
    lane = lax.broadcasted_iota(jnp.int32, (tm, N_EXPERTS), 1).astype(F32)
    out_lane = lax.broadcasted_iota(jnp.int32, (tm, LANES), 1)
    vals, idxs = [], []
    cur = logits
    for _ in range(TOP_K):
        mx = jnp.max(cur, axis=-1, keepdims=True)
        ix = jnp.min(jnp.where(cur == mx, lane, float(N_EXPERTS)), axis=-1, keepdims=True)
        vals.append(mx)
        idxs.append(ix)
        cur = jnp.where(lane == ix, -jnp.inf, cur)
    es = [jnp.exp(v - vals[0]) for v in vals]
    inv = 1.0 / (es[0] + es[1] + es[2] + es[3])
    idx_out = jnp.zeros((tm, LANES), F32)
    gate_out = jnp.zeros((tm, LANES), F32)
    for k in range(TOP_K):
        idx_out = jnp.where(out_lane == k, idxs[k], idx_out)
        gate_out = jnp.where(out_lane == k, es[k] * inv, gate_out)
    idx_ref[0] = idx_out.astype(jnp.int32)
    gate_ref[0] = gate_out


def router(x, shift, scale, g, router_w, router_b):
    b, l, d = x.shape
    tm = min(l, 512)
    wh, wl = _split_bf16(router_w)
    vec3 = lambda: pl.BlockSpec((1, 1, d), lambda i, j: (i, 0, 0))
    tile = lambda w: pl.BlockSpec((1, tm, w), lambda i, j: (i, j, 0))
    h, idx, gates = pl.pallas_call(
        _router_kernel,
        grid=(b, l // tm),
        in_specs=[tile(d), vec3(), vec3(),
                  pl.BlockSpec((1, d), lambda i, j: (0, 0)),
                  pl.BlockSpec(wh.shape, lambda i, j: (0, 0)),
                  pl.BlockSpec(wl.shape, lambda i, j: (0, 0)),
                  pl.BlockSpec((1, N_EXPERTS), lambda i, j: (0, 0))],
        out_specs=[tile(d), tile(LANES), tile(LANES)],
        out_shape=[jax.ShapeDtypeStruct((b, l, d), F32),
                   jax.ShapeDtypeStruct((b, l, LANES), jnp.int32),
                   jax.ShapeDtypeStruct((b, l, LANES), F32)],
        compiler_params=_cparams("arbitrary", "arbitrary"),
        name="moe_router",
    )(x, shift, scale, g, wh, wl, router_b[None, :])
    return h, idx[..., :TOP_K], gates[..., :TOP_K]


GATHER_RING = 16


def _dispatch_kernel(tok_ref, src_ref, dst_ref, sem, *, rows):
    base = pl.program_id(0) * rows

    def copy(p):
        return pltpu.make_async_copy(src_ref.at[pl.ds(tok_ref[p], 1)], dst_ref.at[pl.ds(base + p, 1)], sem)

    def body(p, carry):
        copy(p).start()

        @pl.when(p >= GATHER_RING)
        def _():
            copy(p - GATHER_RING).wait()
        return carry
    lax.fori_loop(0, rows, body, 0)

    def drain(p, carry):
        copy(p).wait()
        return carry
    lax.fori_loop(rows - GATHER_RING, rows, drain, 0)


def dispatch_rows(h, tok, n_slots):
    rows = 2048
    return pl.pallas_call(
        functools.partial(_dispatch_kernel, rows=rows),
        grid=(n_slots // rows,),
        in_specs=[pl.BlockSpec((rows,), lambda i: (i,), memory_space=pltpu.SMEM),
                  pl.BlockSpec(memory_space=pl.ANY)],
        out_specs=pl.BlockSpec(memory_space=pl.ANY),
        out_shape=jax.ShapeDtypeStruct((n_slots, h.shape[1]), h.dtype),
        scratch_shapes=[pltpu.SemaphoreType.DMA(())],
        compiler_params=_cparams("arbitrary"),
        name="moe_dispatch",
    )(tok, h)


def _expert_kernel(te_ref, tv_ref, x_ref, w1_ref, b1_ref, w2_ref, b2_ref, o_ref):
    i = pl.program_id(0)
    ff = w2_ref.shape[1]

    @pl.when(tv_ref[i] > 0)
    def _():
        x = x_ref[...].astype(BF16)
        hid = _dot(x, w1_ref[0]) + b1_ref[0]
        gate = jnp.minimum(hid[:, :ff], SWIGLU_LIMIT)
        up = jnp.clip(hid[:, ff:], -SWIGLU_LIMIT, SWIGLU_LIMIT)
        act = gate * _sigmoid(SWIGLU_ALPHA * gate) * (up + 1.0)
        o_ref[...] = _dot(act.astype(BF16), w2_ref[0]) + b2_ref[0]

    @pl.when(tv_ref[i] == 0)
    def _():
        o_ref[...] = jnp.zeros_like(o_ref)


def expert_ffn(xs, tile_expert, tile_valid, w1, b1, w2, b2, tm):
    n_slots, d = xs.shape
    ne, _, ff2 = w1.shape
    ff = w2.shape[1]
    grid_spec = pltpu.PrefetchScalarGridSpec(
        num_scalar_prefetch=2,
        grid=(n_slots // tm,),
        in_specs=[pl.BlockSpec((tm, d), lambda i, te, tv: (i, 0)),
                  pl.BlockSpec((1, d, ff2), lambda i, te, tv: (te[i], 0, 0)),
                  pl.BlockSpec((1, 1, ff2), lambda i, te, tv: (te[i], 0, 0)),
                  pl.BlockSpec((1, ff, d), lambda i, te, tv: (te[i], 0, 0)),
                  pl.BlockSpec((1, 1, d), lambda i, te, tv: (te[i], 0, 0))],
        out_specs=pl.BlockSpec((tm, d), lambda i, te, tv: (i, 0)))
    return pl.pallas_call(
        _expert_kernel,
        grid_spec=grid_spec,
        out_shape=jax.ShapeDtypeStruct((n_slots, d), F32),
        compiler_params=_cparams("arbitrary"),
        name="moe_experts",
    )(tile_expert, tile_valid, xs, w1, b1.reshape(ne, 1, ff2), w2, b2.reshape(ne, 1, d))


def _combine_kernel(pos_ref, x_ref, gt_ref, gates_ref, fin_ref, ys_ref, o_ref, buf, sem, *, tc, final):
    n = tc * TOP_K

    def copy(r):
        return pltpu.make_async_copy(ys_ref.at[pl.ds(pos_ref[r], 1)], buf.at[pl.ds(r, 1)], sem)

    def issue(r, carry):
        copy(r).start()
        return carry
    lax.fori_loop(0, n, issue, 0)

    def drain(r, carry):
        copy(r).wait()
        return carry
    lax.fori_loop(0, n, drain, 0)
    gates = gates_ref[...]
    y = jnp.zeros((tc, buf.shape[1]), F32)
    for k in range(TOP_K):
        y = y + gates[:, k:k + 1] * buf[k * tc:(k + 1) * tc, :]
    out = x_ref[...] + gt_ref[0] * y
    if final:
        ms = jnp.mean(out * out, axis=-1, keepdims=True)
        out = out * lax.rsqrt(ms + NORM_EPS) * fin_ref[...]
    o_ref[...] = out


def combine_rows(x, gate_vec, gates, pos, ys, final_g, final):
    b, l, d = x.shape
    t = b * l
    tc = min(l, 256)
    steps = t // tc
    per_seq = l // tc
    pos_t = jnp.transpose(pos.reshape(steps, tc, TOP_K), (0, 2, 1)).reshape(steps * tc * TOP_K)
    gates_p = jnp.pad(gates.reshape(t, TOP_K), ((0, 0), (0, LANES - TOP_K)))
    out = pl.pallas_call(
        functools.partial(_combine_kernel, tc=tc, final=final),
        grid=(steps,),
        in_specs=[pl.BlockSpec((tc * TOP_K,), lambda i: (i,), memory_space=pltpu.SMEM),
                  pl.BlockSpec((tc, d), lambda i: (i, 0)),
                  pl.BlockSpec((1, 1, d), lambda i: (i // per_seq, 0, 0)),
                  pl.BlockSpec((tc, LANES), lambda i: (i, 0)),
                  pl.BlockSpec((1, d), lambda i: (0, 0)),
                  pl.BlockSpec(memory_space=pl.ANY)],
        out_specs=pl.BlockSpec((tc, d), lambda i: (i, 0)),
        out_shape=jax.ShapeDtypeStruct((t, d), F32),
        scratch_shapes=[pltpu.VMEM((tc * TOP_K, d), F32), pltpu.SemaphoreType.DMA(())],
        compiler_params=_cparams("arbitrary"),
        name="moe_combine",
    )(pos_t, x.reshape(t, d), gate_vec, gates_p, final_g[None, :], ys)
    return out.reshape(b, l, d)


def routing_plan(idx, tm):
    t = idx.shape[0]
    n = t * TOP_K
    e = idx.reshape(n)
    onehot = (e[:, None] == jnp.arange(N_EXPERTS, dtype=e.dtype)[None, :]).astype(jnp.int32)
    csum = jnp.cumsum(onehot, axis=0)
    counts = csum[-1]
    rank = jnp.take_along_axis(csum, e[:, None], axis=1)[:, 0] - 1
    padded = ((counts + tm - 1) // tm) * tm
    ends = jnp.cumsum(padded)
    starts = ends - padded
    pos = starts[e] + rank
    gather_rows = 2048
    unit = tm * gather_rows // math.gcd(tm, gather_rows)
    n_slots = -(-(n + N_EXPERTS * (tm - 1)) // unit) * unit
    tok = jnp.zeros((n_slots,), jnp.int32).at[pos].set(jnp.arange(n, dtype=jnp.int32) // TOP_K)
    tile_start = jnp.arange(n_slots // tm, dtype=jnp.int32) * tm
    tile_expert = jnp.minimum(jnp.searchsorted(ends, tile_start, side="right"), N_EXPERTS - 1).astype(jnp.int32)
    tile_valid = (tile_start < ends[-1]).astype(jnp.int32)
    return pos.reshape(t, TOP_K).astype(jnp.int32), tok, tile_expert, tile_valid


MOE_TILE = 512


def moe_block(x, shift, scale, gate_vec, g, router_w, router_b, w1, b1, w2, b2, final_g, final):
    b, l, d = x.shape
    h, idx, gates = router(x, shift, scale, g, router_w, router_b)
    pos, tok, tile_expert, tile_valid = routing_plan(idx.reshape(b * l, TOP_K), MOE_TILE)
    xs = dispatch_rows(h.reshape(b * l, d), tok, tok.shape[0])
    ys = expert_ffn(xs, tile_expert, tile_valid, w1, b1, w2, b2, MOE_TILE)
    return combine_rows(x, gate_vec, gates, pos, ys, final_g, final)


def _diff_lambda(lp, lam_init):
    lp = lp.astype(F32)
    return jnp.exp(jnp.sum(lp[0] * lp[1])) - jnp.exp(jnp.sum(lp[2] * lp[3])) + lam_init


def kernel(x, c, ctx, c_ctx, norm1, norm2, w_mod, b_mod, w_in, w_out, attn_lambda, attn_subln,
           hyena_short_w, hyena_short_b, hyena_w1, hyena_b1, hyena_w2, hyena_b2, hyena_w3, hyena_freq,
           hyena_bias, pool_w, pool_scale, conv_dw_w, conv_dw_b, conv_ln_g, conv_ln_b, conv_pw_w,
           conv_pw_b, router_w, router_b, moe_w1, moe_b1, moe_w2, moe_b2, final_norm):
    bsz, n_lat, d = x.shape
    n_ctx = ctx.shape[1]
    depth = w_mod.shape[0]
    g = GROUP
    off_hy, off_pool, off_conv = 3 * g, 6 * g, 7 * g

    rows = -(-(bsz + 1) // SUBLANES) * SUBLANES
    cvec = jnp.zeros((rows, d), F32).at[:bsz].set(c).at[bsz].set(c_ctx)
    mods = adaln_all(cvec, w_mod, b_mod)

    cos_t, sin_t, partner = rope_tables(n_lat)
    ones_t = jnp.ones((n_ctx, g), F32)
    zeros_t = jnp.zeros((n_ctx, g), F32)
    qk_scale = HEAD_QK ** -0.5

    xc = ctx
    for i in range(depth):
        last = i == depth - 1
        m = mods[i]
        lat = [m[:bsz, None, k * d:(k + 1) * d] for k in range(6)]
        cm = [jnp.broadcast_to(m[bsz, k * d:(k + 1) * d][None, None, :], (bsz, 1, d)) for k in range(6)]
        lam_init = 0.8 - 0.6 * math.exp(-0.3 * i)
        lam = _diff_lambda(attn_lambda[i], lam_init)
        wi = w_in[i]
        g1 = norm1[i][None, :]
        g2 = norm2[i][None, :]
        w1b = moe_w1[i].astype(BF16)
        w2b = moe_w2[i].astype(BF16)

        w_lat = jnp.concatenate([wi, wi[:, partner], wi[:, g + partner]], axis=1).astype(BF16)
        nin = wi.shape[1]
        segs = [(0, g, nin, qk_scale), (g, g, nin + g, 1.0), (2 * g, g, None, 1.0),
                (off_hy, 3 * g, None, 1.0), (off_pool, g, None, 1.0), (off_conv, 2 * g, None, 1.0)]
        q, k, v, p_hy, p_pool, p_conv = modulated_projection(
            x, lat[0], lat[1], g1, w_lat, segs, [BF16, BF16, BF16, F32, F32, F32], cos_t, sin_t)

        if last:
            w_ctx = wi[:, g:3 * g].astype(BF16)
            kc, vc = modulated_projection(xc, cm[0], cm[1], g1, w_ctx,
                                          [(0, g, None, 1.0), (g, g, None, 1.0)], [BF16, BF16], ones_t, zeros_t)
        else:
            csegs = [(0, g, None, qk_scale), (g, g, None, 1.0), (2 * g, g, None, 1.0),
                     (off_hy, 3 * g, None, 1.0), (off_pool, g, None, 1.0), (off_conv, 2 * g, None, 1.0)]
            qc, kc, vc, pc_hy, pc_pool, pc_conv = modulated_projection(
                xc, cm[0], cm[1], g1, wi.astype(BF16), csegs, [BF16, BF16, BF16, F32, F32, F32], ones_t, zeros_t)

        kt_all = jnp.transpose(jnp.concatenate([kc, k], axis=1), (0, 2, 1))
        v_all = jnp.concatenate([vc, v], axis=1)
        o_attn = diff_attention(q, kt_all, v_all, lam, attn_subln[i], 1.0 - lam_init)

        filt_params = (hyena_w1[i], hyena_b1[i], hyena_w2[i], hyena_b2[i], hyena_w3[i], hyena_freq[i])

        def local_groups(p_hy_, p_pool_, p_conv_, n_tokens):
            grev = reversed_filter_rows(hyena_filters(n_tokens, *filt_params), n_tokens)
            u = hyena_short_conv(p_hy_, hyena_short_w[i], hyena_short_b[i])
            o_hy = hyena_long_conv(u, grev, hyena_bias[i])
            o_pool = pool_mixer(p_pool_, pool_w[i], pool_scale[i])
            o_conv = conformer_mixer(p_conv_, conv_dw_w[i], conv_dw_b[i], conv_ln_g[i], conv_ln_b[i],
                                     conv_pw_w[i], conv_pw_b[i])
            return o_hy, o_pool, o_conv

        o_hy, o_pool, o_conv = local_groups(p_hy, p_pool, p_conv, n_lat)
        x = out_projection(x, lat[2], (o_attn, o_hy, o_pool, o_conv), w_out[i])
        if not last:
            oc_attn = diff_attention(qc, jnp.transpose(kc, (0, 2, 1)), vc, lam, attn_subln[i], 1.0 - lam_init)
            oc_hy, oc_pool, oc_conv = local_groups(pc_hy, pc_pool, pc_conv, n_ctx)
            xc = out_projection(xc, cm[2], (oc_attn, oc_hy, oc_pool, oc_conv), w_out[i])

        x = moe_block(x, lat[3], lat[4], lat[5], g2, router_w[i], router_b[i], w1b, moe_b1[i], w2b, moe_b2[i],
                      final_norm, last)
        if not last:
            xc = moe_block(xc, cm[3], cm[4], cm[5], g2, router_w[i], router_b[i], w1b, moe_b1[i], w2b, moe_b2[i],
                           final_norm, False)
    return x
```

```python
import functools
import math

import jax
import jax.numpy as jnp
from jax import lax
from jax.experimental import pallas as pl
from jax.experimental.pallas import tpu as pltpu

F32 = jnp.float32
BF16 = jnp.bfloat16

GRID_W = 64
N_HEADS = 4
HEAD_V = 64
HEAD_QK = 32
GROUP = 256
ROPE_BASE = 10000.0
SUBLN_EPS = 1e-5
NORM_EPS = 1e-6
LN_EPS = 1e-5
POOL_WINDOWS = (2, 4, 8, 16)
CONV_KERNEL = 31
HYENA_SHORT = 3
HYENA_EMB = 33
HYENA_TARGET = 1e-2
HYENA_FAST_PCT = 0.3
HYENA_SLOW_PCT = 1.5
N_EXPERTS = 32
TOP_K = 4
SWIGLU_ALPHA = 1.702
SWIGLU_LIMIT = 7.0

LANES = 128
SUBLANES = 8
VMEM_LIMIT = 56 * 1024 * 1024
HALO = 16


def _cparams(*sem):
    return pltpu.CompilerParams(dimension_semantics=sem, vmem_limit_bytes=VMEM_LIMIT)


def _split_bf16(a):
    hi = a.astype(BF16)
    lo = (a - hi.astype(F32)).astype(BF16)
    return hi, lo


def _dot(a, b):
    return jnp.dot(a, b, preferred_element_type=F32)


def _dot3(a, b):
    ah, al = _split_bf16(a)
    bh, bl = _split_bf16(b)
    return _dot(ah, bh) + (_dot(ah, bl) + _dot(al, bh))


def _sigmoid(x):
    return 1.0 / (1.0 + jnp.exp(-x))


def _adaln_kernel(c_ref, w_ref, b_ref, o_ref):
    cv = c_ref[...]
    s = cv * _sigmoid(cv)
    o_ref[0] = _dot3(s, w_ref[0]) + b_ref[0]


def adaln_all(cvec, w_mod, b_mod):
    depth, d, n = w_mod.shape
    r = cvec.shape[0]
    tn = 1536
    return pl.pallas_call(
        _adaln_kernel,
        grid=(depth, n // tn),
        in_specs=[pl.BlockSpec((r, d), lambda i, j: (0, 0)),
                  pl.BlockSpec((1, d, tn), lambda i, j: (i, 0, j)),
                  pl.BlockSpec((1, 1, tn), lambda i, j: (i, 0, j))],
        out_specs=pl.BlockSpec((1, r, tn), lambda i, j: (i, 0, j)),
        out_shape=jax.ShapeDtypeStruct((depth, r, n), F32),
        compiler_params=_cparams("arbitrary", "arbitrary"),
        name="adaln",
    )(cvec, w_mod, b_mod.reshape(depth, 1, n))


def _filter_kernel(feat_ref, w1_ref, b1_ref, w2_ref, b2_ref, w3_ref, fr_ref, dec_ref, o_ref):
    fr = fr_ref[...]
    z = jnp.sin(fr * (_dot3(feat_ref[...], w1_ref[...]) + b1_ref[...]))
    z = jnp.sin(fr * (_dot3(z, w2_ref[...]) + b2_ref[...]))
    o_ref[...] = _dot3(z, w3_ref[...]) * dec_ref[...]


def hyena_filters(n_tokens, w1, b1, w2, b2, w3, freq):
    bands = (HYENA_EMB - 1) // 2
    t = jnp.linspace(0.0, 1.0, n_tokens, dtype=F32)[:, None]
    w = 2.0 * math.pi * jnp.arange(n_tokens, dtype=F32)[:, None] / n_tokens
    f = jnp.linspace(1e-4, bands - 1, bands, dtype=F32)[None, :]
    feat = jnp.concatenate([t, jnp.cos(f * w), -jnp.sin(f * w)], axis=-1)
    kpad = 40
    feat = jnp.pad(feat, ((0, 0), (0, kpad - HYENA_EMB)))
    w1p = jnp.pad(w1, ((0, kpad - HYENA_EMB), (0, 0)))
    min_decay = math.log(HYENA_TARGET) / HYENA_SLOW_PCT
    max_decay = math.log(HYENA_TARGET) / HYENA_FAST_PCT
    deltas = jnp.abs(jnp.linspace(min_decay, max_decay, GROUP, dtype=F32))
    decay = jnp.exp(-t * deltas)
    nf = w3.shape[1]
    decay = jnp.tile(decay, (1, nf // GROUP))
    ffn = w2.shape[0]
    tl = min(n_tokens, 512)
    full = lambda shape: pl.BlockSpec(shape, lambda i: (0, 0))
    return pl.pallas_call(
        _filter_kernel,
        grid=(n_tokens // tl,),
        in_specs=[pl.BlockSpec((tl, kpad), lambda i: (i, 0)),
                  full((kpad, ffn)), full((1, ffn)), full((ffn, ffn)), full((1, ffn)),
                  full((ffn, nf)), full((1, ffn)),
                  pl.BlockSpec((tl, nf), lambda i: (i, 0))],
        out_specs=pl.BlockSpec((tl, nf), lambda i: (i, 0)),
        out_shape=jax.ShapeDtypeStruct((n_tokens, nf), F32),
        compiler_params=_cparams("arbitrary"),
        name="hyena_filter",
    )(feat, w1p, b1[None, :], w2, b2[None, :], w3, freq[None, :], decay)


def filter_rows(filt, n_tokens):
    order = filt.shape[1] // (2 * GROUP)
    f4 = filt.reshape(n_tokens, order, 2, GROUP)
    hf = f4[:, :, 0]
    hb = f4[:, :, 1]
    rows = jnp.concatenate([jnp.zeros_like(hf[:1]), hb[:0:-1], hf], axis=0)
    return jnp.transpose(rows, (1, 2, 0))[:, :, None, :]


def _proj_kernel(x_ref, sh_ref, sc_ref, g_ref, w_ref, cos_ref, sin_ref, *out_refs, segs):
    x = x_ref[0]
    ms = jnp.mean(x * x, axis=-1, keepdims=True)
    h = (x * lax.rsqrt(ms + NORM_EPS)) * g_ref[...] * (1.0 + sc_ref[0]) + sh_ref[0]
    hb = h.astype(BF16)
    for o_ref, (start, width, swap_start, scale) in zip(out_refs, segs):
        p = _dot(hb, w_ref[:, start:start + width])
        if swap_start is not None:
            ps = _dot(hb, w_ref[:, swap_start:swap_start + width])
            p = p * cos_ref[...] + ps * sin_ref[...]
        if scale != 1.0:
            p = p * scale
        o_ref[0] = p.astype(o_ref.dtype)


def modulated_projection(x, shift, scale, g, w, segs, out_dtypes, cos_t, sin_t):
    b, l, d = x.shape
    tm = min(l, 512)
    nw = w.shape[1]
    rw = cos_t.shape[1]
    outs = [jax.ShapeDtypeStruct((b, l, s[1]), dt) for s, dt in zip(segs, out_dtypes)]
    return pl.pallas_call(
        functools.partial(_proj_kernel, segs=tuple(segs)),
        grid=(b, l // tm),
        in_specs=[pl.BlockSpec((1, tm, d), lambda i, j: (i, j, 0)),
                  pl.BlockSpec((1, 1, d), lambda i, j: (i, 0, 0)),
                  pl.BlockSpec((1, 1, d), lambda i, j: (i, 0, 0)),
                  pl.BlockSpec((1, d), lambda i, j: (0, 0)),
                  pl.BlockSpec((d, nw), lambda i, j: (0, 0)),
                  pl.BlockSpec((tm, rw), lambda i, j: (j, 0)),
                  pl.BlockSpec((tm, rw), lambda i, j: (j, 0))],
        out_specs=[pl.BlockSpec((1, tm, s[1]), lambda i, j: (i, j, 0)) for s in segs],
        out_shape=outs,
        compiler_params=_cparams("arbitrary", "arbitrary"),
        name="mod_proj",
    )(x, shift, scale, g, w, cos_t, sin_t)


def rope_tables(n_tokens):
    rows = n_tokens // GRID_W
    row = jnp.repeat(jnp.arange(rows, dtype=F32), GRID_W)
    col = jnp.tile(jnp.arange(GRID_W, dtype=F32), rows)
    n_freq = HEAD_QK // 4
    inv = ROPE_BASE ** (-jnp.arange(n_freq, dtype=F32) / n_freq)
    lane = jnp.arange(GROUP)
    d = lane % HEAD_QK
    pos = jnp.where((d < 2 * n_freq)[None, :], row[:, None], col[:, None])
    ang = pos * inv[d % n_freq][None, :]
    first = (d % (2 * n_freq)) < n_freq
    cos_t = jnp.cos(ang)
    sin_t = jnp.where(first[None, :], -jnp.sin(ang), jnp.sin(ang))
    partner = jnp.where(first, lane + n_freq, lane - n_freq)
    return cos_t, sin_t, partner


def _attn_kernel(lam_ref, q_ref, kt_ref, v_ref, g_ref, o_ref, *, out_scale):
    q = q_ref[0]
    kt = kt_ref[0]
    v = v_ref[0]
    lam = lam_ref[0]
    tq = q.shape[0]
    lane = lax.broadcasted_iota(jnp.int32, (1, GROUP), 1)
    map_of_lane = lane // HEAD_QK
    head_of_lane = lane // HEAD_V
    acc = jnp.zeros((tq, GROUP), F32)
    for h in range(N_HEADS):
        es, sums = [], []
        for m in range(2):
            keep = jnp.where(map_of_lane == 2 * h + m, 1.0, 0.0).astype(BF16)
            s = _dot(q * keep, kt)
            e = jnp.exp2(s - jnp.max(s, axis=-1, keepdims=True))
            es.append(e)
            sums.append(jnp.sum(e, axis=-1, keepdims=True))
        c = lam * sums[0] / sums[1]
        o = _dot((es[0] - es[1] * c).astype(BF16), v) * (1.0 / sums[0])
        acc = jnp.where(head_of_lane == h, o, acc)
    sq = acc * acc
    ms = jnp.zeros((tq, GROUP), F32)
    for h in range(N_HEADS):
        hsel = head_of_lane == h
        mh = jnp.sum(jnp.where(hsel, sq, 0.0), axis=-1, keepdims=True) * (1.0 / HEAD_V)
        ms = jnp.where(hsel, mh, ms)
    y = acc * lax.rsqrt(ms + SUBLN_EPS) * g_ref[...] * out_scale
    o_ref[0] = y.astype(o_ref.dtype)


def diff_attention(q, kt, v, lam, subln, out_scale):
    b, nq, _ = q.shape
    nk = v.shape[1]
    tq = min(nq, 256)
    g = jnp.tile(subln, N_HEADS)[None, :]
    return pl.pallas_call(
        functools.partial(_attn_kernel, out_scale=out_scale),
        grid=(b, nq // tq),
        in_specs=[pl.BlockSpec(memory_space=pltpu.SMEM),
                  pl.BlockSpec((1, tq, GROUP), lambda i, j: (i, j, 0)),
                  pl.BlockSpec((1, GROUP, nk), lambda i, j: (i, 0, 0)),
                  pl.BlockSpec((1, nk, GROUP), lambda i, j: (i, 0, 0)),
                  pl.BlockSpec((1, GROUP), lambda i, j: (0, 0))],
        out_specs=pl.BlockSpec((1, tq, GROUP), lambda i, j: (i, j, 0)),
        out_shape=jax.ShapeDtypeStruct((b, nq, GROUP), F32),
        compiler_params=_cparams("arbitrary", "arbitrary"),
        name="diff_attn",
    )(lam.reshape(1), q, kt, v, g)


def _fill_padded(pad_ref, n, write_rows):
    zeros = jnp.zeros((HALO, pad_ref.shape[1]), F32)
    pad_ref[0:HALO, :] = zeros
    pad_ref[HALO + n:HALO + n + HALO, :] = zeros
    write_rows()


def _shifted_rows(pad_ref, r0, ch, offsets):
    rows = ch + 2 * HALO
    blk = pad_ref[pl.ds(r0, rows), :]
    rolled = {0: blk}
    out = {}
    for k in offsets:
        o = HALO + k
        r = o % SUBLANES
        if r not in rolled:
            rolled[r] = pltpu.roll(blk, rows - r, 0)
        out[k] = rolled[r][o - r:o - r + ch]
    return out


def _pool_kernel(u_ref, w_ref, sc_ref, o_ref, pad_ref, *, n, ch):
    def copy(i, c):
        r0 = pl.multiple_of(i * ch, ch)
        pad_ref[pl.ds(HALO + r0, ch), :] = u_ref[0, pl.ds(r0, ch), :]
        return c
    _fill_padded(pad_ref, n, lambda: lax.fori_loop(0, n // ch, copy, 0))
    lane = lax.broadcasted_iota(jnp.int32, (1, GROUP), 1)
    pg = GROUP // len(POOL_WINDOWS)
    reach = POOL_WINDOWS[-1] // 2

    def body(i, c):
        r0 = pl.multiple_of(i * ch, ch)
        taps = _shifted_rows(pad_ref, r0, ch, range(-reach, reach))
        ld = lambda k: taps[k]
        t = r0 + lax.broadcasted_iota(jnp.int32, (ch, 1), 0)
        u = ld(0)
        sums = []
        s = u + ld(-1)
        sums.append(s)
        half = 1
        for _ in POOL_WINDOWS[1:]:
            for k in range(half, 2 * half):
                s = s + ld(k) + ld(-k - 1)
            half *= 2
            sums.append(s)
        pooled = None
        for gi, win in reversed(list(enumerate(POOL_WINDOWS))):
            hi = jnp.minimum(t + (win - win // 2), n)
            lo = jnp.maximum(t - win // 2, 0)
            mean = sums[gi] / (hi - lo).astype(F32)
            pooled = mean if pooled is None else jnp.where(lane < (gi + 1) * pg, mean, pooled)
        pooled = pooled - u
        y = _dot(pooled.astype(BF16), w_ref[...]) * sc_ref[...]
        o_ref[0, pl.ds(r0, ch), :] = y.astype(o_ref.dtype)
        return c
    lax.fori_loop(0, n // ch, body, 0)


def pool_mixer(u, w_pool, scale):
    b, n, c = u.shape
    ng, pg, _ = w_pool.shape
    wbd = jnp.zeros((c, c), F32)
    for gi in range(ng):
        wbd = wbd.at[gi * pg:(gi + 1) * pg, gi * pg:(gi + 1) * pg].set(w_pool[gi])
    ch = min(n, 256)
    return pl.pallas_call(
        functools.partial(_pool_kernel, n=n, ch=ch),
        grid=(b,),
        in_specs=[pl.BlockSpec((1, n, c), lambda i: (i, 0, 0)),
                  pl.BlockSpec((c, c), lambda i: (0, 0)),
                  pl.BlockSpec((1, c), lambda i: (0, 0))],
        out_specs=pl.BlockSpec((1, n, c), lambda i: (i, 0, 0)),
        out_shape=jax.ShapeDtypeStruct((b, n, c), F32),
        scratch_shapes=[pltpu.VMEM((n + 2 * HALO, c), F32)],
        compiler_params=_cparams("arbitrary"),
        name="pool_mixer",
    )(u, wbd.astype(BF16), scale[None, :])


def _conformer_kernel(p_ref, dw_ref, dwb_ref, lg_ref, lb_ref, pw_ref, pwb_ref, o_ref, pad_ref, *, n, ch):
    c = GROUP

    def glu(i, carry):
        r0 = pl.multiple_of(i * ch, ch)
        a = p_ref[0, pl.ds(r0, ch), 0:c]
        g = p_ref[0, pl.ds(r0, ch), c:2 * c]
        pad_ref[pl.ds(HALO + r0, ch), :] = a * _sigmoid(g)
        return carry
    _fill_padded(pad_ref, n, lambda: lax.fori_loop(0, n // ch, glu, 0))
    half = (CONV_KERNEL - 1) // 2

    def body(i, carry):
        r0 = pl.multiple_of(i * ch, ch)
        acc = jnp.zeros((ch, c), F32) + dwb_ref[...]
        taps = _shifted_rows(pad_ref, r0, ch, range(-half, half + 1))
        for k in range(CONV_KERNEL):
            acc = acc + taps[k - half] * dw_ref[k:k + 1, :]
        mu = jnp.mean(acc, axis=-1, keepdims=True)
        xc = acc - mu
        var = jnp.mean(xc * xc, axis=-1, keepdims=True)
        y = xc * lax.rsqrt(var + LN_EPS) * lg_ref[...] + lb_ref[...]
        y = y * _sigmoid(y)
        out = _dot(y.astype(BF16), pw_ref[...]) + pwb_ref[...]
        o_ref[0, pl.ds(r0, ch), :] = out.astype(o_ref.dtype)
        return carry
    lax.fori_loop(0, n // ch, body, 0)


def conformer_mixer(p, dw_w, dw_b, ln_g, ln_b, pw_w, pw_b):
    b, n, c2 = p.shape
    c = c2 // 2
    ch = min(n, 128)
    kp = 32
    dw = jnp.pad(dw_w, ((0, kp - CONV_KERNEL), (0, 0)))
    vec = lambda: pl.BlockSpec((1, c), lambda i: (0, 0))
    return pl.pallas_call(
        functools.partial(_conformer_kernel, n=n, ch=ch),
        grid=(b,),
        in_specs=[pl.BlockSpec((1, n, c2), lambda i: (i, 0, 0)),
                  pl.BlockSpec((kp, c), lambda i: (0, 0)),
                  vec(), vec(), vec(),
                  pl.BlockSpec((c, c), lambda i: (0, 0)),
                  vec()],
        out_specs=pl.BlockSpec((1, n, c), lambda i: (i, 0, 0)),
        out_shape=jax.ShapeDtypeStruct((b, n, c), F32),
        scratch_shapes=[pltpu.VMEM((n + 2 * HALO, c), F32)],
        compiler_params=_cparams("arbitrary"),
        name="conformer_mixer",
    )(p, dw, dw_b[None, :], ln_g[None, :], ln_b[None, :], pw_w.astype(BF16), pw_b[None, :])


def _short_conv_kernel(p_ref, w_ref, b_ref, o_ref, pad_ref, *, n, ch):
    def copy(i, carry):
        r0 = pl.multiple_of(i * ch, ch)
        pad_ref[pl.ds(HALO + r0, ch), :] = p_ref[0, pl.ds(r0, ch), :]
        return carry
    _fill_padded(pad_ref, n, lambda: lax.fori_loop(0, n // ch, copy, 0))
    half = (HYENA_SHORT - 1) // 2

    def body(i, carry):
        r0 = pl.multiple_of(i * ch, ch)
        acc = jnp.zeros((ch, pad_ref.shape[1]), F32) + b_ref[...]
        taps = _shifted_rows(pad_ref, r0, ch, range(-half, half + 1))
        for k in range(HYENA_SHORT):
            acc = acc + taps[k - half] * w_ref[k:k + 1, :]
        o_ref[0, pl.ds(r0, ch), :] = acc.astype(o_ref.dtype)
        return carry
    lax.fori_loop(0, n // ch, body, 0)


def hyena_short_conv(p, w, bias):
    b, n, c = p.shape
    ch = min(n, 128)
    wp = jnp.pad(w, ((0, SUBLANES - HYENA_SHORT), (0, 0)))
    return pl.pallas_call(
        functools.partial(_short_conv_kernel, n=n, ch=ch),
        grid=(b,),
        in_specs=[pl.BlockSpec((1, n, c), lambda i: (i, 0, 0)),
                  pl.BlockSpec((SUBLANES, c), lambda i: (0, 0)),
                  pl.BlockSpec((1, c), lambda i: (0, 0))],
        out_specs=pl.BlockSpec((1, n, c), lambda i: (i, 0, 0)),
        out_shape=jax.ShapeDtypeStruct((b, n, c), BF16),
        scratch_shapes=[pltpu.VMEM((n + 2 * HALO, c), F32)],
        compiler_params=_cparams("arbitrary"),
        name="hyena_short_conv",
    )(p, wp, bias[None, :])


def _toeplitz_conv(g_ref, u, nb, tb, bsz):
    ys = [jnp.zeros((bsz, tb), F32) for _ in range(nb)]
    for d in range(-(nb - 1), nb):
        start = tb * (nb - 1 + d)
        win = jnp.broadcast_to(g_ref[:, start:start + 2 * tb], (tb, 2 * tb))
        rolled = pltpu.roll(win, 0, 1, stride=1, stride_axis=0)
        blk = rolled[:, tb:2 * tb].astype(BF16)
        j_lo, j_hi = max(0, -d), min(nb, nb - d)
        o = _dot(u[j_lo * bsz:j_hi * bsz], blk)
        for j in range(j_lo, j_hi):
            ys[j + d] = ys[j + d] + o[(j - j_lo) * bsz:(j - j_lo + 1) * bsz]
    return ys


def _hyena_kernel(bias_ref, g_ref, v_ref, x1_ref, x2_ref, o_ref, *, nb, tb, bsz, cb):
    base = pl.program_id(0) * cb

    def body(ci, carry):
        v = v_ref[ci]
        y1 = jnp.concatenate(_toeplitz_conv(g_ref.at[0, ci], v, nb, tb, bsz), axis=0)
        z = x1_ref[ci].astype(F32) * (y1 + bias_ref[0, base + ci] * v.astype(F32))
        y2 = jnp.concatenate(_toeplitz_conv(g_ref.at[1, ci], z.astype(BF16), nb, tb, bsz), axis=0)
        out = x2_ref[ci].astype(F32) * (y2 + bias_ref[1, base + ci] * z)
        o_ref[ci] = out.astype(o_ref.dtype)
        return carry
    lax.fori_loop(0, cb, body, 0, unroll=2)


def hyena_long_conv(u, g_rows, bias):
    b, n, _ = u.shape
    tb = min(n, 256)
    nb = n // tb
    nbb = nb * b
    ut = jnp.transpose(u.reshape(b, nb, tb, 3, GROUP), (3, 4, 1, 0, 2)).reshape(3, GROUP, nbb, tb)
    cb = 8
    blk = lambda part: pl.BlockSpec((None, cb, nbb, tb), lambda i, part=part: (part, i, 0, 0))
    out = pl.pallas_call(
        functools.partial(_hyena_kernel, nb=nb, tb=tb, bsz=b, cb=cb),
        grid=(GROUP // cb,),
        in_specs=[pl.BlockSpec(memory_space=pltpu.SMEM),
                  pl.BlockSpec((2, cb, 1, 2 * n), lambda i: (0, i, 0, 0)),
                  blk(0), blk(1), blk(2)],
        out_specs=pl.BlockSpec((cb, nbb, tb), lambda i: (i, 0, 0)),
        out_shape=jax.ShapeDtypeStruct((GROUP, nbb, tb), F32),
        compiler_params=_cparams("arbitrary"),
        name="hyena_long_conv",
    )(bias, g_rows, ut, ut, ut)
    return jnp.transpose(out.reshape(GROUP, nb, b, tb), (2, 1, 3, 0)).reshape(b, n, GROUP)


def _out_proj_kernel(x_ref, gt_ref, a_ref, h_ref, p_ref, c_ref, w_ref, o_ref):
    acc = _dot(a_ref[0].astype(BF16), w_ref[0:GROUP, :])
    acc = acc + _dot(h_ref[0].astype(BF16), w_ref[GROUP:2 * GROUP, :])
    acc = acc + _dot(p_ref[0].astype(BF16), w_ref[2 * GROUP:3 * GROUP, :])
    acc = acc + _dot(c_ref[0].astype(BF16), w_ref[3 * GROUP:4 * GROUP, :])
    o_ref[0] = x_ref[0] + gt_ref[0] * acc


def out_projection(x, gate, parts, w_out):
    b, l, d = x.shape
    tm = min(l, 512)
    part = lambda: pl.BlockSpec((1, tm, GROUP), lambda i, j: (i, j, 0))
    return pl.pallas_call(
        _out_proj_kernel,
        grid=(b, l // tm),
        in_specs=[pl.BlockSpec((1, tm, d), lambda i, j: (i, j, 0)),
                  pl.BlockSpec((1, 1, d), lambda i, j: (i, 0, 0)),
                  part(), part(), part(), part(),
                  pl.BlockSpec(w_out.shape, lambda i, j: (0, 0))],
        out_specs=pl.BlockSpec((1, tm, d), lambda i, j: (i, j, 0)),
        out_shape=jax.ShapeDtypeStruct((b, l, d), F32),
        compiler_params=_cparams("arbitrary", "arbitrary"),
        name="out_proj",
    )(x, gate, *parts, w_out.astype(BF16))


SLAB = 8


def _store_slab(ref, val):
    rows = val.shape[0]
    for s in range(SLAB):
        ref[pl.ds(s, rows, stride=SLAB), :] = val[:, s * LANES:(s + 1) * LANES]


def _load_slab(ref, start, rows):
    return jnp.concatenate([ref[pl.ds(start * SLAB + s, rows, stride=SLAB), :] for s in range(SLAB)], axis=1)


def _router_kernel(x_ref, sh_ref, sc_ref, g_ref, wh_ref, wl_ref, rb_ref, h_ref, idx_ref, gate_ref):
    x = x_ref[0]
    ms = jnp.mean(x * x, axis=-1, keepdims=True)
    h = (x * lax.rsqrt(ms + NORM_EPS)) * g_ref[...] * (1.0 + sc_ref[0]) + sh_ref[0]
    _store_slab(h_ref, h)
    hh, hl = _split_bf16(h)
    logits = _dot(hh, wh_ref[...]) + (_dot(hh, wl_ref[...]) + _dot(hl, wh_ref[...])) + rb_ref[...]
    tm = x.shape[0]
    lane = lax.broadcasted_iota(jnp.int32, (tm, N_EXPERTS), 1).astype(F32)
    out_lane = lax.broadcasted_iota(jnp.int32, (tm, LANES), 1)
    vals, idxs = [], []
    cur = logits
    for _ in range(TOP_K):
        mx = jnp.max(cur, axis=-1, keepdims=True)
        ix = jnp.min(jnp.where(cur == mx, lane, float(N_EXPERTS)), axis=-1, keepdims=True)
        vals.append(mx)
        idxs.append(ix)
        cur = jnp.where(lane == ix, -jnp.inf, cur)
    es = [jnp.exp(v - vals[0]) for v in vals]
    inv = 1.0 / (es[0] + es[1] + es[2] + es[3])
    idx_out = jnp.zeros((tm, LANES), F32)
    gate_out = jnp.zeros((tm, LANES), F32)
    for k in range(TOP_K):
        idx_out = jnp.where(out_lane == k, idxs[k], idx_out)
        gate_out = jnp.where(out_lane == k, es[k] * inv, gate_out)
    idx_ref[0] = idx_out.astype(jnp.int32)
    gate_ref[0] = gate_out


def router(x, shift, scale, g, router_w, router_b):
    b, l, d = x.shape
    assert d == SLAB * LANES
    tm = min(l, 512)
    per_seq = l // tm
    wh, wl = _split_bf16(router_w)
    vec3 = lambda: pl.BlockSpec((1, 1, d), lambda i, j: (i, 0, 0))
    tile = lambda w: pl.BlockSpec((1, tm, w), lambda i, j: (i, j, 0))
    h, idx, gates = pl.pallas_call(
        _router_kernel,
        grid=(b, per_seq),
        in_specs=[tile(d), vec3(), vec3(),
                  pl.BlockSpec((1, d), lambda i, j: (0, 0)),
                  pl.BlockSpec(wh.shape, lambda i, j: (0, 0)),
                  pl.BlockSpec(wl.shape, lambda i, j: (0, 0)),
                  pl.BlockSpec((1, N_EXPERTS), lambda i, j: (0, 0))],
        out_specs=[pl.BlockSpec((tm * SLAB, LANES), lambda i, j: (i * per_seq + j, 0)), tile(LANES), tile(LANES)],
        out_shape=[jax.ShapeDtypeStruct((b * l * SLAB, LANES), F32),
                   jax.ShapeDtypeStruct((b, l, LANES), jnp.int32),
                   jax.ShapeDtypeStruct((b, l, LANES), F32)],
        compiler_params=_cparams("arbitrary", "arbitrary"),
        name="moe_router",
    )(x, shift, scale, g, wh, wl, router_b[None, :])
    return h, idx[..., :TOP_K], gates[..., :TOP_K]


GATHER_RING = 32
DISPATCH_ROWS = 2048


def _dispatch_kernel(lo_ref, hi_ref, pos_ref, src_ref, zero_ref, dst_ref, sem, *, rows):
    i = pl.program_id(0)
    tok0 = i * (rows // TOP_K)

    def copy(p):
        tok = tok0 + lax.shift_right_logical(p, TOP_K.bit_length() - 1)
        src = src_ref.at[pl.ds(pl.multiple_of(tok * SLAB, SLAB), SLAB)]
        dst = dst_ref.at[pl.ds(pl.multiple_of(pos_ref[p], SLAB), SLAB)]
        return pltpu.make_async_copy(src, dst, sem)

    def body(p, carry):
        copy(p).start()

        @pl.when(jnp.logical_or(i > 0, p >= GATHER_RING))
        def _():
            copy(p).wait()
        return carry
    lax.fori_loop(0, rows, body, 0, unroll=8)

    @pl.when(i == pl.num_programs(0) - 1)
    def _():
        def drain(p, carry):
            copy(p).wait()
            return carry
        lax.fori_loop(0, GATHER_RING, drain, 0)

        def fill(r):
            return pltpu.make_async_copy(zero_ref, dst_ref.at[pl.ds(pl.multiple_of(r * SLAB, SLAB), SLAB)], sem)

        def per_expert(e, carry):
            def start(r, c):
                fill(r).start()
                return c

            def wait(r, c):
                fill(r).wait()
                return c
            lax.fori_loop(lo_ref[e], hi_ref[e], start, 0)
            lax.fori_loop(lo_ref[e], hi_ref[e], wait, 0)
            return carry
        lax.fori_loop(0, N_EXPERTS, per_expert, 0)


def dispatch_rows(h, pos_rows, pad_lo, pad_hi, n_slots):
    rows = DISPATCH_ROWS
    grid_spec = pltpu.PrefetchScalarGridSpec(
        num_scalar_prefetch=2,
        grid=(pos_rows.shape[0] // rows,),
        in_specs=[pl.BlockSpec((rows,), lambda i, lo, hi: (i,), memory_space=pltpu.SMEM),
                  pl.BlockSpec(memory_space=pl.ANY),
                  pl.BlockSpec(memory_space=pl.ANY)],
        out_specs=pl.BlockSpec(memory_space=pl.ANY),
        scratch_shapes=[pltpu.SemaphoreType.DMA(())])
    return pl.pallas_call(
        functools.partial(_dispatch_kernel, rows=rows),
        grid_spec=grid_spec,
        out_shape=jax.ShapeDtypeStruct((n_slots * SLAB, LANES), h.dtype),
        compiler_params=_cparams("arbitrary"),
        name="moe_dispatch",
    )(pad_lo, pad_hi, pos_rows, h, jnp.zeros((SLAB, LANES), h.dtype))


def _expert_kernel(te_ref, tv_ref, x_ref, w1_ref, b1_ref, w2_ref, b2_ref, o_ref, *, tm):
    i = pl.program_id(0)
    ff = w2_ref.shape[1]

    @pl.when(tv_ref[i] > 0)
    def _():
        x = _load_slab(x_ref, 0, tm).astype(BF16)
        hid = _dot(x, w1_ref[0]) + b1_ref[0]
        gate = jnp.minimum(hid[:, :ff], SWIGLU_LIMIT)
        up = jnp.clip(hid[:, ff:], -SWIGLU_LIMIT, SWIGLU_LIMIT)
        act = gate * _sigmoid(SWIGLU_ALPHA * gate) * (up + 1.0)
        _store_slab(o_ref, _dot(act.astype(BF16), w2_ref[0]) + b2_ref[0])

    @pl.when(tv_ref[i] == 0)
    def _():
        o_ref[...] = jnp.zeros_like(o_ref)


def expert_ffn(xs, tile_expert, tile_valid, w1, b1, w2, b2, tm):
    n_slots = xs.shape[0] // SLAB
    ne, d, ff2 = w1.shape
    ff = w2.shape[1]
    slab = lambda: pl.BlockSpec((tm * SLAB, LANES), lambda i, te, tv: (i, 0))
    grid_spec = pltpu.PrefetchScalarGridSpec(
        num_scalar_prefetch=2,
        grid=(n_slots // tm,),
        in_specs=[pl.BlockSpec((tm * SLAB, LANES), lambda i, te, tv: (i * tv[i], 0)),
                  pl.BlockSpec((1, d, ff2), lambda i, te, tv: (te[i], 0, 0)),
                  pl.BlockSpec((1, 1, ff2), lambda i, te, tv: (te[i], 0, 0)),
                  pl.BlockSpec((1, ff, d), lambda i, te, tv: (te[i], 0, 0)),
                  pl.BlockSpec((1, 1, d), lambda i, te, tv: (te[i], 0, 0))],
        out_specs=slab())
    return pl.pallas_call(
        functools.partial(_expert_kernel, tm=tm),
        grid_spec=grid_spec,
        out_shape=jax.ShapeDtypeStruct((n_slots * SLAB, LANES), F32),
        compiler_params=_cparams("arbitrary"),
        name="moe_experts",
    )(tile_expert, tile_valid, xs, w1, b1.reshape(ne, 1, ff2), w2, b2.reshape(ne, 1, d))


def _combine_kernel(pos_ref, nxt_ref, x_ref, gt_ref, gates_ref, fin_ref, ys_ref, o_ref, buf, sems, *, tc, final):
    i = pl.program_id(0)
    n = tc * TOP_K
    slot = i % 2

    def copy(idx_ref, r, s):
        src = ys_ref.at[pl.ds(pl.multiple_of(idx_ref[r], SLAB), SLAB)]
        return pltpu.make_async_copy(src, buf.at[s, pl.ds(pl.multiple_of(r * SLAB, SLAB), SLAB)], sems.at[s])

    def issue(idx_ref, s):
        def body(r, carry):
            copy(idx_ref, r, s).start()
            return carry
        lax.fori_loop(0, n, body, 0, unroll=8)

    @pl.when(i == 0)
    def _():
        issue(pos_ref, 0)

    @pl.when(i + 1 < pl.num_programs(0))
    def _():
        issue(nxt_ref, 1 - slot)

    def drain(r, carry):
        copy(pos_ref, r, slot).wait()
        return carry
    lax.fori_loop(0, n, drain, 0, unroll=8)
    gates = gates_ref[...]
    y = jnp.zeros((tc, SLAB * LANES), F32)
    for k in range(TOP_K):
        y = y + gates[:, k:k + 1] * _load_slab(buf.at[slot], k * tc, tc)
    out = x_ref[...] + gt_ref[0] * y
    if final:
        ms = jnp.mean(out * out, axis=-1, keepdims=True)
        out = out * lax.rsqrt(ms + NORM_EPS) * fin_ref[...]
    o_ref[...] = out


def combine_rows(x, gate_vec, gates, pos, ys, final_g, final):
    b, l, d = x.shape
    t = b * l
    tc = min(l, 256)
    steps = t // tc
    per_seq = l // tc
    n = tc * TOP_K
    pos_t = jnp.transpose(pos.reshape(steps, tc, TOP_K), (0, 2, 1)).reshape(steps * n) * SLAB
    gates_p = jnp.pad(gates.reshape(t, TOP_K), ((0, 0), (0, LANES - TOP_K)))
    out = pl.pallas_call(
        functools.partial(_combine_kernel, tc=tc, final=final),
        grid=(steps,),
        in_specs=[pl.BlockSpec((n,), lambda i: (i,), memory_space=pltpu.SMEM),
                  pl.BlockSpec((n,), lambda i: (jnp.minimum(i + 1, steps - 1),), memory_space=pltpu.SMEM),
                  pl.BlockSpec((tc, d), lambda i: (i, 0)),
                  pl.BlockSpec((1, 1, d), lambda i: (i // per_seq, 0, 0)),
                  pl.BlockSpec((tc, LANES), lambda i: (i, 0)),
                  pl.BlockSpec((1, d), lambda i: (0, 0)),
                  pl.BlockSpec(memory_space=pl.ANY)],
        out_specs=pl.BlockSpec((tc, d), lambda i: (i, 0)),
        out_shape=jax.ShapeDtypeStruct((t, d), F32),
        scratch_shapes=[pltpu.VMEM((2, n * SLAB, LANES), F32), pltpu.SemaphoreType.DMA((2,))],
        compiler_params=_cparams("arbitrary"),
        name="moe_combine",
    )(pos_t, pos_t, x.reshape(t, d), gate_vec, gates_p, final_g[None, :], ys)
    return out.reshape(b, l, d)


def routing_plan(idx, tm):
    t = idx.shape[0]
    n = t * TOP_K
    e = idx.reshape(n)
    onehot = (e[:, None] == jnp.arange(N_EXPERTS, dtype=e.dtype)[None, :]).astype(jnp.int32)
    csum = jnp.cumsum(onehot, axis=0)
    counts = csum[-1]
    rank = jnp.take_along_axis(csum, e[:, None], axis=1)[:, 0] - 1
    padded = ((counts + tm - 1) // tm) * tm
    ends = jnp.cumsum(padded)
    starts = ends - padded
    pos = starts[e] + rank
    n_slots = -(-(n + N_EXPERTS * (tm - 1)) // tm) * tm
    tile_start = jnp.arange(n_slots // tm, dtype=jnp.int32) * tm
    tile_expert = jnp.minimum(jnp.sum((tile_start[:, None] >= ends[None, :]).astype(jnp.int32), axis=1), N_EXPERTS - 1)
    tile_valid = (tile_start < ends[-1]).astype(jnp.int32)
    return (pos.reshape(t, TOP_K).astype(jnp.int32), (starts + counts).astype(jnp.int32), ends.astype(jnp.int32),
            n_slots, tile_expert.astype(jnp.int32), tile_valid)


MOE_TILE = 512


def moe_block(x, shift, scale, gate_vec, g, router_w, router_b, w1, b1, w2, b2, final_g, final):
    b, l, d = x.shape
    h, idx, gates = router(x, shift, scale, g, router_w, router_b)
    pos, pad_lo, pad_hi, n_slots, tile_expert, tile_valid = routing_plan(idx.reshape(b * l, TOP_K), MOE_TILE)
    xs = dispatch_rows(h, pos.reshape(-1) * SLAB, pad_lo, pad_hi, n_slots)
    ys = expert_ffn(xs, tile_expert, tile_valid, w1, b1, w2, b2, MOE_TILE)
    return combine_rows(x, gate_vec, gates, pos, ys, final_g, final)


def _diff_lambda(lp, lam_init):
    lp = lp.astype(F32)
    return jnp.exp(jnp.sum(lp[0] * lp[1])) - jnp.exp(jnp.sum(lp[2] * lp[3])) + lam_init


def kernel(x, c, ctx, c_ctx, norm1, norm2, w_mod, b_mod, w_in, w_out, attn_lambda, attn_subln,
           hyena_short_w, hyena_short_b, hyena_w1, hyena_b1, hyena_w2, hyena_b2, hyena_w3, hyena_freq,
           hyena_bias, pool_w, pool_scale, conv_dw_w, conv_dw_b, conv_ln_g, conv_ln_b, conv_pw_w,
           conv_pw_b, router_w, router_b, moe_w1, moe_b1, moe_w2, moe_b2, final_norm):
    bsz, n_lat, d = x.shape
    n_ctx = ctx.shape[1]
    depth = w_mod.shape[0]
    g = GROUP
    off_hy, off_pool, off_conv = 3 * g, 6 * g, 7 * g

    rows = -(-(bsz + 1) // SUBLANES) * SUBLANES
    cvec = jnp.zeros((rows, d), F32).at[:bsz].set(c).at[bsz].set(c_ctx)
    mods = adaln_all(cvec, w_mod, b_mod)

    cos_t, sin_t, partner = rope_tables(n_lat)
    ones_t = jnp.ones((n_ctx, g), F32)
    zeros_t = jnp.zeros((n_ctx, g), F32)
    qk_scale = HEAD_QK ** -0.5 * math.log2(math.e)

    xc = ctx
    for i in range(depth):
        last = i == depth - 1
        m = mods[i]
        lat = [m[:bsz, None, k * d:(k + 1) * d] for k in range(6)]
        cm = [jnp.broadcast_to(m[bsz, k * d:(k + 1) * d][None, None, :], (bsz, 1, d)) for k in range(6)]
        lam_init = 0.8 - 0.6 * math.exp(-0.3 * i)
        lam = _diff_lambda(attn_lambda[i], lam_init)
        wi = w_in[i]
        g1 = norm1[i][None, :]
        g2 = norm2[i][None, :]
        w1b = moe_w1[i].astype(BF16)
        w2b = moe_w2[i].astype(BF16)

        w_lat = jnp.concatenate([wi, wi[:, partner], wi[:, g + partner]], axis=1).astype(BF16)
        nin = wi.shape[1]
        segs = [(0, g, nin, qk_scale), (g, g, nin + g, 1.0), (2 * g, g, None, 1.0),
                (off_hy, 3 * g, None, 1.0), (off_pool, g, None, 1.0), (off_conv, 2 * g, None, 1.0)]
        q, k, v, p_hy, p_pool, p_conv = modulated_projection(
            x, lat[0], lat[1], g1, w_lat, segs, [BF16, BF16, BF16, F32, F32, F32], cos_t, sin_t)

        if last:
            w_ctx = wi[:, g:3 * g].astype(BF16)
            kc, vc = modulated_projection(xc, cm[0], cm[1], g1, w_ctx,
                                          [(0, g, None, 1.0), (g, g, None, 1.0)], [BF16, BF16], ones_t, zeros_t)
        else:
            csegs = [(0, g, None, qk_scale), (g, g, None, 1.0), (2 * g, g, None, 1.0),
                     (off_hy, 3 * g, None, 1.0), (off_pool, g, None, 1.0), (off_conv, 2 * g, None, 1.0)]
            qc, kc, vc, pc_hy, pc_pool, pc_conv = modulated_projection(
                xc, cm[0], cm[1], g1, wi.astype(BF16), csegs, [BF16, BF16, BF16, F32, F32, F32], ones_t, zeros_t)

        kt_all = jnp.transpose(jnp.concatenate([kc, k], axis=1), (0, 2, 1))
        v_all = jnp.concatenate([vc, v], axis=1)
        o_attn = diff_attention(q, kt_all, v_all, lam, attn_subln[i], 1.0 - lam_init)

        filt_params = (hyena_w1[i], hyena_b1[i], hyena_w2[i], hyena_b2[i], hyena_w3[i], hyena_freq[i])

        def local_groups(p_hy_, p_pool_, p_conv_, n_tokens):
            g_rows = filter_rows(hyena_filters(n_tokens, *filt_params), n_tokens)
            u = hyena_short_conv(p_hy_, hyena_short_w[i], hyena_short_b[i])
            o_hy = hyena_long_conv(u, g_rows, hyena_bias[i])
            o_pool = pool_mixer(p_pool_, pool_w[i], pool_scale[i])
            o_conv = conformer_mixer(p_conv_, conv_dw_w[i], conv_dw_b[i], conv_ln_g[i], conv_ln_b[i],
                                     conv_pw_w[i], conv_pw_b[i])
            return o_hy, o_pool, o_conv

        o_hy, o_pool, o_conv = local_groups(p_hy, p_pool, p_conv, n_lat)
        x = out_projection(x, lat[2], (o_attn, o_hy, o_pool, o_conv), w_out[i])
        if not last:
            oc_attn = diff_attention(qc, jnp.transpose(kc, (0, 2, 1)), vc, lam, attn_subln[i], 1.0 - lam_init)
            oc_hy, oc_pool, oc_conv = local_groups(pc_hy, pc_pool, pc_conv, n_ctx)
            xc = out_projection(xc, cm[2], (oc_attn, oc_hy, oc_pool, oc_conv), w_out[i])

        x = moe_block(x, lat[3], lat[4], lat[5], g2, router_w[i], router_b[i], w1b, moe_b1[i], w2b, moe_b2[i],
                      final_norm, last)
        if not last:
            xc = moe_block(xc, cm[3], cm[4], cm[5], g2, router_w[i], router_b[i], w1b, moe_b1[i], w2b, moe_b2[i],
                           final_norm, False)
    return x
```

```python
import functools
import math

import jax
import jax.numpy as jnp
from jax import lax
from jax.experimental import pallas as pl
from jax.experimental.pallas import tpu as pltpu

F32 = jnp.float32
BF16 = jnp.bfloat16

GRID_W = 64
N_HEADS = 4
HEAD_V = 64
HEAD_QK = 32
GROUP = 256
ROPE_BASE = 10000.0
SUBLN_EPS = 1e-5
NORM_EPS = 1e-6
LN_EPS = 1e-5
POOL_WINDOWS = (2, 4, 8, 16)
CONV_KERNEL = 31
HYENA_SHORT = 3
HYENA_EMB = 33
HYENA_TARGET = 1e-2
HYENA_FAST_PCT = 0.3
HYENA_SLOW_PCT = 1.5
N_EXPERTS = 32
TOP_K = 4
SWIGLU_ALPHA = 1.702
SWIGLU_LIMIT = 7.0

LANES = 128
SUBLANES = 8
VMEM_LIMIT = 56 * 1024 * 1024
HALO = 16


def _cparams(*sem):
    return pltpu.CompilerParams(dimension_semantics=sem, vmem_limit_bytes=VMEM_LIMIT)


def _split_bf16(a):
    hi = a.astype(BF16)
    lo = (a - hi.astype(F32)).astype(BF16)
    return hi, lo


def _dot(a, b):
    return jnp.dot(a, b, preferred_element_type=F32)


def _dot3(a, b):
    ah, al = _split_bf16(a)
    bh, bl = _split_bf16(b)
    return _dot(ah, bh) + (_dot(ah, bl) + _dot(al, bh))


def _sigmoid(x):
    return 1.0 / (1.0 + jnp.exp(-x))


def _adaln_kernel(c_ref, w_ref, b_ref, o_ref):
    cv = c_ref[...]
    s = cv * _sigmoid(cv)
    o_ref[0] = _dot3(s, w_ref[0]) + b_ref[0]


def adaln_all(cvec, w_mod, b_mod):
    depth, d, n = w_mod.shape
    r = cvec.shape[0]
    tn = 1536
    return pl.pallas_call(
        _adaln_kernel,
        grid=(depth, n // tn),
        in_specs=[pl.BlockSpec((r, d), lambda i, j: (0, 0)),
                  pl.BlockSpec((1, d, tn), lambda i, j: (i, 0, j)),
                  pl.BlockSpec((1, 1, tn), lambda i, j: (i, 0, j))],
        out_specs=pl.BlockSpec((1, r, tn), lambda i, j: (i, 0, j)),
        out_shape=jax.ShapeDtypeStruct((depth, r, n), F32),
        compiler_params=_cparams("arbitrary", "arbitrary"),
        name="adaln",
    )(cvec, w_mod, b_mod.reshape(depth, 1, n))


def _filter_kernel(feat_ref, w1_ref, b1_ref, w2_ref, b2_ref, w3_ref, fr_ref, dec_ref, o_ref):
    fr = fr_ref[...]
    z = jnp.sin(fr * (_dot3(feat_ref[...], w1_ref[...]) + b1_ref[...]))
    z = jnp.sin(fr * (_dot3(z, w2_ref[...]) + b2_ref[...]))
    o_ref[...] = _dot3(z, w3_ref[...]) * dec_ref[...]


def hyena_filters(n_tokens, w1, b1, w2, b2, w3, freq):
    bands = (HYENA_EMB - 1) // 2
    t = jnp.linspace(0.0, 1.0, n_tokens, dtype=F32)[:, None]
    w = 2.0 * math.pi * jnp.arange(n_tokens, dtype=F32)[:, None] / n_tokens
    f = jnp.linspace(1e-4, bands - 1, bands, dtype=F32)[None, :]
    feat = jnp.concatenate([t, jnp.cos(f * w), -jnp.sin(f * w)], axis=-1)
    kpad = 40
    feat = jnp.pad(feat, ((0, 0), (0, kpad - HYENA_EMB)))
    w1p = jnp.pad(w1, ((0, kpad - HYENA_EMB), (0, 0)))
    min_decay = math.log(HYENA_TARGET) / HYENA_SLOW_PCT
    max_decay = math.log(HYENA_TARGET) / HYENA_FAST_PCT
    deltas = jnp.abs(jnp.linspace(min_decay, max_decay, GROUP, dtype=F32))
    decay = jnp.exp(-t * deltas)
    nf = w3.shape[1]
    decay = jnp.tile(decay, (1, nf // GROUP))
    ffn = w2.shape[0]
    tl = min(n_tokens, 512)
    full = lambda shape: pl.BlockSpec(shape, lambda i: (0, 0))
    return pl.pallas_call(
        _filter_kernel,
        grid=(n_tokens // tl,),
        in_specs=[pl.BlockSpec((tl, kpad), lambda i: (i, 0)),
                  full((kpad, ffn)), full((1, ffn)), full((ffn, ffn)), full((1, ffn)),
                  full((ffn, nf)), full((1, ffn)),
                  pl.BlockSpec((tl, nf), lambda i: (i, 0))],
        out_specs=pl.BlockSpec((tl, nf), lambda i: (i, 0)),
        out_shape=jax.ShapeDtypeStruct((n_tokens, nf), F32),
        compiler_params=_cparams("arbitrary"),
        name="hyena_filter",
    )(feat, w1p, b1[None, :], w2, b2[None, :], w3, freq[None, :], decay)


def filter_rows(filt, n_tokens):
    order = filt.shape[1] // (2 * GROUP)
    f4 = filt.reshape(n_tokens, order, 2, GROUP)
    hf = f4[:, :, 0]
    hb = f4[:, :, 1]
    rows = jnp.concatenate([jnp.zeros_like(hf[:1]), hb[:0:-1], hf], axis=0)
    return jnp.transpose(rows, (1, 2, 0))[:, :, None, :]


def _proj_kernel(x_ref, sh_ref, sc_ref, g_ref, w_ref, cos_ref, sin_ref, *out_refs, segs):
    x = x_ref[0]
    ms = jnp.mean(x * x, axis=-1, keepdims=True)
    h = (x * lax.rsqrt(ms + NORM_EPS)) * g_ref[...] * (1.0 + sc_ref[0]) + sh_ref[0]
    hb = h.astype(BF16)
    for o_ref, (start, width, swap_start, scale) in zip(out_refs, segs):
        p = _dot(hb, w_ref[:, start:start + width])
        if swap_start is not None:
            ps = _dot(hb, w_ref[:, swap_start:swap_start + width])
            p = p * cos_ref[...] + ps * sin_ref[...]
        if scale != 1.0:
            p = p * scale
        o_ref[0] = p.astype(o_ref.dtype)


def modulated_projection(x, shift, scale, g, w, segs, out_dtypes, cos_t, sin_t):
    b, l, d = x.shape
    tm = min(l, 512)
    nw = w.shape[1]
    rw = cos_t.shape[1]
    outs = [jax.ShapeDtypeStruct((b, l, s[1]), dt) for s, dt in zip(segs, out_dtypes)]
    return pl.pallas_call(
        functools.partial(_proj_kernel, segs=tuple(segs)),
        grid=(b, l // tm),
        in_specs=[pl.BlockSpec((1, tm, d), lambda i, j: (i, j, 0)),
                  pl.BlockSpec((1, 1, d), lambda i, j: (i, 0, 0)),
                  pl.BlockSpec((1, 1, d), lambda i, j: (i, 0, 0)),
                  pl.BlockSpec((1, d), lambda i, j: (0, 0)),
                  pl.BlockSpec((d, nw), lambda i, j: (0, 0)),
                  pl.BlockSpec((tm, rw), lambda i, j: (j, 0)),
                  pl.BlockSpec((tm, rw), lambda i, j: (j, 0))],
        out_specs=[pl.BlockSpec((1, tm, s[1]), lambda i, j: (i, j, 0)) for s in segs],
        out_shape=outs,
        compiler_params=_cparams("arbitrary", "arbitrary"),
        name="mod_proj",
    )(x, shift, scale, g, w, cos_t, sin_t)


def rope_tables(n_tokens):
    rows = n_tokens // GRID_W
    row = jnp.repeat(jnp.arange(rows, dtype=F32), GRID_W)
    col = jnp.tile(jnp.arange(GRID_W, dtype=F32), rows)
    n_freq = HEAD_QK // 4
    inv = ROPE_BASE ** (-jnp.arange(n_freq, dtype=F32) / n_freq)
    lane = jnp.arange(GROUP)
    d = lane % HEAD_QK
    pos = jnp.where((d < 2 * n_freq)[None, :], row[:, None], col[:, None])
    ang = pos * inv[d % n_freq][None, :]
    first = (d % (2 * n_freq)) < n_freq
    cos_t = jnp.cos(ang)
    sin_t = jnp.where(first[None, :], -jnp.sin(ang), jnp.sin(ang))
    partner = jnp.where(first, lane + n_freq, lane - n_freq)
    return cos_t, sin_t, partner


def _attn_kernel(lam_ref, q_ref, kt_ref, v_ref, g_ref, o_ref, *, out_scale):
    q = q_ref[0]
    kt = kt_ref[0]
    v = v_ref[0]
    lam = lam_ref[0]
    tq = q.shape[0]
    lane = lax.broadcasted_iota(jnp.int32, (1, GROUP), 1)
    map_of_lane = lane // HEAD_QK
    head_of_lane = lane // HEAD_V
    acc = jnp.zeros((tq, GROUP), F32)
    for h in range(N_HEADS):
        es, sums = [], []
        for m in range(2):
            keep = jnp.where(map_of_lane == 2 * h + m, 1.0, 0.0).astype(BF16)
            s = _dot(q * keep, kt)
            e = jnp.exp2(s - jnp.max(s, axis=-1, keepdims=True))
            es.append(e)
            sums.append(jnp.sum(e, axis=-1, keepdims=True))
        c = lam * sums[0] / sums[1]
        o = _dot((es[0] - es[1] * c).astype(BF16), v) * (1.0 / sums[0])
        acc = jnp.where(head_of_lane == h, o, acc)
    sq = acc * acc
    ms = jnp.zeros((tq, GROUP), F32)
    for h in range(N_HEADS):
        hsel = head_of_lane == h
        mh = jnp.sum(jnp.where(hsel, sq, 0.0), axis=-1, keepdims=True) * (1.0 / HEAD_V)
        ms = jnp.where(hsel, mh, ms)
    y = acc * lax.rsqrt(ms + SUBLN_EPS) * g_ref[...] * out_scale
    o_ref[0] = y.astype(o_ref.dtype)


def diff_attention(q, kt, v, lam, subln, out_scale):
    b, nq, _ = q.shape
    nk = v.shape[1]
    tq = min(nq, 256)
    g = jnp.tile(subln, N_HEADS)[None, :]
    return pl.pallas_call(
        functools.partial(_attn_kernel, out_scale=out_scale),
        grid=(b, nq // tq),
        in_specs=[pl.BlockSpec(memory_space=pltpu.SMEM),
                  pl.BlockSpec((1, tq, GROUP), lambda i, j: (i, j, 0)),
                  pl.BlockSpec((1, GROUP, nk), lambda i, j: (i, 0, 0)),
                  pl.BlockSpec((1, nk, GROUP), lambda i, j: (i, 0, 0)),
                  pl.BlockSpec((1, GROUP), lambda i, j: (0, 0))],
        out_specs=pl.BlockSpec((1, tq, GROUP), lambda i, j: (i, j, 0)),
        out_shape=jax.ShapeDtypeStruct((b, nq, GROUP), F32),
        compiler_params=_cparams("arbitrary", "arbitrary"),
        name="diff_attn",
    )(lam.reshape(1), q, kt, v, g)


def _fill_padded(pad_ref, n, write_rows):
    zeros = jnp.zeros((HALO, pad_ref.shape[1]), F32)
    pad_ref[0:HALO, :] = zeros
    pad_ref[HALO + n:HALO + n + HALO, :] = zeros
    write_rows()


def _shifted_rows(pad_ref, r0, ch, offsets):
    rows = ch + 2 * HALO
    blk = pad_ref[pl.ds(r0, rows), :]
    rolled = {0: blk}
    out = {}
    for k in offsets:
        o = HALO + k
        r = o % SUBLANES
        if r not in rolled:
            rolled[r] = pltpu.roll(blk, rows - r, 0)
        out[k] = rolled[r][o - r:o - r + ch]
    return out


def _pool_kernel(u_ref, w_ref, sc_ref, o_ref, pad_ref, *, n, ch):
    def copy(i, c):
        r0 = pl.multiple_of(i * ch, ch)
        pad_ref[pl.ds(HALO + r0, ch), :] = u_ref[0, pl.ds(r0, ch), :]
        return c
    _fill_padded(pad_ref, n, lambda: lax.fori_loop(0, n // ch, copy, 0))
    lane = lax.broadcasted_iota(jnp.int32, (1, GROUP), 1)
    pg = GROUP // len(POOL_WINDOWS)
    reach = POOL_WINDOWS[-1] // 2

    def body(i, c):
        r0 = pl.multiple_of(i * ch, ch)
        taps = _shifted_rows(pad_ref, r0, ch, range(-reach, reach))
        ld = lambda k: taps[k]
        t = r0 + lax.broadcasted_iota(jnp.int32, (ch, 1), 0)
        u = ld(0)
        sums = []
        s = u + ld(-1)
        sums.append(s)
        half = 1
        for _ in POOL_WINDOWS[1:]:
            for k in range(half, 2 * half):
                s = s + ld(k) + ld(-k - 1)
            half *= 2
            sums.append(s)
        pooled = None
        for gi, win in reversed(list(enumerate(POOL_WINDOWS))):
            hi = jnp.minimum(t + (win - win // 2), n)
            lo = jnp.maximum(t - win // 2, 0)
            mean = sums[gi] / (hi - lo).astype(F32)
            pooled = mean if pooled is None else jnp.where(lane < (gi + 1) * pg, mean, pooled)
        pooled = pooled - u
        y = _dot(pooled.astype(BF16), w_ref[...]) * sc_ref[...]
        o_ref[0, pl.ds(r0, ch), :] = y.astype(o_ref.dtype)
        return c
    lax.fori_loop(0, n // ch, body, 0)


def pool_mixer(u, w_pool, scale):
    b, n, c = u.shape
    ng, pg, _ = w_pool.shape
    wbd = jnp.zeros((c, c), F32)
    for gi in range(ng):
        wbd = wbd.at[gi * pg:(gi + 1) * pg, gi * pg:(gi + 1) * pg].set(w_pool[gi])
    ch = min(n, 256)
    return pl.pallas_call(
        functools.partial(_pool_kernel, n=n, ch=ch),
        grid=(b,),
        in_specs=[pl.BlockSpec((1, n, c), lambda i: (i, 0, 0)),
                  pl.BlockSpec((c, c), lambda i: (0, 0)),
                  pl.BlockSpec((1, c), lambda i: (0, 0))],
        out_specs=pl.BlockSpec((1, n, c), lambda i: (i, 0, 0)),
        out_shape=jax.ShapeDtypeStruct((b, n, c), F32),
        scratch_shapes=[pltpu.VMEM((n + 2 * HALO, c), F32)],
        compiler_params=_cparams("arbitrary"),
        name="pool_mixer",
    )(u, wbd.astype(BF16), scale[None, :])


def _conformer_kernel(p_ref, dw_ref, dwb_ref, lg_ref, lb_ref, pw_ref, pwb_ref, o_ref, pad_ref, *, n, ch):
    c = GROUP

    def glu(i, carry):
        r0 = pl.multiple_of(i * ch, ch)
        a = p_ref[0, pl.ds(r0, ch), 0:c]
        g = p_ref[0, pl.ds(r0, ch), c:2 * c]
        pad_ref[pl.ds(HALO + r0, ch), :] = a * _sigmoid(g)
        return carry
    _fill_padded(pad_ref, n, lambda: lax.fori_loop(0, n // ch, glu, 0))
    half = (CONV_KERNEL - 1) // 2

    def body(i, carry):
        r0 = pl.multiple_of(i * ch, ch)
        acc = jnp.zeros((ch, c), F32) + dwb_ref[...]
        taps = _shifted_rows(pad_ref, r0, ch, range(-half, half + 1))
        for k in range(CONV_KERNEL):
            acc = acc + taps[k - half] * dw_ref[k:k + 1, :]
        mu = jnp.mean(acc, axis=-1, keepdims=True)
        xc = acc - mu
        var = jnp.mean(xc * xc, axis=-1, keepdims=True)
        y = xc * lax.rsqrt(var + LN_EPS) * lg_ref[...] + lb_ref[...]
        y = y * _sigmoid(y)
        out = _dot(y.astype(BF16), pw_ref[...]) + pwb_ref[...]
        o_ref[0, pl.ds(r0, ch), :] = out.astype(o_ref.dtype)
        return carry
    lax.fori_loop(0, n // ch, body, 0)


def conformer_mixer(p, dw_w, dw_b, ln_g, ln_b, pw_w, pw_b):
    b, n, c2 = p.shape
    c = c2 // 2
    ch = min(n, 128)
    kp = 32
    dw = jnp.pad(dw_w, ((0, kp - CONV_KERNEL), (0, 0)))
    vec = lambda: pl.BlockSpec((1, c), lambda i: (0, 0))
    return pl.pallas_call(
        functools.partial(_conformer_kernel, n=n, ch=ch),
        grid=(b,),
        in_specs=[pl.BlockSpec((1, n, c2), lambda i: (i, 0, 0)),
                  pl.BlockSpec((kp, c), lambda i: (0, 0)),
                  vec(), vec(), vec(),
                  pl.BlockSpec((c, c), lambda i: (0, 0)),
                  vec()],
        out_specs=pl.BlockSpec((1, n, c), lambda i: (i, 0, 0)),
        out_shape=jax.ShapeDtypeStruct((b, n, c), F32),
        scratch_shapes=[pltpu.VMEM((n + 2 * HALO, c), F32)],
        compiler_params=_cparams("arbitrary"),
        name="conformer_mixer",
    )(p, dw, dw_b[None, :], ln_g[None, :], ln_b[None, :], pw_w.astype(BF16), pw_b[None, :])


def _short_conv_kernel(p_ref, w_ref, b_ref, o_ref, pad_ref, *, n, ch):
    def copy(i, carry):
        r0 = pl.multiple_of(i * ch, ch)
        pad_ref[pl.ds(HALO + r0, ch), :] = p_ref[0, pl.ds(r0, ch), :]
        return carry
    _fill_padded(pad_ref, n, lambda: lax.fori_loop(0, n // ch, copy, 0))
    half = (HYENA_SHORT - 1) // 2

    def body(i, carry):
        r0 = pl.multiple_of(i * ch, ch)
        acc = jnp.zeros((ch, pad_ref.shape[1]), F32) + b_ref[...]
        taps = _shifted_rows(pad_ref, r0, ch, range(-half, half + 1))
        for k in range(HYENA_SHORT):
            acc = acc + taps[k - half] * w_ref[k:k + 1, :]
        o_ref[0, pl.ds(r0, ch), :] = acc.astype(o_ref.dtype)
        return carry
    lax.fori_loop(0, n // ch, body, 0)


def hyena_short_conv(p, w, bias):
    b, n, c = p.shape
    ch = min(n, 128)
    wp = jnp.pad(w, ((0, SUBLANES - HYENA_SHORT), (0, 0)))
    return pl.pallas_call(
        functools.partial(_short_conv_kernel, n=n, ch=ch),
        grid=(b,),
        in_specs=[pl.BlockSpec((1, n, c), lambda i: (i, 0, 0)),
                  pl.BlockSpec((SUBLANES, c), lambda i: (0, 0)),
                  pl.BlockSpec((1, c), lambda i: (0, 0))],
        out_specs=pl.BlockSpec((1, n, c), lambda i: (i, 0, 0)),
        out_shape=jax.ShapeDtypeStruct((b, n, c), BF16),
        scratch_shapes=[pltpu.VMEM((n + 2 * HALO, c), F32)],
        compiler_params=_cparams("arbitrary"),
        name="hyena_short_conv",
    )(p, wp, bias[None, :])


def _toeplitz_conv(g_ref, u, nb, tb, bsz):
    ys = [jnp.zeros((bsz, tb), F32) for _ in range(nb)]
    for d in range(-(nb - 1), nb):
        start = tb * (nb - 1 + d)
        win = jnp.broadcast_to(g_ref[:, start:start + 2 * tb], (tb, 2 * tb))
        rolled = pltpu.roll(win, 0, 1, stride=1, stride_axis=0)
        blk = rolled[:, tb:2 * tb].astype(BF16)
        j_lo, j_hi = max(0, -d), min(nb, nb - d)
        o = _dot(u[j_lo * bsz:j_hi * bsz], blk)
        for j in range(j_lo, j_hi):
            ys[j + d] = ys[j + d] + o[(j - j_lo) * bsz:(j - j_lo + 1) * bsz]
    return ys


def _hyena_kernel(bias_ref, g_ref, v_ref, x1_ref, x2_ref, o_ref, *, nb, tb, bsz, cb):
    base = pl.program_id(0) * cb

    def body(ci, carry):
        v = v_ref[ci]
        y1 = jnp.concatenate(_toeplitz_conv(g_ref.at[0, ci], v, nb, tb, bsz), axis=0)
        z = x1_ref[ci].astype(F32) * (y1 + bias_ref[0, base + ci] * v.astype(F32))
        y2 = jnp.concatenate(_toeplitz_conv(g_ref.at[1, ci], z.astype(BF16), nb, tb, bsz), axis=0)
        out = x2_ref[ci].astype(F32) * (y2 + bias_ref[1, base + ci] * z)
        o_ref[ci] = out.astype(o_ref.dtype)
        return carry
    lax.fori_loop(0, cb, body, 0, unroll=2)


def hyena_long_conv(u, g_rows, bias):
    b, n, _ = u.shape
    tb = min(n, 256)
    nb = n // tb
    nbb = nb * b
    ut = jnp.transpose(u.reshape(b, nb, tb, 3, GROUP), (3, 4, 1, 0, 2)).reshape(3, GROUP, nbb, tb)
    cb = 8
    blk = lambda part: pl.BlockSpec((None, cb, nbb, tb), lambda i, part=part: (part, i, 0, 0))
    out = pl.pallas_call(
        functools.partial(_hyena_kernel, nb=nb, tb=tb, bsz=b, cb=cb),
        grid=(GROUP // cb,),
        in_specs=[pl.BlockSpec(memory_space=pltpu.SMEM),
                  pl.BlockSpec((2, cb, 1, 2 * n), lambda i: (0, i, 0, 0)),
                  blk(0), blk(1), blk(2)],
        out_specs=pl.BlockSpec((cb, nbb, tb), lambda i: (i, 0, 0)),
        out_shape=jax.ShapeDtypeStruct((GROUP, nbb, tb), F32),
        compiler_params=_cparams("arbitrary"),
        name="hyena_long_conv",
    )(bias, g_rows, ut, ut, ut)
    return jnp.transpose(out.reshape(GROUP, nb, b, tb), (2, 1, 3, 0)).reshape(b, n, GROUP)


def _out_proj_kernel(x_ref, gt_ref, a_ref, h_ref, p_ref, c_ref, w_ref, o_ref):
    acc = _dot(a_ref[0].astype(BF16), w_ref[0:GROUP, :])
    acc = acc + _dot(h_ref[0].astype(BF16), w_ref[GROUP:2 * GROUP, :])
    acc = acc + _dot(p_ref[0].astype(BF16), w_ref[2 * GROUP:3 * GROUP, :])
    acc = acc + _dot(c_ref[0].astype(BF16), w_ref[3 * GROUP:4 * GROUP, :])
    o_ref[0] = x_ref[0] + gt_ref[0] * acc


def out_projection(x, gate, parts, w_out):
    b, l, d = x.shape
    tm = min(l, 512)
    part = lambda: pl.BlockSpec((1, tm, GROUP), lambda i, j: (i, j, 0))
    return pl.pallas_call(
        _out_proj_kernel,
        grid=(b, l // tm),
        in_specs=[pl.BlockSpec((1, tm, d), lambda i, j: (i, j, 0)),
                  pl.BlockSpec((1, 1, d), lambda i, j: (i, 0, 0)),
                  part(), part(), part(), part(),
                  pl.BlockSpec(w_out.shape, lambda i, j: (0, 0))],
        out_specs=pl.BlockSpec((1, tm, d), lambda i, j: (i, j, 0)),
        out_shape=jax.ShapeDtypeStruct((b, l, d), F32),
        compiler_params=_cparams("arbitrary", "arbitrary"),
        name="out_proj",
    )(x, gate, *parts, w_out.astype(BF16))


SLAB = 8


def _store_slab(ref, val):
    rows = val.shape[0]
    for s in range(SLAB):
        ref[pl.ds(s, rows, stride=SLAB), :] = val[:, s * LANES:(s + 1) * LANES]


def _load_slab(ref, start, rows):
    return jnp.concatenate([ref[pl.ds(start * SLAB + s, rows, stride=SLAB), :] for s in range(SLAB)], axis=1)


def _router_kernel(x_ref, sh_ref, sc_ref, g_ref, wh_ref, wl_ref, rb_ref, h_ref, idx_ref, gate_ref):
    x = x_ref[0]
    ms = jnp.mean(x * x, axis=-1, keepdims=True)
    h = (x * lax.rsqrt(ms + NORM_EPS)) * g_ref[...] * (1.0 + sc_ref[0]) + sh_ref[0]
    _store_slab(h_ref, h)
    hh, hl = _split_bf16(h)
    logits = _dot(hh, wh_ref[...]) + (_dot(hh, wl_ref[...]) + _dot(hl, wh_ref[...])) + rb_ref[...]
    tm = x.shape[0]
    lane = lax.broadcasted_iota(jnp.int32, (tm, N_EXPERTS), 1).astype(F32)
    out_lane = lax.broadcasted_iota(jnp.int32, (tm, LANES), 1)
    vals, idxs = [], []
    cur = logits
    for _ in range(TOP_K):
        mx = jnp.max(cur, axis=-1, keepdims=True)
        ix = jnp.min(jnp.where(cur == mx, lane, float(N_EXPERTS)), axis=-1, keepdims=True)
        vals.append(mx)
        idxs.append(ix)
        cur = jnp.where(lane == ix, -jnp.inf, cur)
    es = [jnp.exp(v - vals[0]) for v in vals]
    inv = 1.0 / (es[0] + es[1] + es[2] + es[3])
    idx_out = jnp.zeros((tm, LANES), F32)
    gate_out = jnp.zeros((tm, LANES), F32)
    for k in range(TOP_K):
        idx_out = jnp.where(out_lane == k, idxs[k], idx_out)
        gate_out = jnp.where(out_lane == k, es[k] * inv, gate_out)
    idx_ref[0] = idx_out.astype(jnp.int32)
    gate_ref[0] = gate_out


def router(x, shift, scale, g, router_w, router_b):
    b, l, d = x.shape
    assert d == SLAB * LANES
    tm = min(l, 512)
    per_seq = l // tm
    wh, wl = _split_bf16(router_w)
    vec3 = lambda: pl.BlockSpec((1, 1, d), lambda i, j: (i, 0, 0))
    tile = lambda w: pl.BlockSpec((1, tm, w), lambda i, j: (i, j, 0))
    h, idx, gates = pl.pallas_call(
        _router_kernel,
        grid=(b, per_seq),
        in_specs=[tile(d), vec3(), vec3(),
                  pl.BlockSpec((1, d), lambda i, j: (0, 0)),
                  pl.BlockSpec(wh.shape, lambda i, j: (0, 0)),
                  pl.BlockSpec(wl.shape, lambda i, j: (0, 0)),
                  pl.BlockSpec((1, N_EXPERTS), lambda i, j: (0, 0))],
        out_specs=[pl.BlockSpec((tm * SLAB, LANES), lambda i, j: (i * per_seq + j, 0)), tile(LANES), tile(LANES)],
        out_shape=[jax.ShapeDtypeStruct((b * l * SLAB, LANES), F32),
                   jax.ShapeDtypeStruct((b, l, LANES), jnp.int32),
                   jax.ShapeDtypeStruct((b, l, LANES), F32)],
        compiler_params=_cparams("arbitrary", "arbitrary"),
        name="moe_router",
    )(x, shift, scale, g, wh, wl, router_b[None, :])
    return h, idx[..., :TOP_K], gates[..., :TOP_K]


DISPATCH_ROWS = 2048


def _dispatch_kernel(lo_ref, hi_ref, pos_ref, src_ref, dst_ref, zero_ref, sem, *, rows):
    i = pl.program_id(0)

    def copy(p):
        tok = lax.shift_right_logical(p, TOP_K.bit_length() - 1)
        src = src_ref.at[pl.ds(pl.multiple_of(tok * SLAB, SLAB), SLAB)]
        dst = dst_ref.at[pl.ds(pl.multiple_of(pos_ref[p], SLAB), SLAB)]
        return pltpu.make_async_copy(src, dst, sem)

    def start(p, carry):
        copy(p).start()
        return carry
    lax.fori_loop(0, rows, start, 0, unroll=8)

    def wait(p, carry):
        copy(p).wait()
        return carry
    lax.fori_loop(0, rows, wait, 0, unroll=8)

    @pl.when(i == pl.num_programs(0) - 1)
    def _():
        zero_ref[...] = jnp.zeros_like(zero_ref)

        def fill(r):
            return pltpu.make_async_copy(zero_ref, dst_ref.at[pl.ds(pl.multiple_of(r * SLAB, SLAB), SLAB)], sem)

        def per_expert(e, carry):
            def fill_start(r, c):
                fill(r).start()
                return c

            def fill_wait(r, c):
                fill(r).wait()
                return c
            lax.fori_loop(lo_ref[e], hi_ref[e], fill_start, 0)
            lax.fori_loop(lo_ref[e], hi_ref[e], fill_wait, 0)
            return carry
        lax.fori_loop(0, N_EXPERTS, per_expert, 0)


def dispatch_rows(h, pos_rows, pad_lo, pad_hi, n_slots):
    rows = DISPATCH_ROWS
    tokens = rows // TOP_K
    grid_spec = pltpu.PrefetchScalarGridSpec(
        num_scalar_prefetch=2,
        grid=(pos_rows.shape[0] // rows,),
        in_specs=[pl.BlockSpec((rows,), lambda i, lo, hi: (i,), memory_space=pltpu.SMEM),
                  pl.BlockSpec((tokens * SLAB, LANES), lambda i, lo, hi: (i, 0))],
        out_specs=pl.BlockSpec(memory_space=pl.ANY),
        scratch_shapes=[pltpu.VMEM((SLAB, LANES), h.dtype), pltpu.SemaphoreType.DMA(())])
    return pl.pallas_call(
        functools.partial(_dispatch_kernel, rows=rows),
        grid_spec=grid_spec,
        out_shape=jax.ShapeDtypeStruct((n_slots * SLAB, LANES), h.dtype),
        compiler_params=_cparams("arbitrary"),
        name="moe_dispatch",
    )(pad_lo, pad_hi, pos_rows, h)


def _expert_kernel(te_ref, tv_ref, x_ref, w1_ref, b1_ref, w2_ref, b2_ref, o_ref, *, tm):
    i = pl.program_id(0)
    ff = w2_ref.shape[1]

    @pl.when(tv_ref[i] > 0)
    def _():
        x = _load_slab(x_ref, 0, tm).astype(BF16)
        hid = _dot(x, w1_ref[0]) + b1_ref[0]
        gate = jnp.minimum(hid[:, :ff], SWIGLU_LIMIT)
        up = jnp.clip(hid[:, ff:], -SWIGLU_LIMIT, SWIGLU_LIMIT)
        act = gate * _sigmoid(SWIGLU_ALPHA * gate) * (up + 1.0)
        _store_slab(o_ref, _dot(act.astype(BF16), w2_ref[0]) + b2_ref[0])

    @pl.when(tv_ref[i] == 0)
    def _():
        o_ref[...] = jnp.zeros_like(o_ref)


def expert_ffn(xs, tile_expert, tile_valid, w1, b1, w2, b2, tm):
    n_slots = xs.shape[0] // SLAB
    ne, d, ff2 = w1.shape
    ff = w2.shape[1]
    slab = lambda: pl.BlockSpec((tm * SLAB, LANES), lambda i, te, tv: (i, 0))
    grid_spec = pltpu.PrefetchScalarGridSpec(
        num_scalar_prefetch=2,
        grid=(n_slots // tm,),
        in_specs=[pl.BlockSpec((tm * SLAB, LANES), lambda i, te, tv: (i * tv[i], 0)),
                  pl.BlockSpec((1, d, ff2), lambda i, te, tv: (te[i], 0, 0)),
                  pl.BlockSpec((1, 1, ff2), lambda i, te, tv: (te[i], 0, 0)),
                  pl.BlockSpec((1, ff, d), lambda i, te, tv: (te[i], 0, 0)),
                  pl.BlockSpec((1, 1, d), lambda i, te, tv: (te[i], 0, 0))],
        out_specs=slab())
    return pl.pallas_call(
        functools.partial(_expert_kernel, tm=tm),
        grid_spec=grid_spec,
        out_shape=jax.ShapeDtypeStruct((n_slots * SLAB, LANES), F32),
        compiler_params=_cparams("arbitrary"),
        name="moe_experts",
    )(tile_expert, tile_valid, xs, w1, b1.reshape(ne, 1, ff2), w2, b2.reshape(ne, 1, d))


def _combine_kernel(pos_ref, nxt_ref, x_ref, gt_ref, gates_ref, fin_ref, ys_ref, o_ref, buf, sems, *, tc, final):
    i = pl.program_id(0)
    n = tc * TOP_K
    slot = i % 2

    def copy(idx_ref, r, s):
        src = ys_ref.at[pl.ds(pl.multiple_of(idx_ref[r], SLAB), SLAB)]
        return pltpu.make_async_copy(src, buf.at[s, pl.ds(pl.multiple_of(r * SLAB, SLAB), SLAB)], sems.at[s])

    def issue(idx_ref, s):
        def body(r, carry):
            copy(idx_ref, r, s).start()
            return carry
        lax.fori_loop(0, n, body, 0, unroll=8)

    @pl.when(i == 0)
    def _():
        issue(pos_ref, 0)

    @pl.when(i + 1 < pl.num_programs(0))
    def _():
        issue(nxt_ref, 1 - slot)

    def drain(r, carry):
        copy(pos_ref, r, slot).wait()
        return carry
    lax.fori_loop(0, n, drain, 0, unroll=8)
    gates = gates_ref[...]
    y = jnp.zeros((tc, SLAB * LANES), F32)
    for k in range(TOP_K):
        y = y + gates[:, k:k + 1] * _load_slab(buf.at[slot], k * tc, tc)
    out = x_ref[...] + gt_ref[0] * y
    if final:
        ms = jnp.mean(out * out, axis=-1, keepdims=True)
        out = out * lax.rsqrt(ms + NORM_EPS) * fin_ref[...]
    o_ref[...] = out


def combine_rows(x, gate_vec, gates, pos, ys, final_g, final):
    b, l, d = x.shape
    t = b * l
    tc = min(l, 256)
    steps = t // tc
    per_seq = l // tc
    n = tc * TOP_K
    pos_t = jnp.transpose(pos.reshape(steps, tc, TOP_K), (0, 2, 1)).reshape(steps * n) * SLAB
    gates_p = jnp.pad(gates.reshape(t, TOP_K), ((0, 0), (0, LANES - TOP_K)))
    out = pl.pallas_call(
        functools.partial(_combine_kernel, tc=tc, final=final),
        grid=(steps,),
        in_specs=[pl.BlockSpec((n,), lambda i: (i,), memory_space=pltpu.SMEM),
                  pl.BlockSpec((n,), lambda i: (jnp.minimum(i + 1, steps - 1),), memory_space=pltpu.SMEM),
                  pl.BlockSpec((tc, d), lambda i: (i, 0)),
                  pl.BlockSpec((1, 1, d), lambda i: (i // per_seq, 0, 0)),
                  pl.BlockSpec((tc, LANES), lambda i: (i, 0)),
                  pl.BlockSpec((1, d), lambda i: (0, 0)),
                  pl.BlockSpec(memory_space=pl.ANY)],
        out_specs=pl.BlockSpec((tc, d), lambda i: (i, 0)),
        out_shape=jax.ShapeDtypeStruct((t, d), F32),
        scratch_shapes=[pltpu.VMEM((2, n * SLAB, LANES), F32), pltpu.SemaphoreType.DMA((2,))],
        compiler_params=_cparams("arbitrary"),
        name="moe_combine",
    )(pos_t, pos_t, x.reshape(t, d), gate_vec, gates_p, final_g[None, :], ys)
    return out.reshape(b, l, d)


def routing_plan(idx, tm):
    t = idx.shape[0]
    n = t * TOP_K
    e = idx.reshape(n)
    onehot = (e[:, None] == jnp.arange(N_EXPERTS, dtype=e.dtype)[None, :]).astype(jnp.int32)
    csum = jnp.cumsum(onehot, axis=0)
    counts = csum[-1]
    rank = jnp.take_along_axis(csum, e[:, None], axis=1)[:, 0] - 1
    padded = ((counts + tm - 1) // tm) * tm
    ends = jnp.cumsum(padded)
    starts = ends - padded
    pos = starts[e] + rank
    n_slots = -(-(n + N_EXPERTS * (tm - 1)) // tm) * tm
    tile_start = jnp.arange(n_slots // tm, dtype=jnp.int32) * tm
    tile_expert = jnp.minimum(jnp.sum((tile_start[:, None] >= ends[None, :]).astype(jnp.int32), axis=1), N_EXPERTS - 1)
    tile_valid = (tile_start < ends[-1]).astype(jnp.int32)
    return (pos.reshape(t, TOP_K).astype(jnp.int32), (starts + counts).astype(jnp.int32), ends.astype(jnp.int32),
            n_slots, tile_expert.astype(jnp.int32), tile_valid)


MOE_TILE = 512


def moe_block(x, shift, scale, gate_vec, g, router_w, router_b, w1, b1, w2, b2, final_g, final):
    b, l, d = x.shape
    h, idx, gates = router(x, shift, scale, g, router_w, router_b)
    pos, pad_lo, pad_hi, n_slots, tile_expert, tile_valid = routing_plan(idx.reshape(b * l, TOP_K), MOE_TILE)
    xs = dispatch_rows(h, pos.reshape(-1) * SLAB, pad_lo, pad_hi, n_slots)
    ys = expert_ffn(xs, tile_expert, tile_valid, w1, b1, w2, b2, MOE_TILE)
    return combine_rows(x, gate_vec, gates, pos, ys, final_g, final)


def _diff_lambda(lp, lam_init):
    lp = lp.astype(F32)
    return jnp.exp(jnp.sum(lp[0] * lp[1])) - jnp.exp(jnp.sum(lp[2] * lp[3])) + lam_init


def kernel(x, c, ctx, c_ctx, norm1, norm2, w_mod, b_mod, w_in, w_out, attn_lambda, attn_subln,
           hyena_short_w, hyena_short_b, hyena_w1, hyena_b1, hyena_w2, hyena_b2, hyena_w3, hyena_freq,
           hyena_bias, pool_w, pool_scale, conv_dw_w, conv_dw_b, conv_ln_g, conv_ln_b, conv_pw_w,
           conv_pw_b, router_w, router_b, moe_w1, moe_b1, moe_w2, moe_b2, final_norm):
    bsz, n_lat, d = x.shape
    n_ctx = ctx.shape[1]
    depth = w_mod.shape[0]
    g = GROUP
    off_hy, off_pool, off_conv = 3 * g, 6 * g, 7 * g

    rows = -(-(bsz + 1) // SUBLANES) * SUBLANES
    cvec = jnp.zeros((rows, d), F32).at[:bsz].set(c).at[bsz].set(c_ctx)
    mods = adaln_all(cvec, w_mod, b_mod)

    cos_t, sin_t, partner = rope_tables(n_lat)
    ones_t = jnp.ones((n_ctx, g), F32)
    zeros_t = jnp.zeros((n_ctx, g), F32)
    qk_scale = HEAD_QK ** -0.5 * math.log2(math.e)

    xc = ctx
    for i in range(depth):
        last = i == depth - 1
        m = mods[i]
        lat = [m[:bsz, None, k * d:(k + 1) * d] for k in range(6)]
        cm = [jnp.broadcast_to(m[bsz, k * d:(k + 1) * d][None, None, :], (bsz, 1, d)) for k in range(6)]
        lam_init = 0.8 - 0.6 * math.exp(-0.3 * i)
        lam = _diff_lambda(attn_lambda[i], lam_init)
        wi = w_in[i]
        g1 = norm1[i][None, :]
        g2 = norm2[i][None, :]
        w1b = moe_w1[i].astype(BF16)
        w2b = moe_w2[i].astype(BF16)

        w_lat = jnp.concatenate([wi, wi[:, partner], wi[:, g + partner]], axis=1).astype(BF16)
        nin = wi.shape[1]
        segs = [(0, g, nin, qk_scale), (g, g, nin + g, 1.0), (2 * g, g, None, 1.0),
                (off_hy, 3 * g, None, 1.0), (off_pool, g, None, 1.0), (off_conv, 2 * g, None, 1.0)]
        q, k, v, p_hy, p_pool, p_conv = modulated_projection(
            x, lat[0], lat[1], g1, w_lat, segs, [BF16, BF16, BF16, F32, F32, F32], cos_t, sin_t)

        if last:
            w_ctx = wi[:, g:3 * g].astype(BF16)
            kc, vc = modulated_projection(xc, cm[0], cm[1], g1, w_ctx,
                                          [(0, g, None, 1.0), (g, g, None, 1.0)], [BF16, BF16], ones_t, zeros_t)
        else:
            csegs = [(0, g, None, qk_scale), (g, g, None, 1.0), (2 * g, g, None, 1.0),
                     (off_hy, 3 * g, None, 1.0), (off_pool, g, None, 1.0), (off_conv, 2 * g, None, 1.0)]
            qc, kc, vc, pc_hy, pc_pool, pc_conv = modulated_projection(
                xc, cm[0], cm[1], g1, wi.astype(BF16), csegs, [BF16, BF16, BF16, F32, F32, F32], ones_t, zeros_t)

        kt_all = jnp.transpose(jnp.concatenate([kc, k], axis=1), (0, 2, 1))
        v_all = jnp.concatenate([vc, v], axis=1)
        o_attn = diff_attention(q, kt_all, v_all, lam, attn_subln[i], 1.0 - lam_init)

        filt_params = (hyena_w1[i], hyena_b1[i], hyena_w2[i], hyena_b2[i], hyena_w3[i], hyena_freq[i])

        def local_groups(p_hy_, p_pool_, p_conv_, n_tokens):
            g_rows = filter_rows(hyena_filters(n_tokens, *filt_params), n_tokens)
            u = hyena_short_conv(p_hy_, hyena_short_w[i], hyena_short_b[i])
            o_hy = hyena_long_conv(u, g_rows, hyena_bias[i])
            o_pool = pool_mixer(p_pool_, pool_w[i], pool_scale[i])
            o_conv = conformer_mixer(p_conv_, conv_dw_w[i], conv_dw_b[i], conv_ln_g[i], conv_ln_b[i],
                                     conv_pw_w[i], conv_pw_b[i])
            return o_hy, o_pool, o_conv

        o_hy, o_pool, o_conv = local_groups(p_hy, p_pool, p_conv, n_lat)
        x = out_projection(x, lat[2], (o_attn, o_hy, o_pool, o_conv), w_out[i])
        if not last:
            oc_attn = diff_attention(qc, jnp.transpose(kc, (0, 2, 1)), vc, lam, attn_subln[i], 1.0 - lam_init)
            oc_hy, oc_pool, oc_conv = local_groups(pc_hy, pc_pool, pc_conv, n_ctx)
            xc = out_projection(xc, cm[2], (oc_attn, oc_hy, oc_pool, oc_conv), w_out[i])

        x = moe_block(x, lat[3], lat[4], lat[5], g2, router_w[i], router_b[i], w1b, moe_b1[i], w2b, moe_b2[i],
                      final_norm, last)
        if not last:
            xc = moe_block(xc, cm[3], cm[4], cm[5], g2, router_w[i], router_b[i], w1b, moe_b1[i], w2b, moe_b2[i],
                           final_norm, False)
    return x
```

```python
import functools
import math

import jax
import jax.numpy as jnp
from jax import lax
from jax.experimental import pallas as pl
from jax.experimental.pallas import tpu as pltpu

F32 = jnp.float32
BF16 = jnp.bfloat16

GRID_W = 64
N_HEADS = 4
HEAD_V = 64
HEAD_QK = 32
GROUP = 256
ROPE_BASE = 10000.0
SUBLN_EPS = 1e-5
NORM_EPS = 1e-6
LN_EPS = 1e-5
POOL_WINDOWS = (2, 4, 8, 16)
CONV_KERNEL = 31
HYENA_SHORT = 3
HYENA_EMB = 33
HYENA_TARGET = 1e-2
HYENA_FAST_PCT = 0.3
HYENA_SLOW_PCT = 1.5
N_EXPERTS = 32
TOP_K = 4
SWIGLU_ALPHA = 1.702
SWIGLU_LIMIT = 7.0

LANES = 128
SUBLANES = 8
VMEM_LIMIT = 56 * 1024 * 1024
HALO = 16


def _cparams(*sem):
    return pltpu.CompilerParams(dimension_semantics=sem, vmem_limit_bytes=VMEM_LIMIT)


def _split_bf16(a):
    hi = a.astype(BF16)
    lo = (a - hi.astype(F32)).astype(BF16)
    return hi, lo


def _dot(a, b):
    return jnp.dot(a, b, preferred_element_type=F32)


def _dot3(a, b):
    ah, al = _split_bf16(a)
    bh, bl = _split_bf16(b)
    return _dot(ah, bh) + (_dot(ah, bl) + _dot(al, bh))


def _sigmoid(x):
    return 1.0 / (1.0 + jnp.exp(-x))


def _adaln_kernel(c_ref, w_ref, b_ref, o_ref):
    cv = c_ref[...]
    s = cv * _sigmoid(cv)
    o_ref[0] = _dot3(s, w_ref[0]) + b_ref[0]


def adaln_all(cvec, w_mod, b_mod):
    depth, d, n = w_mod.shape
    r = cvec.shape[0]
    tn = 1536
    return pl.pallas_call(
        _adaln_kernel,
        grid=(depth, n // tn),
        in_specs=[pl.BlockSpec((r, d), lambda i, j: (0, 0)),
                  pl.BlockSpec((1, d, tn), lambda i, j: (i, 0, j)),
                  pl.BlockSpec((1, 1, tn), lambda i, j: (i, 0, j))],
        out_specs=pl.BlockSpec((1, r, tn), lambda i, j: (i, 0, j)),
        out_shape=jax.ShapeDtypeStruct((depth, r, n), F32),
        compiler_params=_cparams("arbitrary", "arbitrary"),
        name="adaln",
    )(cvec, w_mod, b_mod.reshape(depth, 1, n))


def _filter_kernel(feat_ref, w1_ref, b1_ref, w2_ref, b2_ref, w3_ref, fr_ref, dec_ref, o_ref):
    fr = fr_ref[...]
    z = jnp.sin(fr * (_dot3(feat_ref[...], w1_ref[...]) + b1_ref[...]))
    z = jnp.sin(fr * (_dot3(z, w2_ref[...]) + b2_ref[...]))
    o_ref[...] = _dot3(z, w3_ref[...]) * dec_ref[...]


def hyena_filters(n_tokens, w1, b1, w2, b2, w3, freq):
    bands = (HYENA_EMB - 1) // 2
    t = jnp.linspace(0.0, 1.0, n_tokens, dtype=F32)[:, None]
    w = 2.0 * math.pi * jnp.arange(n_tokens, dtype=F32)[:, None] / n_tokens
    f = jnp.linspace(1e-4, bands - 1, bands, dtype=F32)[None, :]
    feat = jnp.concatenate([t, jnp.cos(f * w), -jnp.sin(f * w)], axis=-1)
    kpad = 40
    feat = jnp.pad(feat, ((0, 0), (0, kpad - HYENA_EMB)))
    w1p = jnp.pad(w1, ((0, kpad - HYENA_EMB), (0, 0)))
    min_decay = math.log(HYENA_TARGET) / HYENA_SLOW_PCT
    max_decay = math.log(HYENA_TARGET) / HYENA_FAST_PCT
    deltas = jnp.abs(jnp.linspace(min_decay, max_decay, GROUP, dtype=F32))
    decay = jnp.exp(-t * deltas)
    nf = w3.shape[1]
    decay = jnp.tile(decay, (1, nf // GROUP))
    ffn = w2.shape[0]
    tl = min(n_tokens, 512)
    full = lambda shape: pl.BlockSpec(shape, lambda i: (0, 0))
    return pl.pallas_call(
        _filter_kernel,
        grid=(n_tokens // tl,),
        in_specs=[pl.BlockSpec((tl, kpad), lambda i: (i, 0)),
                  full((kpad, ffn)), full((1, ffn)), full((ffn, ffn)), full((1, ffn)),
                  full((ffn, nf)), full((1, ffn)),
                  pl.BlockSpec((tl, nf), lambda i: (i, 0))],
        out_specs=pl.BlockSpec((tl, nf), lambda i: (i, 0)),
        out_shape=jax.ShapeDtypeStruct((n_tokens, nf), F32),
        compiler_params=_cparams("arbitrary"),
        name="hyena_filter",
    )(feat, w1p, b1[None, :], w2, b2[None, :], w3, freq[None, :], decay)


def filter_rows(filt, n_tokens):
    order = filt.shape[1] // (2 * GROUP)
    f4 = filt.reshape(n_tokens, order, 2, GROUP)
    hf = f4[:, :, 0]
    hb = f4[:, :, 1]
    rows = jnp.concatenate([jnp.zeros_like(hf[:1]), hb[:0:-1], hf], axis=0)
    return jnp.transpose(rows, (1, 2, 0))[:, :, None, :]


def _proj_kernel(x_ref, sh_ref, sc_ref, g_ref, w_ref, cos_ref, sin_ref, *out_refs, segs):
    x = x_ref[0]
    ms = jnp.mean(x * x, axis=-1, keepdims=True)
    h = (x * lax.rsqrt(ms + NORM_EPS)) * g_ref[...] * (1.0 + sc_ref[0]) + sh_ref[0]
    hb = h.astype(BF16)
    for o_ref, (start, width, swap_start, scale) in zip(out_refs, segs):
        p = _dot(hb, w_ref[:, start:start + width])
        if swap_start is not None:
            ps = _dot(hb, w_ref[:, swap_start:swap_start + width])
            p = p * cos_ref[...] + ps * sin_ref[...]
        if scale != 1.0:
            p = p * scale
        o_ref[0] = p.astype(o_ref.dtype)


def modulated_projection(x, shift, scale, g, w, segs, out_dtypes, cos_t, sin_t):
    b, l, d = x.shape
    tm = min(l, 512)
    nw = w.shape[1]
    rw = cos_t.shape[1]
    outs = [jax.ShapeDtypeStruct((b, l, s[1]), dt) for s, dt in zip(segs, out_dtypes)]
    return pl.pallas_call(
        functools.partial(_proj_kernel, segs=tuple(segs)),
        grid=(b, l // tm),
        in_specs=[pl.BlockSpec((1, tm, d), lambda i, j: (i, j, 0)),
                  pl.BlockSpec((1, 1, d), lambda i, j: (i, 0, 0)),
                  pl.BlockSpec((1, 1, d), lambda i, j: (i, 0, 0)),
                  pl.BlockSpec((1, d), lambda i, j: (0, 0)),
                  pl.BlockSpec((d, nw), lambda i, j: (0, 0)),
                  pl.BlockSpec((tm, rw), lambda i, j: (j, 0)),
                  pl.BlockSpec((tm, rw), lambda i, j: (j, 0))],
        out_specs=[pl.BlockSpec((1, tm, s[1]), lambda i, j: (i, j, 0)) for s in segs],
        out_shape=outs,
        compiler_params=_cparams("arbitrary", "arbitrary"),
        name="mod_proj",
    )(x, shift, scale, g, w, cos_t, sin_t)


def rope_tables(n_tokens):
    rows = n_tokens // GRID_W
    row = jnp.repeat(jnp.arange(rows, dtype=F32), GRID_W)
    col = jnp.tile(jnp.arange(GRID_W, dtype=F32), rows)
    n_freq = HEAD_QK // 4
    inv = ROPE_BASE ** (-jnp.arange(n_freq, dtype=F32) / n_freq)
    lane = jnp.arange(GROUP)
    d = lane % HEAD_QK
    pos = jnp.where((d < 2 * n_freq)[None, :], row[:, None], col[:, None])
    ang = pos * inv[d % n_freq][None, :]
    first = (d % (2 * n_freq)) < n_freq
    cos_t = jnp.cos(ang)
    sin_t = jnp.where(first[None, :], -jnp.sin(ang), jnp.sin(ang))
    partner = jnp.where(first, lane + n_freq, lane - n_freq)
    return cos_t, sin_t, partner


ATTN_ROWS = 16


def _attn_kernel(lam_ref, q_ref, kt_ref, v_ref, g_ref, o_ref, s_even, s_odd, a_even, a_odd, *, out_scale):
    q = q_ref[0]
    lam = lam_ref[0]
    tq = q.shape[0]
    lane = lax.broadcasted_iota(jnp.int32, (1, GROUP), 1)
    map_of_lane = lane // HEAD_QK
    head_of_lane = lane // HEAD_V
    s_bufs = (s_even, s_odd)
    a_bufs = (a_even, a_odd)

    z = 0

    def scores(h):
        for m in range(2):
            keep = jnp.where(map_of_lane == 2 * h + m, 1.0, 0.0).astype(BF16)
            s_bufs[h % 2][z + m] = _dot(q * keep, kt_ref[0])

    def softmax(h):
        s_ref, a_ref = s_bufs[h % 2], a_bufs[h % 2]
        inv = []
        for r in range(0, tq, ATTN_ROWS):
            s1 = s_ref[z + 1, r:r + ATTN_ROWS, :]
            e1 = jnp.exp2(s1 - jnp.max(s1, axis=-1, keepdims=True))
            sum1 = jnp.sum(e1, axis=-1, keepdims=True)
            s0 = s_ref[z, r:r + ATTN_ROWS, :]
            e0 = jnp.exp2(s0 - jnp.max(s0, axis=-1, keepdims=True))
            sum0 = jnp.sum(e0, axis=-1, keepdims=True)
            a_ref[z, r:r + ATTN_ROWS, :] = (e0 - e1 * (lam * sum0 / sum1)).astype(BF16)
            inv.append(1.0 / sum0)
        return jnp.concatenate(inv, axis=0)

    acc = jnp.zeros((tq, GROUP), F32)
    scores(0)
    for h in range(N_HEADS):
        if h + 1 < N_HEADS:
            scores(h + 1)
        inv0 = softmax(h)
        o = _dot(a_bufs[h % 2][z], v_ref[0]) * inv0
        acc = jnp.where(head_of_lane == h, o, acc)
    sq = acc * acc
    ms = jnp.zeros((tq, GROUP), F32)
    for h in range(N_HEADS):
        hsel = head_of_lane == h
        mh = jnp.sum(jnp.where(hsel, sq, 0.0), axis=-1, keepdims=True) * (1.0 / HEAD_V)
        ms = jnp.where(hsel, mh, ms)
    y = acc * lax.rsqrt(ms + SUBLN_EPS) * g_ref[...] * out_scale
    o_ref[0] = y.astype(o_ref.dtype)


def diff_attention(q, kt, v, lam, subln, out_scale):
    b, nq, _ = q.shape
    nk = v.shape[1]
    tq = min(nq, 256)
    g = jnp.tile(subln, N_HEADS)[None, :]
    return pl.pallas_call(
        functools.partial(_attn_kernel, out_scale=out_scale),
        grid=(b, nq // tq),
        in_specs=[pl.BlockSpec(memory_space=pltpu.SMEM),
                  pl.BlockSpec((1, tq, GROUP), lambda i, j: (i, j, 0)),
                  pl.BlockSpec((1, GROUP, nk), lambda i, j: (i, 0, 0)),
                  pl.BlockSpec((1, nk, GROUP), lambda i, j: (i, 0, 0)),
                  pl.BlockSpec((1, GROUP), lambda i, j: (0, 0))],
        out_specs=pl.BlockSpec((1, tq, GROUP), lambda i, j: (i, j, 0)),
        out_shape=jax.ShapeDtypeStruct((b, nq, GROUP), F32),
        scratch_shapes=[pltpu.VMEM((2, tq, nk), F32), pltpu.VMEM((2, tq, nk), F32),
                        pltpu.VMEM((1, tq, nk), BF16), pltpu.VMEM((1, tq, nk), BF16)],
        compiler_params=_cparams("arbitrary", "arbitrary"),
        name="diff_attn",
    )(lam.reshape(1), q, kt, v, g)


def _fill_padded(pad_ref, n, write_rows):
    zeros = jnp.zeros((HALO, pad_ref.shape[1]), F32)
    pad_ref[0:HALO, :] = zeros
    pad_ref[HALO + n:HALO + n + HALO, :] = zeros
    write_rows()


def _shifted_rows(pad_ref, r0, ch, offsets):
    rows = ch + 2 * HALO
    blk = pad_ref[pl.ds(r0, rows), :]
    rolled = {0: blk}
    out = {}
    for k in offsets:
        o = HALO + k
        r = o % SUBLANES
        if r not in rolled:
            rolled[r] = pltpu.roll(blk, rows - r, 0)
        out[k] = rolled[r][o - r:o - r + ch]
    return out


def _pool_kernel(u_ref, w_ref, sc_ref, o_ref, pad_ref, *, n, ch):
    def copy(i, c):
        r0 = pl.multiple_of(i * ch, ch)
        pad_ref[pl.ds(HALO + r0, ch), :] = u_ref[0, pl.ds(r0, ch), :]
        return c
    _fill_padded(pad_ref, n, lambda: lax.fori_loop(0, n // ch, copy, 0))
    lane = lax.broadcasted_iota(jnp.int32, (1, GROUP), 1)
    pg = GROUP // len(POOL_WINDOWS)
    reach = POOL_WINDOWS[-1] // 2

    def body(i, c):
        r0 = pl.multiple_of(i * ch, ch)
        taps = _shifted_rows(pad_ref, r0, ch, range(-reach, reach))
        ld = lambda k: taps[k]
        t = r0 + lax.broadcasted_iota(jnp.int32, (ch, 1), 0)
        u = ld(0)
        sums = []
        s = u + ld(-1)
        sums.append(s)
        half = 1
        for _ in POOL_WINDOWS[1:]:
            for k in range(half, 2 * half):
                s = s + ld(k) + ld(-k - 1)
            half *= 2
            sums.append(s)
        pooled = None
        for gi, win in reversed(list(enumerate(POOL_WINDOWS))):
            hi = jnp.minimum(t + (win - win // 2), n)
            lo = jnp.maximum(t - win // 2, 0)
            mean = sums[gi] / (hi - lo).astype(F32)
            pooled = mean if pooled is None else jnp.where(lane < (gi + 1) * pg, mean, pooled)
        pooled = pooled - u
        y = _dot(pooled.astype(BF16), w_ref[...]) * sc_ref[...]
        o_ref[0, pl.ds(r0, ch), :] = y.astype(o_ref.dtype)
        return c
    lax.fori_loop(0, n // ch, body, 0)


def pool_mixer(u, w_pool, scale):
    b, n, c = u.shape
    ng, pg, _ = w_pool.shape
    wbd = jnp.zeros((c, c), F32)
    for gi in range(ng):
        wbd = wbd.at[gi * pg:(gi + 1) * pg, gi * pg:(gi + 1) * pg].set(w_pool[gi])
    ch = min(n, 256)
    return pl.pallas_call(
        functools.partial(_pool_kernel, n=n, ch=ch),
        grid=(b,),
        in_specs=[pl.BlockSpec((1, n, c), lambda i: (i, 0, 0)),
                  pl.BlockSpec((c, c), lambda i: (0, 0)),
                  pl.BlockSpec((1, c), lambda i: (0, 0))],
        out_specs=pl.BlockSpec((1, n, c), lambda i: (i, 0, 0)),
        out_shape=jax.ShapeDtypeStruct((b, n, c), F32),
        scratch_shapes=[pltpu.VMEM((n + 2 * HALO, c), F32)],
        compiler_params=_cparams("arbitrary"),
        name="pool_mixer",
    )(u, wbd.astype(BF16), scale[None, :])


def _conformer_kernel(p_ref, dw_ref, dwb_ref, lg_ref, lb_ref, pw_ref, pwb_ref, o_ref, pad_ref, *, n, ch):
    c = GROUP

    def glu(i, carry):
        r0 = pl.multiple_of(i * ch, ch)
        a = p_ref[0, pl.ds(r0, ch), 0:c]
        g = p_ref[0, pl.ds(r0, ch), c:2 * c]
        pad_ref[pl.ds(HALO + r0, ch), :] = a * _sigmoid(g)
        return carry
    _fill_padded(pad_ref, n, lambda: lax.fori_loop(0, n // ch, glu, 0))
    half = (CONV_KERNEL - 1) // 2

    def body(i, carry):
        r0 = pl.multiple_of(i * ch, ch)
        acc = jnp.zeros((ch, c), F32) + dwb_ref[...]
        taps = _shifted_rows(pad_ref, r0, ch, range(-half, half + 1))
        for k in range(CONV_KERNEL):
            acc = acc + taps[k - half] * dw_ref[k:k + 1, :]
        mu = jnp.mean(acc, axis=-1, keepdims=True)
        xc = acc - mu
        var = jnp.mean(xc * xc, axis=-1, keepdims=True)
        y = xc * lax.rsqrt(var + LN_EPS) * lg_ref[...] + lb_ref[...]
        y = y * _sigmoid(y)
        out = _dot(y.astype(BF16), pw_ref[...]) + pwb_ref[...]
        o_ref[0, pl.ds(r0, ch), :] = out.astype(o_ref.dtype)
        return carry
    lax.fori_loop(0, n // ch, body, 0)


def conformer_mixer(p, dw_w, dw_b, ln_g, ln_b, pw_w, pw_b):
    b, n, c2 = p.shape
    c = c2 // 2
    ch = min(n, 128)
    kp = 32
    dw = jnp.pad(dw_w, ((0, kp - CONV_KERNEL), (0, 0)))
    vec = lambda: pl.BlockSpec((1, c), lambda i: (0, 0))
    return pl.pallas_call(
        functools.partial(_conformer_kernel, n=n, ch=ch),
        grid=(b,),
        in_specs=[pl.BlockSpec((1, n, c2), lambda i: (i, 0, 0)),
                  pl.BlockSpec((kp, c), lambda i: (0, 0)),
                  vec(), vec(), vec(),
                  pl.BlockSpec((c, c), lambda i: (0, 0)),
                  vec()],
        out_specs=pl.BlockSpec((1, n, c), lambda i: (i, 0, 0)),
        out_shape=jax.ShapeDtypeStruct((b, n, c), F32),
        scratch_shapes=[pltpu.VMEM((n + 2 * HALO, c), F32)],
        compiler_params=_cparams("arbitrary"),
        name="conformer_mixer",
    )(p, dw, dw_b[None, :], ln_g[None, :], ln_b[None, :], pw_w.astype(BF16), pw_b[None, :])


def _short_conv_kernel(p_ref, w_ref, b_ref, o_ref, pad_ref, *, n, ch):
    def copy(i, carry):
        r0 = pl.multiple_of(i * ch, ch)
        pad_ref[pl.ds(HALO + r0, ch), :] = p_ref[0, pl.ds(r0, ch), :]
        return carry
    _fill_padded(pad_ref, n, lambda: lax.fori_loop(0, n // ch, copy, 0))
    half = (HYENA_SHORT - 1) // 2

    def body(i, carry):
        r0 = pl.multiple_of(i * ch, ch)
        acc = jnp.zeros((ch, pad_ref.shape[1]), F32) + b_ref[...]
        taps = _shifted_rows(pad_ref, r0, ch, range(-half, half + 1))
        for k in range(HYENA_SHORT):
            acc = acc + taps[k - half] * w_ref[k:k + 1, :]
        o_ref[0, pl.ds(r0, ch), :] = acc.astype(o_ref.dtype)
        return carry
    lax.fori_loop(0, n // ch, body, 0)


def hyena_short_conv(p, w, bias):
    b, n, c = p.shape
    ch = min(n, 128)
    wp = jnp.pad(w, ((0, SUBLANES - HYENA_SHORT), (0, 0)))
    return pl.pallas_call(
        functools.partial(_short_conv_kernel, n=n, ch=ch),
        grid=(b,),
        in_specs=[pl.BlockSpec((1, n, c), lambda i: (i, 0, 0)),
                  pl.BlockSpec((SUBLANES, c), lambda i: (0, 0)),
                  pl.BlockSpec((1, c), lambda i: (0, 0))],
        out_specs=pl.BlockSpec((1, n, c), lambda i: (i, 0, 0)),
        out_shape=jax.ShapeDtypeStruct((b, n, c), BF16),
        scratch_shapes=[pltpu.VMEM((n + 2 * HALO, c), F32)],
        compiler_params=_cparams("arbitrary"),
        name="hyena_short_conv",
    )(p, wp, bias[None, :])


def _toeplitz_conv(g_ref, u, nb, tb, bsz):
    ys = [jnp.zeros((bsz, tb), F32) for _ in range(nb)]
    for d in range(-(nb - 1), nb):
        start = tb * (nb - 1 + d)
        win = jnp.broadcast_to(g_ref[:, start:start + 2 * tb], (tb, 2 * tb))
        rolled = pltpu.roll(win, 0, 1, stride=1, stride_axis=0)
        blk = rolled[:, tb:2 * tb].astype(BF16)
        j_lo, j_hi = max(0, -d), min(nb, nb - d)
        o = _dot(u[j_lo * bsz:j_hi * bsz], blk)
        for j in range(j_lo, j_hi):
            ys[j + d] = ys[j + d] + o[(j - j_lo) * bsz:(j - j_lo + 1) * bsz]
    return ys


def _hyena_kernel(bias_ref, g_ref, v_ref, x1_ref, x2_ref, o_ref, *, nb, tb, bsz, cb):
    base = pl.program_id(0) * cb

    def body(ci, carry):
        v = v_ref[ci]
        y1 = jnp.concatenate(_toeplitz_conv(g_ref.at[0, ci], v, nb, tb, bsz), axis=0)
        z = x1_ref[ci].astype(F32) * (y1 + bias_ref[0, base + ci] * v.astype(F32))
        y2 = jnp.concatenate(_toeplitz_conv(g_ref.at[1, ci], z.astype(BF16), nb, tb, bsz), axis=0)
        out = x2_ref[ci].astype(F32) * (y2 + bias_ref[1, base + ci] * z)
        o_ref[ci] = out.astype(o_ref.dtype)
        return carry
    lax.fori_loop(0, cb, body, 0, unroll=2)


def hyena_long_conv(u, g_rows, bias):
    b, n, _ = u.shape
    tb = min(n, 256)
    nb = n // tb
    nbb = nb * b
    ut = jnp.transpose(u.reshape(b, nb, tb, 3, GROUP), (3, 4, 1, 0, 2)).reshape(3, GROUP, nbb, tb)
    cb = 8
    blk = lambda part: pl.BlockSpec((None, cb, nbb, tb), lambda i, part=part: (part, i, 0, 0))
    out = pl.pallas_call(
        functools.partial(_hyena_kernel, nb=nb, tb=tb, bsz=b, cb=cb),
        grid=(GROUP // cb,),
        in_specs=[pl.BlockSpec(memory_space=pltpu.SMEM),
                  pl.BlockSpec((2, cb, 1, 2 * n), lambda i: (0, i, 0, 0)),
                  blk(0), blk(1), blk(2)],
        out_specs=pl.BlockSpec((cb, nbb, tb), lambda i: (i, 0, 0)),
        out_shape=jax.ShapeDtypeStruct((GROUP, nbb, tb), F32),
        compiler_params=_cparams("arbitrary"),
        name="hyena_long_conv",
    )(bias, g_rows, ut, ut, ut)
    return jnp.transpose(out.reshape(GROUP, nb, b, tb), (2, 1, 3, 0)).reshape(b, n, GROUP)


def _out_proj_kernel(x_ref, gt_ref, a_ref, h_ref, p_ref, c_ref, w_ref, o_ref):
    acc = _dot(a_ref[0].astype(BF16), w_ref[0:GROUP, :])
    acc = acc + _dot(h_ref[0].astype(BF16), w_ref[GROUP:2 * GROUP, :])
    acc = acc + _dot(p_ref[0].astype(BF16), w_ref[2 * GROUP:3 * GROUP, :])
    acc = acc + _dot(c_ref[0].astype(BF16), w_ref[3 * GROUP:4 * GROUP, :])
    o_ref[0] = x_ref[0] + gt_ref[0] * acc


def out_projection(x, gate, parts, w_out):
    b, l, d = x.shape
    tm = min(l, 512)
    part = lambda: pl.BlockSpec((1, tm, GROUP), lambda i, j: (i, j, 0))
    return pl.pallas_call(
        _out_proj_kernel,
        grid=(b, l // tm),
        in_specs=[pl.BlockSpec((1, tm, d), lambda i, j: (i, j, 0)),
                  pl.BlockSpec((1, 1, d), lambda i, j: (i, 0, 0)),
                  part(), part(), part(), part(),
                  pl.BlockSpec(w_out.shape, lambda i, j: (0, 0))],
        out_specs=pl.BlockSpec((1, tm, d), lambda i, j: (i, j, 0)),
        out_shape=jax.ShapeDtypeStruct((b, l, d), F32),
        compiler_params=_cparams("arbitrary", "arbitrary"),
        name="out_proj",
    )(x, gate, *parts, w_out.astype(BF16))


SLAB = 8


def _store_slab(ref, val, start=0):
    rows = val.shape[0]
    for s in range(SLAB):
        ref[pl.ds(start * SLAB + s, rows, stride=SLAB), :] = val[:, s * LANES:(s + 1) * LANES]


def _load_slab(ref, start, rows):
    return jnp.concatenate([ref[pl.ds(start * SLAB + s, rows, stride=SLAB), :] for s in range(SLAB)], axis=1)


def _router_kernel(x_ref, sh_ref, sc_ref, g_ref, wh_ref, wl_ref, rb_ref, h_ref, idx_ref, gate_ref):
    x = x_ref[0]
    ms = jnp.mean(x * x, axis=-1, keepdims=True)
    h = (x * lax.rsqrt(ms + NORM_EPS)) * g_ref[...] * (1.0 + sc_ref[0]) + sh_ref[0]
    _store_slab(h_ref, h)
    hh, hl = _split_bf16(h)
    logits = _dot(hh, wh_ref[...]) + (_dot(hh, wl_ref[...]) + _dot(hl, wh_ref[...])) + rb_ref[...]
    tm = x.shape[0]
    lane = lax.broadcasted_iota(jnp.int32, (tm, N_EXPERTS), 1).astype(F32)
    out_lane = lax.broadcasted_iota(jnp.int32, (tm, LANES), 1)
    vals, idxs = [], []
    cur = logits
    for _ in range(TOP_K):
        mx = jnp.max(cur, axis=-1, keepdims=True)
        ix = jnp.min(jnp.where(cur == mx, lane, float(N_EXPERTS)), axis=-1, keepdims=True)
        vals.append(mx)
        idxs.append(ix)
        cur = jnp.where(lane == ix, -jnp.inf, cur)
    es = [jnp.exp(v - vals[0]) for v in vals]
    inv = 1.0 / (es[0] + es[1] + es[2] + es[3])
    idx_out = jnp.zeros((tm, LANES), F32)
    gate_out = jnp.zeros((tm, LANES), F32)
    for k in range(TOP_K):
        idx_out = jnp.where(out_lane == k, idxs[k], idx_out)
        gate_out = jnp.where(out_lane == k, es[k] * inv, gate_out)
    idx_ref[0] = idx_out.astype(jnp.int32)
    gate_ref[0] = gate_out


def router(x, shift, scale, g, router_w, router_b):
    b, l, d = x.shape
    assert d == SLAB * LANES
    tm = min(l, 512)
    per_seq = l // tm
    wh, wl = _split_bf16(router_w)
    vec3 = lambda: pl.BlockSpec((1, 1, d), lambda i, j: (i, 0, 0))
    tile = lambda w: pl.BlockSpec((1, tm, w), lambda i, j: (i, j, 0))
    h, idx, gates = pl.pallas_call(
        _router_kernel,
        grid=(b, per_seq),
        in_specs=[tile(d), vec3(), vec3(),
                  pl.BlockSpec((1, d), lambda i, j: (0, 0)),
                  pl.BlockSpec(wh.shape, lambda i, j: (0, 0)),
                  pl.BlockSpec(wl.shape, lambda i, j: (0, 0)),
                  pl.BlockSpec((1, N_EXPERTS), lambda i, j: (0, 0))],
        out_specs=[pl.BlockSpec((tm * SLAB, LANES), lambda i, j: (i * per_seq + j, 0)), tile(LANES), tile(LANES)],
        out_shape=[jax.ShapeDtypeStruct((b * l * SLAB, LANES), F32),
                   jax.ShapeDtypeStruct((b, l, LANES), jnp.int32),
                   jax.ShapeDtypeStruct((b, l, LANES), F32)],
        compiler_params=_cparams("arbitrary", "arbitrary"),
        name="moe_router",
    )(x, shift, scale, g, wh, wl, router_b[None, :])
    return h, idx[..., :TOP_K], gates[..., :TOP_K]


DISPATCH_ROWS = 2048


def _dispatch_kernel(lo_ref, hi_ref, pos_ref, src_ref, dst_ref, zero_ref, sem, *, rows):
    i = pl.program_id(0)

    def copy(tok, k):
        src = src_ref.at[pl.ds(pl.multiple_of(tok * SLAB, SLAB), SLAB)]
        dst = dst_ref.at[pl.ds(pl.multiple_of(pos_ref[tok * TOP_K + k], SLAB), SLAB)]
        return pltpu.make_async_copy(src, dst, sem)

    def start(tok, carry):
        for k in range(TOP_K):
            copy(tok, k).start(priority=k % 2)
        return carry
    lax.fori_loop(0, rows // TOP_K, start, 0, unroll=2)

    def wait(tok, carry):
        for k in range(TOP_K):
            copy(tok, k).wait()
        return carry
    lax.fori_loop(0, rows // TOP_K, wait, 0, unroll=2)

    @pl.when(i == pl.num_programs(0) - 1)
    def _():
        zero_ref[...] = jnp.zeros_like(zero_ref)

        def fill(r):
            return pltpu.make_async_copy(zero_ref, dst_ref.at[pl.ds(pl.multiple_of(r * SLAB, SLAB), SLAB)], sem)

        def per_expert(e, carry):
            def fill_start(r, c):
                fill(r).start()
                return c

            def fill_wait(r, c):
                fill(r).wait()
                return c
            lax.fori_loop(lo_ref[e], hi_ref[e], fill_start, 0)
            lax.fori_loop(lo_ref[e], hi_ref[e], fill_wait, 0)
            return carry
        lax.fori_loop(0, N_EXPERTS, per_expert, 0)


def dispatch_rows(h, pos_rows, pad_lo, pad_hi, n_slots):
    rows = DISPATCH_ROWS
    tokens = rows // TOP_K
    grid_spec = pltpu.PrefetchScalarGridSpec(
        num_scalar_prefetch=2,
        grid=(pos_rows.shape[0] // rows,),
        in_specs=[pl.BlockSpec((rows,), lambda i, lo, hi: (i,), memory_space=pltpu.SMEM),
                  pl.BlockSpec((tokens * SLAB, LANES), lambda i, lo, hi: (i, 0))],
        out_specs=pl.BlockSpec(memory_space=pl.ANY),
        scratch_shapes=[pltpu.VMEM((SLAB, LANES), h.dtype), pltpu.SemaphoreType.DMA(())])
    return pl.pallas_call(
        functools.partial(_dispatch_kernel, rows=rows),
        grid_spec=grid_spec,
        out_shape=jax.ShapeDtypeStruct((n_slots * SLAB, LANES), h.dtype),
        compiler_params=_cparams("arbitrary"),
        name="moe_dispatch",
    )(pad_lo, pad_hi, pos_rows, h)


def _expert_kernel(te_ref, tv_ref, x_ref, w1_ref, b1_ref, w2_ref, b2_ref, o_ref, *, tm):
    i = pl.program_id(0)
    ff = w2_ref.shape[1]

    @pl.when(tv_ref[i] > 0)
    def _():
        x = _load_slab(x_ref, 0, tm).astype(BF16)
        hid = _dot(x, w1_ref[0]) + b1_ref[0]
        gate = jnp.minimum(hid[:, :ff], SWIGLU_LIMIT)
        up = jnp.clip(hid[:, ff:], -SWIGLU_LIMIT, SWIGLU_LIMIT)
        act = gate * _sigmoid(SWIGLU_ALPHA * gate) * (up + 1.0)
        _store_slab(o_ref, _dot(act.astype(BF16), w2_ref[0]) + b2_ref[0])

    @pl.when(tv_ref[i] == 0)
    def _():
        o_ref[...] = jnp.zeros_like(o_ref)


def expert_ffn(xs, tile_expert, tile_valid, w1, b1, w2, b2, tm):
    n_slots = xs.shape[0] // SLAB
    ne, d, ff2 = w1.shape
    ff = w2.shape[1]
    slab = lambda: pl.BlockSpec((tm * SLAB, LANES), lambda i, te, tv: (i, 0))
    grid_spec = pltpu.PrefetchScalarGridSpec(
        num_scalar_prefetch=2,
        grid=(n_slots // tm,),
        in_specs=[pl.BlockSpec((tm * SLAB, LANES), lambda i, te, tv: (i * tv[i], 0)),
                  pl.BlockSpec((1, d, ff2), lambda i, te, tv: (te[i], 0, 0)),
                  pl.BlockSpec((1, 1, ff2), lambda i, te, tv: (te[i], 0, 0)),
                  pl.BlockSpec((1, ff, d), lambda i, te, tv: (te[i], 0, 0)),
                  pl.BlockSpec((1, 1, d), lambda i, te, tv: (te[i], 0, 0))],
        out_specs=slab())
    return pl.pallas_call(
        functools.partial(_expert_kernel, tm=tm),
        grid_spec=grid_spec,
        out_shape=jax.ShapeDtypeStruct((n_slots * SLAB, LANES), F32),
        compiler_params=_cparams("arbitrary"),
        name="moe_experts",
    )(tile_expert, tile_valid, xs, w1, b1.reshape(ne, 1, ff2), w2, b2.reshape(ne, 1, d))


def _combine_kernel(pos_ref, nxt_ref, x_ref, gt_ref, gates_ref, fin_ref, ys_ref, o_ref, buf, sems, *, tc, final):
    i = pl.program_id(0)
    n = tc * TOP_K
    slot = i % 2

    def copy(idx_ref, r, s):
        src = ys_ref.at[pl.ds(pl.multiple_of(idx_ref[r], SLAB), SLAB)]
        return pltpu.make_async_copy(src, buf.at[s, pl.ds(pl.multiple_of(r * SLAB, SLAB), SLAB)], sems.at[s])

    def issue(idx_ref, s):
        def body(r2, carry):
            for j in range(2):
                copy(idx_ref, 2 * r2 + j, s).start(priority=j)
            return carry
        lax.fori_loop(0, n // 2, body, 0, unroll=4)

    @pl.when(i == 0)
    def _():
        issue(pos_ref, 0)

    @pl.when(i + 1 < pl.num_programs(0))
    def _():
        issue(nxt_ref, 1 - slot)

    def drain(r, carry):
        copy(pos_ref, r, slot).wait()
        return carry
    lax.fori_loop(0, n, drain, 0, unroll=8)
    gates = gates_ref[...]
    y = jnp.zeros((tc, SLAB * LANES), F32)
    for k in range(TOP_K):
        y = y + gates[:, k:k + 1] * _load_slab(buf.at[slot], k * tc, tc)
    out = x_ref[...] + gt_ref[0] * y
    if final:
        ms = jnp.mean(out * out, axis=-1, keepdims=True)
        out = out * lax.rsqrt(ms + NORM_EPS) * fin_ref[...]
    o_ref[...] = out


def combine_rows(x, gate_vec, gates, pos, ys, final_g, final):
    b, l, d = x.shape
    t = b * l
    tc = min(l, 256)
    steps = t // tc
    per_seq = l // tc
    n = tc * TOP_K
    pos_t = jnp.transpose(pos.reshape(steps, tc, TOP_K), (0, 2, 1)).reshape(steps * n) * SLAB
    gates_p = jnp.pad(gates.reshape(t, TOP_K), ((0, 0), (0, LANES - TOP_K)))
    out = pl.pallas_call(
        functools.partial(_combine_kernel, tc=tc, final=final),
        grid=(steps,),
        in_specs=[pl.BlockSpec((n,), lambda i: (i,), memory_space=pltpu.SMEM),
                  pl.BlockSpec((n,), lambda i: (jnp.minimum(i + 1, steps - 1),), memory_space=pltpu.SMEM),
                  pl.BlockSpec((tc, d), lambda i: (i, 0)),
                  pl.BlockSpec((1, 1, d), lambda i: (i // per_seq, 0, 0)),
                  pl.BlockSpec((tc, LANES), lambda i: (i, 0)),
                  pl.BlockSpec((1, d), lambda i: (0, 0)),
                  pl.BlockSpec(memory_space=pl.ANY)],
        out_specs=pl.BlockSpec((tc, d), lambda i: (i, 0)),
        out_shape=jax.ShapeDtypeStruct((t, d), F32),
        scratch_shapes=[pltpu.VMEM((2, n * SLAB, LANES), F32), pltpu.SemaphoreType.DMA((2,))],
        compiler_params=_cparams("arbitrary"),
        name="moe_combine",
    )(pos_t, pos_t, x.reshape(t, d), gate_vec, gates_p, final_g[None, :], ys)
    return out.reshape(b, l, d)


def routing_plan(idx, tm):
    t = idx.shape[0]
    n = t * TOP_K
    e = idx.reshape(n)
    onehot = (e[:, None] == jnp.arange(N_EXPERTS, dtype=e.dtype)[None, :]).astype(jnp.int32)
    csum = jnp.cumsum(onehot, axis=0)
    counts = csum[-1]
    rank = jnp.take_along_axis(csum, e[:, None], axis=1)[:, 0] - 1
    padded = ((counts + tm - 1) // tm) * tm
    ends = jnp.cumsum(padded)
    starts = ends - padded
    pos = starts[e] + rank
    n_slots = -(-(n + N_EXPERTS * (tm - 1)) // tm) * tm
    tile_start = jnp.arange(n_slots // tm, dtype=jnp.int32) * tm
    tile_expert = jnp.minimum(jnp.sum((tile_start[:, None] >= ends[None, :]).astype(jnp.int32), axis=1), N_EXPERTS - 1)
    tile_valid = (tile_start < ends[-1]).astype(jnp.int32)
    return (pos.reshape(t, TOP_K).astype(jnp.int32), (starts + counts).astype(jnp.int32), ends.astype(jnp.int32),
            n_slots, tile_expert.astype(jnp.int32), tile_valid)


MOE_TILE = 512


def moe_block(x, shift, scale, gate_vec, g, router_w, router_b, w1, b1, w2, b2, final_g, final):
    b, l, d = x.shape
    h, idx, gates = router(x, shift, scale, g, router_w, router_b)
    pos, pad_lo, pad_hi, n_slots, tile_expert, tile_valid = routing_plan(idx.reshape(b * l, TOP_K), MOE_TILE)
    xs = dispatch_rows(h, pos.reshape(-1) * SLAB, pad_lo, pad_hi, n_slots)
    ys = expert_ffn(xs, tile_expert, tile_valid, w1, b1, w2, b2, MOE_TILE)
    return combine_rows(x, gate_vec, gates, pos, ys, final_g, final)


def _diff_lambda(lp, lam_init):
    lp = lp.astype(F32)
    return jnp.exp(jnp.sum(lp[0] * lp[1])) - jnp.exp(jnp.sum(lp[2] * lp[3])) + lam_init


def kernel(x, c, ctx, c_ctx, norm1, norm2, w_mod, b_mod, w_in, w_out, attn_lambda, attn_subln,
           hyena_short_w, hyena_short_b, hyena_w1, hyena_b1, hyena_w2, hyena_b2, hyena_w3, hyena_freq,
           hyena_bias, pool_w, pool_scale, conv_dw_w, conv_dw_b, conv_ln_g, conv_ln_b, conv_pw_w,
           conv_pw_b, router_w, router_b, moe_w1, moe_b1, moe_w2, moe_b2, final_norm):
    bsz, n_lat, d = x.shape
    n_ctx = ctx.shape[1]
    depth = w_mod.shape[0]
    g = GROUP
    off_hy, off_pool, off_conv = 3 * g, 6 * g, 7 * g

    rows = -(-(bsz + 1) // SUBLANES) * SUBLANES
    cvec = jnp.zeros((rows, d), F32).at[:bsz].set(c).at[bsz].set(c_ctx)
    mods = adaln_all(cvec, w_mod, b_mod)

    cos_t, sin_t, partner = rope_tables(n_lat)
    ones_t = jnp.ones((n_ctx, g), F32)
    zeros_t = jnp.zeros((n_ctx, g), F32)
    qk_scale = HEAD_QK ** -0.5 * math.log2(math.e)

    xc = ctx
    for i in range(depth):
        last = i == depth - 1
        m = mods[i]
        lat = [m[:bsz, None, k * d:(k + 1) * d] for k in range(6)]
        cm = [jnp.broadcast_to(m[bsz, k * d:(k + 1) * d][None, None, :], (bsz, 1, d)) for k in range(6)]
        lam_init = 0.8 - 0.6 * math.exp(-0.3 * i)
        lam = _diff_lambda(attn_lambda[i], lam_init)
        wi = w_in[i]
        g1 = norm1[i][None, :]
        g2 = norm2[i][None, :]
        w1b = moe_w1[i].astype(BF16)
        w2b = moe_w2[i].astype(BF16)

        w_lat = jnp.concatenate([wi, wi[:, partner], wi[:, g + partner]], axis=1).astype(BF16)
        nin = wi.shape[1]
        segs = [(0, g, nin, qk_scale), (g, g, nin + g, 1.0), (2 * g, g, None, 1.0),
                (off_hy, 3 * g, None, 1.0), (off_pool, g, None, 1.0), (off_conv, 2 * g, None, 1.0)]
        q, k, v, p_hy, p_pool, p_conv = modulated_projection(
            x, lat[0], lat[1], g1, w_lat, segs, [BF16, BF16, BF16, F32, F32, F32], cos_t, sin_t)

        if last:
            w_ctx = wi[:, g:3 * g].astype(BF16)
            kc, vc = modulated_projection(xc, cm[0], cm[1], g1, w_ctx,
                                          [(0, g, None, 1.0), (g, g, None, 1.0)], [BF16, BF16], ones_t, zeros_t)
        else:
            csegs = [(0, g, None, qk_scale), (g, g, None, 1.0), (2 * g, g, None, 1.0),
                     (off_hy, 3 * g, None, 1.0), (off_pool, g, None, 1.0), (off_conv, 2 * g, None, 1.0)]
            qc, kc, vc, pc_hy, pc_pool, pc_conv = modulated_projection(
                xc, cm[0], cm[1], g1, wi.astype(BF16), csegs, [BF16, BF16, BF16, F32, F32, F32], ones_t, zeros_t)

        kt_all = jnp.transpose(jnp.concatenate([kc, k], axis=1), (0, 2, 1))
        v_all = jnp.concatenate([vc, v], axis=1)
        o_attn = diff_attention(q, kt_all, v_all, lam, attn_subln[i], 1.0 - lam_init)

        filt_params = (hyena_w1[i], hyena_b1[i], hyena_w2[i], hyena_b2[i], hyena_w3[i], hyena_freq[i])

        def local_groups(p_hy_, p_pool_, p_conv_, n_tokens):
            g_rows = filter_rows(hyena_filters(n_tokens, *filt_params), n_tokens)
            u = hyena_short_conv(p_hy_, hyena_short_w[i], hyena_short_b[i])
            o_hy = hyena_long_conv(u, g_rows, hyena_bias[i])
            o_pool = pool_mixer(p_pool_, pool_w[i], pool_scale[i])
            o_conv = conformer_mixer(p_conv_, conv_dw_w[i], conv_dw_b[i], conv_ln_g[i], conv_ln_b[i],
                                     conv_pw_w[i], conv_pw_b[i])
            return o_hy, o_pool, o_conv

        o_hy, o_pool, o_conv = local_groups(p_hy, p_pool, p_conv, n_lat)
        x = out_projection(x, lat[2], (o_attn, o_hy, o_pool, o_conv), w_out[i])
        if not last:
            oc_attn = diff_attention(qc, jnp.transpose(kc, (0, 2, 1)), vc, lam, attn_subln[i], 1.0 - lam_init)
            oc_hy, oc_pool, oc_conv = local_groups(pc_hy, pc_pool, pc_conv, n_ctx)
            xc = out_projection(xc, cm[2], (oc_attn, oc_hy, oc_pool, oc_conv), w_out[i])

        x = moe_block(x, lat[3], lat[4], lat[5], g2, router_w[i], router_b[i], w1b, moe_b1[i], w2b, moe_b2[i],
                      final_norm, last)
        if not last:
            xc = moe_block(xc, cm[3], cm[4], cm[5], g2, router_w[i], router_b[i], w1b, moe_b1[i], w2b, moe_b2[i],
                           final_norm, False)
    return x
```

```python
import functools
import math

import jax
import jax.numpy as jnp
from jax import lax
from jax.experimental import pallas as pl
from jax.experimental.pallas import tpu as pltpu

F32 = jnp.float32
BF16 = jnp.bfloat16

GRID_W = 64
N_HEADS = 4
HEAD_V = 64
HEAD_QK = 32
GROUP = 256
ROPE_BASE = 10000.0
SUBLN_EPS = 1e-5
NORM_EPS = 1e-6
LN_EPS = 1e-5
POOL_WINDOWS = (2, 4, 8, 16)
CONV_KERNEL = 31
HYENA_SHORT = 3
HYENA_EMB = 33
HYENA_TARGET = 1e-2
HYENA_FAST_PCT = 0.3
HYENA_SLOW_PCT = 1.5
N_EXPERTS = 32
TOP_K = 4
SWIGLU_ALPHA = 1.702
SWIGLU_LIMIT = 7.0

LANES = 128
SUBLANES = 8
VMEM_LIMIT = 56 * 1024 * 1024
HALO = 16


def _cparams(*sem):
    return pltpu.CompilerParams(dimension_semantics=sem, vmem_limit_bytes=VMEM_LIMIT)


def _split_bf16(a):
    hi = a.astype(BF16)
    lo = (a - hi.astype(F32)).astype(BF16)
    return hi, lo


def _dot(a, b):
    return jnp.dot(a, b, preferred_element_type=F32)


def _dot3(a, b):
    ah, al = _split_bf16(a)
    bh, bl = _split_bf16(b)
    return _dot(ah, bh) + (_dot(ah, bl) + _dot(al, bh))


def _sigmoid(x):
    return 1.0 / (1.0 + jnp.exp(-x))


def _adaln_kernel(c_ref, w_ref, b_ref, o_ref):
    cv = c_ref[...]
    s = cv * _sigmoid(cv)
    o_ref[0] = _dot3(s, w_ref[0]) + b_ref[0]


def adaln_all(cvec, w_mod, b_mod):
    depth, d, n = w_mod.shape
    r = cvec.shape[0]
    tn = 1536
    return pl.pallas_call(
        _adaln_kernel,
        grid=(depth, n // tn),
        in_specs=[pl.BlockSpec((r, d), lambda i, j: (0, 0)),
                  pl.BlockSpec((1, d, tn), lambda i, j: (i, 0, j)),
                  pl.BlockSpec((1, 1, tn), lambda i, j: (i, 0, j))],
        out_specs=pl.BlockSpec((1, r, tn), lambda i, j: (i, 0, j)),
        out_shape=jax.ShapeDtypeStruct((depth, r, n), F32),
        compiler_params=_cparams("arbitrary", "arbitrary"),
        name="adaln",
    )(cvec, w_mod, b_mod.reshape(depth, 1, n))


def _filter_kernel(feat_ref, w1_ref, b1_ref, w2_ref, b2_ref, w3_ref, fr_ref, dec_ref, o_ref):
    fr = fr_ref[...]
    z = jnp.sin(fr * (_dot3(feat_ref[...], w1_ref[...]) + b1_ref[...]))
    z = jnp.sin(fr * (_dot3(z, w2_ref[...]) + b2_ref[...]))
    o_ref[...] = _dot3(z, w3_ref[...]) * dec_ref[...]


def hyena_filters(n_tokens, w1, b1, w2, b2, w3, freq):
    bands = (HYENA_EMB - 1) // 2
    t = jnp.linspace(0.0, 1.0, n_tokens, dtype=F32)[:, None]
    w = 2.0 * math.pi * jnp.arange(n_tokens, dtype=F32)[:, None] / n_tokens
    f = jnp.linspace(1e-4, bands - 1, bands, dtype=F32)[None, :]
    feat = jnp.concatenate([t, jnp.cos(f * w), -jnp.sin(f * w)], axis=-1)
    kpad = 40
    feat = jnp.pad(feat, ((0, 0), (0, kpad - HYENA_EMB)))
    w1p = jnp.pad(w1, ((0, kpad - HYENA_EMB), (0, 0)))
    min_decay = math.log(HYENA_TARGET) / HYENA_SLOW_PCT
    max_decay = math.log(HYENA_TARGET) / HYENA_FAST_PCT
    deltas = jnp.abs(jnp.linspace(min_decay, max_decay, GROUP, dtype=F32))
    decay = jnp.exp(-t * deltas)
    nf = w3.shape[1]
    decay = jnp.tile(decay, (1, nf // GROUP))
    ffn = w2.shape[0]
    tl = min(n_tokens, 512)
    full = lambda shape: pl.BlockSpec(shape, lambda i: (0, 0))
    return pl.pallas_call(
        _filter_kernel,
        grid=(n_tokens // tl,),
        in_specs=[pl.BlockSpec((tl, kpad), lambda i: (i, 0)),
                  full((kpad, ffn)), full((1, ffn)), full((ffn, ffn)), full((1, ffn)),
                  full((ffn, nf)), full((1, ffn)),
                  pl.BlockSpec((tl, nf), lambda i: (i, 0))],
        out_specs=pl.BlockSpec((tl, nf), lambda i: (i, 0)),
        out_shape=jax.ShapeDtypeStruct((n_tokens, nf), F32),
        compiler_params=_cparams("arbitrary"),
        name="hyena_filter",
    )(feat, w1p, b1[None, :], w2, b2[None, :], w3, freq[None, :], decay)


def filter_rows(filt, n_tokens):
    order = filt.shape[1] // (2 * GROUP)
    f4 = filt.reshape(n_tokens, order, 2, GROUP)
    hf = f4[:, :, 0]
    hb = f4[:, :, 1]
    rows = jnp.concatenate([jnp.zeros_like(hf[:1]), hb[:0:-1], hf], axis=0)
    return jnp.transpose(rows, (1, 2, 0))[:, :, None, :]


def _proj_kernel(x_ref, sh_ref, sc_ref, g_ref, w_ref, cos_ref, sin_ref, *out_refs, segs):
    x = x_ref[0]
    ms = jnp.mean(x * x, axis=-1, keepdims=True)
    h = (x * lax.rsqrt(ms + NORM_EPS)) * g_ref[...] * (1.0 + sc_ref[0]) + sh_ref[0]
    hb = h.astype(BF16)
    for o_ref, (start, width, swap_start, scale) in zip(out_refs, segs):
        p = _dot(hb, w_ref[:, start:start + width])
        if swap_start is not None:
            ps = _dot(hb, w_ref[:, swap_start:swap_start + width])
            p = p * cos_ref[...] + ps * sin_ref[...]
        if scale != 1.0:
            p = p * scale
        o_ref[0] = p.astype(o_ref.dtype)


def modulated_projection(x, shift, scale, g, w, segs, out_dtypes, cos_t, sin_t):
    b, l, d = x.shape
    tm = min(l, 512)
    nw = w.shape[1]
    rw = cos_t.shape[1]
    outs = [jax.ShapeDtypeStruct((b, l, s[1]), dt) for s, dt in zip(segs, out_dtypes)]
    return pl.pallas_call(
        functools.partial(_proj_kernel, segs=tuple(segs)),
        grid=(b, l // tm),
        in_specs=[pl.BlockSpec((1, tm, d), lambda i, j: (i, j, 0)),
                  pl.BlockSpec((1, 1, d), lambda i, j: (i, 0, 0)),
                  pl.BlockSpec((1, 1, d), lambda i, j: (i, 0, 0)),
                  pl.BlockSpec((1, d), lambda i, j: (0, 0)),
                  pl.BlockSpec((d, nw), lambda i, j: (0, 0)),
                  pl.BlockSpec((tm, rw), lambda i, j: (j, 0)),
                  pl.BlockSpec((tm, rw), lambda i, j: (j, 0))],
        out_specs=[pl.BlockSpec((1, tm, s[1]), lambda i, j: (i, j, 0)) for s in segs],
        out_shape=outs,
        compiler_params=_cparams("arbitrary", "arbitrary"),
        name="mod_proj",
    )(x, shift, scale, g, w, cos_t, sin_t)


def rope_tables(n_tokens):
    rows = n_tokens // GRID_W
    row = jnp.repeat(jnp.arange(rows, dtype=F32), GRID_W)
    col = jnp.tile(jnp.arange(GRID_W, dtype=F32), rows)
    n_freq = HEAD_QK // 4
    inv = ROPE_BASE ** (-jnp.arange(n_freq, dtype=F32) / n_freq)
    lane = jnp.arange(GROUP)
    d = lane % HEAD_QK
    pos = jnp.where((d < 2 * n_freq)[None, :], row[:, None], col[:, None])
    ang = pos * inv[d % n_freq][None, :]
    first = (d % (2 * n_freq)) < n_freq
    cos_t = jnp.cos(ang)
    sin_t = jnp.where(first[None, :], -jnp.sin(ang), jnp.sin(ang))
    partner = jnp.where(first, lane + n_freq, lane - n_freq)
    return cos_t, sin_t, partner


ATTN_ROWS = 16


def _attn_kernel(lam_ref, q_ref, kt_ref, v_ref, g_ref, o_ref, s_even, s_odd, a_even, a_odd, *, out_scale):
    q = q_ref[0]
    lam = lam_ref[0]
    tq = q.shape[0]
    lane = lax.broadcasted_iota(jnp.int32, (1, GROUP), 1)
    map_of_lane = lane // HEAD_QK
    head_of_lane = lane // HEAD_V
    s_bufs = (s_even, s_odd)
    a_bufs = (a_even, a_odd)

    z = 0

    def scores(h):
        for m in range(2):
            keep = jnp.where(map_of_lane == 2 * h + m, 1.0, 0.0).astype(BF16)
            s_bufs[h % 2][z + m] = _dot(q * keep, kt_ref[0])

    def softmax(h):
        s_ref, a_ref = s_bufs[h % 2], a_bufs[h % 2]
        inv = []
        for r in range(0, tq, ATTN_ROWS):
            s1 = s_ref[z + 1, r:r + ATTN_ROWS, :]
            e1 = jnp.exp2(s1 - jnp.max(s1, axis=-1, keepdims=True))
            sum1 = jnp.sum(e1, axis=-1, keepdims=True)
            s0 = s_ref[z, r:r + ATTN_ROWS, :]
            e0 = jnp.exp2(s0 - jnp.max(s0, axis=-1, keepdims=True))
            sum0 = jnp.sum(e0, axis=-1, keepdims=True)
            a_ref[z, r:r + ATTN_ROWS, :] = (e0 - e1 * (lam * sum0 / sum1)).astype(BF16)
            inv.append(1.0 / sum0)
        return jnp.concatenate(inv, axis=0)

    acc = jnp.zeros((tq, GROUP), F32)
    scores(0)
    for h in range(N_HEADS):
        if h + 1 < N_HEADS:
            scores(h + 1)
        inv0 = softmax(h)
        o = _dot(a_bufs[h % 2][z], v_ref[0]) * inv0
        acc = jnp.where(head_of_lane == h, o, acc)
    sq = acc * acc
    ms = jnp.zeros((tq, GROUP), F32)
    for h in range(N_HEADS):
        hsel = head_of_lane == h
        mh = jnp.sum(jnp.where(hsel, sq, 0.0), axis=-1, keepdims=True) * (1.0 / HEAD_V)
        ms = jnp.where(hsel, mh, ms)
    y = acc * lax.rsqrt(ms + SUBLN_EPS) * g_ref[...] * out_scale
    o_ref[0] = y.astype(o_ref.dtype)


def diff_attention(q, kt, v, lam, subln, out_scale):
    b, nq, _ = q.shape
    nk = v.shape[1]
    tq = min(nq, 256)
    g = jnp.tile(subln, N_HEADS)[None, :]
    return pl.pallas_call(
        functools.partial(_attn_kernel, out_scale=out_scale),
        grid=(b, nq // tq),
        in_specs=[pl.BlockSpec(memory_space=pltpu.SMEM),
                  pl.BlockSpec((1, tq, GROUP), lambda i, j: (i, j, 0)),
                  pl.BlockSpec((1, GROUP, nk), lambda i, j: (i, 0, 0)),
                  pl.BlockSpec((1, nk, GROUP), lambda i, j: (i, 0, 0)),
                  pl.BlockSpec((1, GROUP), lambda i, j: (0, 0))],
        out_specs=pl.BlockSpec((1, tq, GROUP), lambda i, j: (i, j, 0)),
        out_shape=jax.ShapeDtypeStruct((b, nq, GROUP), F32),
        scratch_shapes=[pltpu.VMEM((2, tq, nk), F32), pltpu.VMEM((2, tq, nk), F32),
                        pltpu.VMEM((1, tq, nk), BF16), pltpu.VMEM((1, tq, nk), BF16)],
        compiler_params=_cparams("arbitrary", "arbitrary"),
        name="diff_attn",
    )(lam.reshape(1), q, kt, v, g)


def _fill_padded(pad_ref, n, write_rows):
    zeros = jnp.zeros((HALO, pad_ref.shape[1]), F32)
    pad_ref[0:HALO, :] = zeros
    pad_ref[HALO + n:HALO + n + HALO, :] = zeros
    write_rows()


def _shifted_rows(pad_ref, r0, ch, offsets):
    rows = ch + 2 * HALO
    blk = pad_ref[pl.ds(r0, rows), :]
    rolled = {0: blk}
    out = {}
    for k in offsets:
        o = HALO + k
        r = o % SUBLANES
        if r not in rolled:
            rolled[r] = pltpu.roll(blk, rows - r, 0)
        out[k] = rolled[r][o - r:o - r + ch]
    return out


def _pool_kernel(u_ref, w_ref, sc_ref, o_ref, pad_ref, *, n, ch):
    def copy(i, c):
        r0 = pl.multiple_of(i * ch, ch)
        pad_ref[pl.ds(HALO + r0, ch), :] = u_ref[0, pl.ds(r0, ch), :]
        return c
    _fill_padded(pad_ref, n, lambda: lax.fori_loop(0, n // ch, copy, 0))
    lane = lax.broadcasted_iota(jnp.int32, (1, GROUP), 1)
    pg = GROUP // len(POOL_WINDOWS)
    reach = POOL_WINDOWS[-1] // 2

    def body(i, c):
        r0 = pl.multiple_of(i * ch, ch)
        taps = _shifted_rows(pad_ref, r0, ch, range(-reach, reach))
        ld = lambda k: taps[k]
        t = r0 + lax.broadcasted_iota(jnp.int32, (ch, 1), 0)
        u = ld(0)
        sums = []
        s = u + ld(-1)
        sums.append(s)
        half = 1
        for _ in POOL_WINDOWS[1:]:
            for k in range(half, 2 * half):
                s = s + ld(k) + ld(-k - 1)
            half *= 2
            sums.append(s)
        pooled = None
        for gi, win in reversed(list(enumerate(POOL_WINDOWS))):
            hi = jnp.minimum(t + (win - win // 2), n)
            lo = jnp.maximum(t - win // 2, 0)
            mean = sums[gi] / (hi - lo).astype(F32)
            pooled = mean if pooled is None else jnp.where(lane < (gi + 1) * pg, mean, pooled)
        pooled = pooled - u
        y = _dot(pooled.astype(BF16), w_ref[...]) * sc_ref[...]
        o_ref[0, pl.ds(r0, ch), :] = y.astype(o_ref.dtype)
        return c
    lax.fori_loop(0, n // ch, body, 0)


def pool_mixer(u, w_pool, scale):
    b, n, c = u.shape
    ng, pg, _ = w_pool.shape
    wbd = jnp.zeros((c, c), F32)
    for gi in range(ng):
        wbd = wbd.at[gi * pg:(gi + 1) * pg, gi * pg:(gi + 1) * pg].set(w_pool[gi])
    ch = min(n, 256)
    return pl.pallas_call(
        functools.partial(_pool_kernel, n=n, ch=ch),
        grid=(b,),
        in_specs=[pl.BlockSpec((1, n, c), lambda i: (i, 0, 0)),
                  pl.BlockSpec((c, c), lambda i: (0, 0)),
                  pl.BlockSpec((1, c), lambda i: (0, 0))],
        out_specs=pl.BlockSpec((1, n, c), lambda i: (i, 0, 0)),
        out_shape=jax.ShapeDtypeStruct((b, n, c), F32),
        scratch_shapes=[pltpu.VMEM((n + 2 * HALO, c), F32)],
        compiler_params=_cparams("arbitrary"),
        name="pool_mixer",
    )(u, wbd.astype(BF16), scale[None, :])


def _conformer_kernel(p_ref, dw_ref, dwb_ref, lg_ref, lb_ref, pw_ref, pwb_ref, o_ref, pad_ref, *, n, ch):
    c = GROUP

    def glu(i, carry):
        r0 = pl.multiple_of(i * ch, ch)
        a = p_ref[0, pl.ds(r0, ch), 0:c]
        g = p_ref[0, pl.ds(r0, ch), c:2 * c]
        pad_ref[pl.ds(HALO + r0, ch), :] = a * _sigmoid(g)
        return carry
    _fill_padded(pad_ref, n, lambda: lax.fori_loop(0, n // ch, glu, 0))
    half = (CONV_KERNEL - 1) // 2

    def body(i, carry):
        r0 = pl.multiple_of(i * ch, ch)
        acc = jnp.zeros((ch, c), F32) + dwb_ref[...]
        taps = _shifted_rows(pad_ref, r0, ch, range(-half, half + 1))
        for k in range(CONV_KERNEL):
            acc = acc + taps[k - half] * dw_ref[k:k + 1, :]
        mu = jnp.mean(acc, axis=-1, keepdims=True)
        xc = acc - mu
        var = jnp.mean(xc * xc, axis=-1, keepdims=True)
        y = xc * lax.rsqrt(var + LN_EPS) * lg_ref[...] + lb_ref[...]
        y = y * _sigmoid(y)
        out = _dot(y.astype(BF16), pw_ref[...]) + pwb_ref[...]
        o_ref[0, pl.ds(r0, ch), :] = out.astype(o_ref.dtype)
        return carry
    lax.fori_loop(0, n // ch, body, 0)


def conformer_mixer(p, dw_w, dw_b, ln_g, ln_b, pw_w, pw_b):
    b, n, c2 = p.shape
    c = c2 // 2
    ch = min(n, 128)
    kp = 32
    dw = jnp.pad(dw_w, ((0, kp - CONV_KERNEL), (0, 0)))
    vec = lambda: pl.BlockSpec((1, c), lambda i: (0, 0))
    return pl.pallas_call(
        functools.partial(_conformer_kernel, n=n, ch=ch),
        grid=(b,),
        in_specs=[pl.BlockSpec((1, n, c2), lambda i: (i, 0, 0)),
                  pl.BlockSpec((kp, c), lambda i: (0, 0)),
                  vec(), vec(), vec(),
                  pl.BlockSpec((c, c), lambda i: (0, 0)),
                  vec()],
        out_specs=pl.BlockSpec((1, n, c), lambda i: (i, 0, 0)),
        out_shape=jax.ShapeDtypeStruct((b, n, c), F32),
        scratch_shapes=[pltpu.VMEM((n + 2 * HALO, c), F32)],
        compiler_params=_cparams("arbitrary"),
        name="conformer_mixer",
    )(p, dw, dw_b[None, :], ln_g[None, :], ln_b[None, :], pw_w.astype(BF16), pw_b[None, :])


def _short_conv_kernel(p_ref, w_ref, b_ref, o_ref, pad_ref, *, n, ch):
    def copy(i, carry):
        r0 = pl.multiple_of(i * ch, ch)
        pad_ref[pl.ds(HALO + r0, ch), :] = p_ref[0, pl.ds(r0, ch), :]
        return carry
    _fill_padded(pad_ref, n, lambda: lax.fori_loop(0, n // ch, copy, 0))
    half = (HYENA_SHORT - 1) // 2

    def body(i, carry):
        r0 = pl.multiple_of(i * ch, ch)
        acc = jnp.zeros((ch, pad_ref.shape[1]), F32) + b_ref[...]
        taps = _shifted_rows(pad_ref, r0, ch, range(-half, half + 1))
        for k in range(HYENA_SHORT):
            acc = acc + taps[k - half] * w_ref[k:k + 1, :]
        o_ref[0, pl.ds(r0, ch), :] = acc.astype(o_ref.dtype)
        return carry
    lax.fori_loop(0, n // ch, body, 0)


def hyena_short_conv(p, w, bias):
    b, n, c = p.shape
    ch = min(n, 128)
    wp = jnp.pad(w, ((0, SUBLANES - HYENA_SHORT), (0, 0)))
    return pl.pallas_call(
        functools.partial(_short_conv_kernel, n=n, ch=ch),
        grid=(b,),
        in_specs=[pl.BlockSpec((1, n, c), lambda i: (i, 0, 0)),
                  pl.BlockSpec((SUBLANES, c), lambda i: (0, 0)),
                  pl.BlockSpec((1, c), lambda i: (0, 0))],
        out_specs=pl.BlockSpec((1, n, c), lambda i: (i, 0, 0)),
        out_shape=jax.ShapeDtypeStruct((b, n, c), BF16),
        scratch_shapes=[pltpu.VMEM((n + 2 * HALO, c), F32)],
        compiler_params=_cparams("arbitrary"),
        name="hyena_short_conv",
    )(p, wp, bias[None, :])


def _toeplitz_conv(g_ref, u, nb, tb, bsz):
    ys = [jnp.zeros((bsz, tb), F32) for _ in range(nb)]
    for d in range(-(nb - 1), nb):
        start = tb * (nb - 1 + d)
        win = jnp.broadcast_to(g_ref[:, start:start + 2 * tb], (tb, 2 * tb))
        rolled = pltpu.roll(win, 0, 1, stride=1, stride_axis=0)
        blk = rolled[:, tb:2 * tb].astype(BF16)
        j_lo, j_hi = max(0, -d), min(nb, nb - d)
        o = _dot(u[j_lo * bsz:j_hi * bsz], blk)
        for j in range(j_lo, j_hi):
            ys[j + d] = ys[j + d] + o[(j - j_lo) * bsz:(j - j_lo + 1) * bsz]
    return ys


def _hyena_kernel(bias_ref, g_ref, v_ref, x1_ref, x2_ref, o_ref, *, nb, tb, bsz, cb):
    base = pl.program_id(0) * cb

    def body(ci, carry):
        v = v_ref[ci]
        y1 = jnp.concatenate(_toeplitz_conv(g_ref.at[0, ci], v, nb, tb, bsz), axis=0)
        z = x1_ref[ci].astype(F32) * (y1 + bias_ref[0, base + ci] * v.astype(F32))
        y2 = jnp.concatenate(_toeplitz_conv(g_ref.at[1, ci], z.astype(BF16), nb, tb, bsz), axis=0)
        out = x2_ref[ci].astype(F32) * (y2 + bias_ref[1, base + ci] * z)
        o_ref[ci] = out.astype(o_ref.dtype)
        return carry
    lax.fori_loop(0, cb, body, 0, unroll=2)


def hyena_long_conv(u, g_rows, bias):
    b, n, _ = u.shape
    tb = min(n, 256)
    nb = n // tb
    nbb = nb * b
    ut = jnp.transpose(u.reshape(b, nb, tb, 3, GROUP), (3, 4, 1, 0, 2)).reshape(3, GROUP, nbb, tb)
    cb = 8
    blk = lambda part: pl.BlockSpec((None, cb, nbb, tb), lambda i, part=part: (part, i, 0, 0))
    out = pl.pallas_call(
        functools.partial(_hyena_kernel, nb=nb, tb=tb, bsz=b, cb=cb),
        grid=(GROUP // cb,),
        in_specs=[pl.BlockSpec(memory_space=pltpu.SMEM),
                  pl.BlockSpec((2, cb, 1, 2 * n), lambda i: (0, i, 0, 0)),
                  blk(0), blk(1), blk(2)],
        out_specs=pl.BlockSpec((cb, nbb, tb), lambda i: (i, 0, 0)),
        out_shape=jax.ShapeDtypeStruct((GROUP, nbb, tb), F32),
        compiler_params=_cparams("arbitrary"),
        name="hyena_long_conv",
    )(bias, g_rows, ut, ut, ut)
    return jnp.transpose(out.reshape(GROUP, nb, b, tb), (2, 1, 3, 0)).reshape(b, n, GROUP)


def _out_proj_kernel(x_ref, gt_ref, a_ref, h_ref, p_ref, c_ref, w_ref, o_ref):
    acc = _dot(a_ref[0].astype(BF16), w_ref[0:GROUP, :])
    acc = acc + _dot(h_ref[0].astype(BF16), w_ref[GROUP:2 * GROUP, :])
    acc = acc + _dot(p_ref[0].astype(BF16), w_ref[2 * GROUP:3 * GROUP, :])
    acc = acc + _dot(c_ref[0].astype(BF16), w_ref[3 * GROUP:4 * GROUP, :])
    o_ref[0] = x_ref[0] + gt_ref[0] * acc


def out_projection(x, gate, parts, w_out):
    b, l, d = x.shape
    tm = min(l, 512)
    part = lambda: pl.BlockSpec((1, tm, GROUP), lambda i, j: (i, j, 0))
    return pl.pallas_call(
        _out_proj_kernel,
        grid=(b, l // tm),
        in_specs=[pl.BlockSpec((1, tm, d), lambda i, j: (i, j, 0)),
                  pl.BlockSpec((1, 1, d), lambda i, j: (i, 0, 0)),
                  part(), part(), part(), part(),
                  pl.BlockSpec(w_out.shape, lambda i, j: (0, 0))],
        out_specs=pl.BlockSpec((1, tm, d), lambda i, j: (i, j, 0)),
        out_shape=jax.ShapeDtypeStruct((b, l, d), F32),
        compiler_params=_cparams("arbitrary", "arbitrary"),
        name="out_proj",
    )(x, gate, *parts, w_out.astype(BF16))


SLAB = 8


def _store_slab(ref, val, start=0):
    rows = val.shape[0]
    for s in range(SLAB):
        ref[pl.ds(start * SLAB + s, rows, stride=SLAB), :] = val[:, s * LANES:(s + 1) * LANES]


def _load_slab(ref, start, rows):
    return jnp.concatenate([ref[pl.ds(start * SLAB + s, rows, stride=SLAB), :] for s in range(SLAB)], axis=1)


def _router_kernel(x_ref, sh_ref, sc_ref, g_ref, wh_ref, wl_ref, rb_ref, h_ref, idx_ref, gate_ref):
    x = x_ref[0]
    ms = jnp.mean(x * x, axis=-1, keepdims=True)
    h = (x * lax.rsqrt(ms + NORM_EPS)) * g_ref[...] * (1.0 + sc_ref[0]) + sh_ref[0]
    _store_slab(h_ref, h)
    hh, hl = _split_bf16(h)
    logits = _dot(hh, wh_ref[...]) + (_dot(hh, wl_ref[...]) + _dot(hl, wh_ref[...])) + rb_ref[...]
    tm = x.shape[0]
    lane = lax.broadcasted_iota(jnp.int32, (tm, N_EXPERTS), 1).astype(F32)
    out_lane = lax.broadcasted_iota(jnp.int32, (tm, LANES), 1)
    vals, idxs = [], []
    cur = logits
    for _ in range(TOP_K):
        mx = jnp.max(cur, axis=-1, keepdims=True)
        ix = jnp.min(jnp.where(cur == mx, lane, float(N_EXPERTS)), axis=-1, keepdims=True)
        vals.append(mx)
        idxs.append(ix)
        cur = jnp.where(lane == ix, -jnp.inf, cur)
    es = [jnp.exp(v - vals[0]) for v in vals]
    inv = 1.0 / (es[0] + es[1] + es[2] + es[3])
    idx_out = jnp.zeros((tm, LANES), F32)
    gate_out = jnp.zeros((tm, LANES), F32)
    for k in range(TOP_K):
        idx_out = jnp.where(out_lane == k, idxs[k], idx_out)
        gate_out = jnp.where(out_lane == k, es[k] * inv, gate_out)
    idx_ref[0] = idx_out.astype(jnp.int32)
    gate_ref[0] = gate_out


def router(x, shift, scale, g, router_w, router_b):
    b, l, d = x.shape
    assert d == SLAB * LANES
    tm = min(l, 512)
    per_seq = l // tm
    wh, wl = _split_bf16(router_w)
    vec3 = lambda: pl.BlockSpec((1, 1, d), lambda i, j: (i, 0, 0))
    tile = lambda w: pl.BlockSpec((1, tm, w), lambda i, j: (i, j, 0))
    h, idx, gates = pl.pallas_call(
        _router_kernel,
        grid=(b, per_seq),
        in_specs=[tile(d), vec3(), vec3(),
                  pl.BlockSpec((1, d), lambda i, j: (0, 0)),
                  pl.BlockSpec(wh.shape, lambda i, j: (0, 0)),
                  pl.BlockSpec(wl.shape, lambda i, j: (0, 0)),
                  pl.BlockSpec((1, N_EXPERTS), lambda i, j: (0, 0))],
        out_specs=[pl.BlockSpec((tm * SLAB, LANES), lambda i, j: (i * per_seq + j, 0)), tile(LANES), tile(LANES)],
        out_shape=[jax.ShapeDtypeStruct((b * l * SLAB, LANES), F32),
                   jax.ShapeDtypeStruct((b, l, LANES), jnp.int32),
                   jax.ShapeDtypeStruct((b, l, LANES), F32)],
        compiler_params=_cparams("arbitrary", "arbitrary"),
        name="moe_router",
    )(x, shift, scale, g, wh, wl, router_b[None, :])
    return h, idx, gates


DISPATCH_ROWS = 2048


def _dispatch_kernel(lo_ref, hi_ref, pos_ref, src_ref, dst_ref, zero_ref, sem, *, rows):
    i = pl.program_id(0)

    def copy(tok, k):
        src = src_ref.at[pl.ds(pl.multiple_of(tok * SLAB, SLAB), SLAB)]
        dst = dst_ref.at[pl.ds(pl.multiple_of(pos_ref[k * (rows // TOP_K) + tok], SLAB), SLAB)]
        return pltpu.make_async_copy(src, dst, sem)

    def start(tok, carry):
        for k in range(TOP_K):
            copy(tok, k).start(priority=k % 2)
        return carry
    lax.fori_loop(0, rows // TOP_K, start, 0, unroll=2)

    def wait(tok, carry):
        for k in range(TOP_K):
            copy(tok, k).wait()
        return carry
    lax.fori_loop(0, rows // TOP_K, wait, 0, unroll=2)

    @pl.when(i == pl.num_programs(0) - 1)
    def _():
        zero_ref[...] = jnp.zeros_like(zero_ref)

        def fill(r):
            return pltpu.make_async_copy(zero_ref, dst_ref.at[pl.ds(pl.multiple_of(r * SLAB, SLAB), SLAB)], sem)

        def per_expert(e, carry):
            def fill_start(r, c):
                fill(r).start()
                return c

            def fill_wait(r, c):
                fill(r).wait()
                return c
            lax.fori_loop(lo_ref[e], hi_ref[e], fill_start, 0)
            lax.fori_loop(lo_ref[e], hi_ref[e], fill_wait, 0)
            return carry
        lax.fori_loop(0, N_EXPERTS, per_expert, 0)


def dispatch_rows(h, pos, pad_lo, pad_hi, n_slots):
    rows = DISPATCH_ROWS
    tokens = rows // TOP_K
    pos_rows = _tiled_rows(pos, tokens)
    grid_spec = pltpu.PrefetchScalarGridSpec(
        num_scalar_prefetch=2,
        grid=(pos_rows.shape[0] // rows,),
        in_specs=[pl.BlockSpec((rows,), lambda i, lo, hi: (i,), memory_space=pltpu.SMEM),
                  pl.BlockSpec((tokens * SLAB, LANES), lambda i, lo, hi: (i, 0))],
        out_specs=pl.BlockSpec(memory_space=pl.ANY),
        scratch_shapes=[pltpu.VMEM((SLAB, LANES), h.dtype), pltpu.SemaphoreType.DMA(())])
    return pl.pallas_call(
        functools.partial(_dispatch_kernel, rows=rows),
        grid_spec=grid_spec,
        out_shape=jax.ShapeDtypeStruct((n_slots * SLAB, LANES), h.dtype),
        compiler_params=_cparams("arbitrary"),
        name="moe_dispatch",
    )(pad_lo, pad_hi, pos_rows, h)


def _expert_kernel(te_ref, tv_ref, x_ref, w1_ref, b1_ref, w2_ref, b2_ref, o_ref, w1_bf, w2_bf, *, tm):
    i = pl.program_id(0)
    ff = w2_ref.shape[1]
    new_expert = jnp.logical_or(i == 0, te_ref[i] != te_ref[jnp.maximum(i - 1, 0)])

    @pl.when(jnp.logical_and(tv_ref[i] > 0, new_expert))
    def _():
        w1_bf[...] = w1_ref[0].astype(BF16)
        w2_bf[...] = w2_ref[0].astype(BF16)

    @pl.when(tv_ref[i] > 0)
    def _():
        x = _load_slab(x_ref, 0, tm).astype(BF16)
        hid = _dot(x, w1_bf[...]) + b1_ref[0]
        gate = jnp.minimum(hid[:, :ff], SWIGLU_LIMIT)
        up = jnp.clip(hid[:, ff:], -SWIGLU_LIMIT, SWIGLU_LIMIT)
        act = gate * _sigmoid(SWIGLU_ALPHA * gate) * (up + 1.0)
        _store_slab(o_ref, _dot(act.astype(BF16), w2_bf[...]) + b2_ref[0])

    @pl.when(tv_ref[i] == 0)
    def _():
        o_ref[...] = jnp.zeros_like(o_ref)


def expert_ffn(xs, tile_expert, tile_valid, w1, b1, w2, b2, tm):
    n_slots = xs.shape[0] // SLAB
    ne, d, ff2 = w1.shape
    ff = w2.shape[1]
    slab = lambda: pl.BlockSpec((tm * SLAB, LANES), lambda i, te, tv: (i, 0))
    grid_spec = pltpu.PrefetchScalarGridSpec(
        num_scalar_prefetch=2,
        grid=(n_slots // tm,),
        in_specs=[pl.BlockSpec((tm * SLAB, LANES), lambda i, te, tv: (i * tv[i], 0)),
                  pl.BlockSpec((1, d, ff2), lambda i, te, tv: (te[i], 0, 0)),
                  pl.BlockSpec((1, 1, ff2), lambda i, te, tv: (te[i], 0, 0)),
                  pl.BlockSpec((1, ff, d), lambda i, te, tv: (te[i], 0, 0)),
                  pl.BlockSpec((1, 1, d), lambda i, te, tv: (te[i], 0, 0))],
        out_specs=slab(),
        scratch_shapes=[pltpu.VMEM((d, ff2), BF16), pltpu.VMEM((ff, d), BF16)])
    return pl.pallas_call(
        functools.partial(_expert_kernel, tm=tm),
        grid_spec=grid_spec,
        out_shape=jax.ShapeDtypeStruct((n_slots * SLAB, LANES), F32),
        compiler_params=_cparams("arbitrary"),
        name="moe_experts",
    )(tile_expert, tile_valid, xs, w1, b1.reshape(ne, 1, ff2), w2, b2.reshape(ne, 1, d))


def _combine_kernel(pos_ref, nxt_ref, x_ref, gt_ref, gates_ref, fin_ref, ys_ref, o_ref, buf, sems, *, tc, final):
    i = pl.program_id(0)
    n = tc * TOP_K
    slot = i % 2

    def copy(idx_ref, r, s):
        src = ys_ref.at[pl.ds(pl.multiple_of(idx_ref[r], SLAB), SLAB)]
        return pltpu.make_async_copy(src, buf.at[s, pl.ds(pl.multiple_of(r * SLAB, SLAB), SLAB)], sems.at[s])

    def issue(idx_ref, s):
        def body(r2, carry):
            for j in range(2):
                copy(idx_ref, 2 * r2 + j, s).start(priority=j)
            return carry
        lax.fori_loop(0, n // 2, body, 0, unroll=4)

    @pl.when(i == 0)
    def _():
        issue(pos_ref, 0)

    @pl.when(i + 1 < pl.num_programs(0))
    def _():
        issue(nxt_ref, 1 - slot)

    def drain(r, carry):
        copy(pos_ref, r, slot).wait()
        return carry
    lax.fori_loop(0, n, drain, 0, unroll=8)
    gates = gates_ref[...]
    y = jnp.zeros((tc, SLAB * LANES), F32)
    for k in range(TOP_K):
        y = y + gates[:, k:k + 1] * _load_slab(buf.at[slot], k * tc, tc)
    out = x_ref[...] + gt_ref[0] * y
    if final:
        ms = jnp.mean(out * out, axis=-1, keepdims=True)
        out = out * lax.rsqrt(ms + NORM_EPS) * fin_ref[...]
    o_ref[...] = out


def combine_rows(x, gate_vec, gates, pos, ys, final_g, final):
    b, l, d = x.shape
    t = b * l
    tc = min(l, 256)
    steps = t // tc
    per_seq = l // tc
    n = tc * TOP_K
    pos_t = _tiled_rows(pos, tc)
    gates_p = gates
    out = pl.pallas_call(
        functools.partial(_combine_kernel, tc=tc, final=final),
        grid=(steps,),
        in_specs=[pl.BlockSpec((n,), lambda i: (i,), memory_space=pltpu.SMEM),
                  pl.BlockSpec((n,), lambda i: (jnp.minimum(i + 1, steps - 1),), memory_space=pltpu.SMEM),
                  pl.BlockSpec((tc, d), lambda i: (i, 0)),
                  pl.BlockSpec((1, 1, d), lambda i: (i // per_seq, 0, 0)),
                  pl.BlockSpec((tc, LANES), lambda i: (i, 0)),
                  pl.BlockSpec((1, d), lambda i: (0, 0)),
                  pl.BlockSpec(memory_space=pl.ANY)],
        out_specs=pl.BlockSpec((tc, d), lambda i: (i, 0)),
        out_shape=jax.ShapeDtypeStruct((t, d), F32),
        scratch_shapes=[pltpu.VMEM((2, n * SLAB, LANES), F32), pltpu.SemaphoreType.DMA((2,))],
        compiler_params=_cparams("arbitrary"),
        name="moe_combine",
    )(pos_t, pos_t, x.reshape(t, d), gate_vec, gates_p, final_g[None, :], ys)
    return out.reshape(b, l, d)


def routing_plan(e, tm):
    k, t = e.shape
    n = k * t
    ef = e.reshape(n)
    onehot = (ef[:, None] == jnp.arange(N_EXPERTS, dtype=ef.dtype)[None, :]).astype(jnp.int32)
    csum = jnp.cumsum(onehot, axis=0)
    counts = csum[-1]
    rank = jnp.take_along_axis(csum, ef[:, None], axis=1)[:, 0] - 1
    padded = ((counts + tm - 1) // tm) * tm
    ends = jnp.cumsum(padded)
    starts = ends - padded
    pos = starts[ef] + rank
    n_slots = -(-(n + N_EXPERTS * (tm - 1)) // tm) * tm
    tile_start = jnp.arange(n_slots // tm, dtype=jnp.int32) * tm
    tile_expert = jnp.minimum(jnp.sum((tile_start[:, None] >= ends[None, :]).astype(jnp.int32), axis=1), N_EXPERTS - 1)
    tile_valid = (tile_start < ends[-1]).astype(jnp.int32)
    return (pos.reshape(k, t).astype(jnp.int32), (starts + counts).astype(jnp.int32), ends.astype(jnp.int32),
            n_slots, tile_expert.astype(jnp.int32), tile_valid)


def _tiled_rows(pos, tile):
    k, t = pos.shape
    return jnp.transpose(pos.reshape(k, t // tile, tile), (1, 0, 2)).reshape(k * t) * SLAB


MOE_TILE = 512


def moe_block(x, shift, scale, gate_vec, g, router_w, router_b, w1, b1, w2, b2, final_g, final):
    b, l, d = x.shape
    t = b * l
    h, idx, gates = router(x, shift, scale, g, router_w, router_b)
    e = jnp.stack([idx[..., k].reshape(t) for k in range(TOP_K)])
    pos, pad_lo, pad_hi, n_slots, tile_expert, tile_valid = routing_plan(e, MOE_TILE)
    xs = dispatch_rows(h, pos, pad_lo, pad_hi, n_slots)
    ys = expert_ffn(xs, tile_expert, tile_valid, w1, b1, w2, b2, MOE_TILE)
    return combine_rows(x, gate_vec, gates.reshape(t, LANES), pos, ys, final_g, final)


def _diff_lambda(lp, lam_init):
    lp = lp.astype(F32)
    return jnp.exp(jnp.sum(lp[0] * lp[1])) - jnp.exp(jnp.sum(lp[2] * lp[3])) + lam_init


def kernel(x, c, ctx, c_ctx, norm1, norm2, w_mod, b_mod, w_in, w_out, attn_lambda, attn_subln,
           hyena_short_w, hyena_short_b, hyena_w1, hyena_b1, hyena_w2, hyena_b2, hyena_w3, hyena_freq,
           hyena_bias, pool_w, pool_scale, conv_dw_w, conv_dw_b, conv_ln_g, conv_ln_b, conv_pw_w,
           conv_pw_b, router_w, router_b, moe_w1, moe_b1, moe_w2, moe_b2, final_norm):
    bsz, n_lat, d = x.shape
    n_ctx = ctx.shape[1]
    depth = w_mod.shape[0]
    g = GROUP
    off_hy, off_pool, off_conv = 3 * g, 6 * g, 7 * g

    rows = -(-(bsz + 1) // SUBLANES) * SUBLANES
    cvec = jnp.zeros((rows, d), F32).at[:bsz].set(c).at[bsz].set(c_ctx)
    mods = adaln_all(cvec, w_mod, b_mod)

    cos_t, sin_t, partner = rope_tables(n_lat)
    ones_t = jnp.ones((n_ctx, g), F32)
    zeros_t = jnp.zeros((n_ctx, g), F32)
    qk_scale = HEAD_QK ** -0.5 * math.log2(math.e)

    xc = ctx
    for i in range(depth):
        last = i == depth - 1
        m = mods[i]
        lat = [m[:bsz, None, k * d:(k + 1) * d] for k in range(6)]
        cm = [jnp.broadcast_to(m[bsz, k * d:(k + 1) * d][None, None, :], (bsz, 1, d)) for k in range(6)]
        lam_init = 0.8 - 0.6 * math.exp(-0.3 * i)
        lam = _diff_lambda(attn_lambda[i], lam_init)
        wi = w_in[i]
        g1 = norm1[i][None, :]
        g2 = norm2[i][None, :]
        w1b = moe_w1[i]
        w2b = moe_w2[i]

        w_lat = jnp.concatenate([wi, wi[:, partner], wi[:, g + partner]], axis=1).astype(BF16)
        nin = wi.shape[1]
        segs = [(0, g, nin, qk_scale), (g, g, nin + g, 1.0), (2 * g, g, None, 1.0),
                (off_hy, 3 * g, None, 1.0), (off_pool, g, None, 1.0), (off_conv, 2 * g, None, 1.0)]
        q, k, v, p_hy, p_pool, p_conv = modulated_projection(
            x, lat[0], lat[1], g1, w_lat, segs, [BF16, BF16, BF16, F32, F32, F32], cos_t, sin_t)

        if last:
            w_ctx = wi[:, g:3 * g].astype(BF16)
            kc, vc = modulated_projection(xc, cm[0], cm[1], g1, w_ctx,
                                          [(0, g, None, 1.0), (g, g, None, 1.0)], [BF16, BF16], ones_t, zeros_t)
        else:
            csegs = [(0, g, None, qk_scale), (g, g, None, 1.0), (2 * g, g, None, 1.0),
                     (off_hy, 3 * g, None, 1.0), (off_pool, g, None, 1.0), (off_conv, 2 * g, None, 1.0)]
            qc, kc, vc, pc_hy, pc_pool, pc_conv = modulated_projection(
                xc, cm[0], cm[1], g1, wi.astype(BF16), csegs, [BF16, BF16, BF16, F32, F32, F32], ones_t, zeros_t)

        kt_all = jnp.transpose(jnp.concatenate([kc, k], axis=1), (0, 2, 1))
        v_all = jnp.concatenate([vc, v], axis=1)
        o_attn = diff_attention(q, kt_all, v_all, lam, attn_subln[i], 1.0 - lam_init)

        filt_params = (hyena_w1[i], hyena_b1[i], hyena_w2[i], hyena_b2[i], hyena_w3[i], hyena_freq[i])

        def local_groups(p_hy_, p_pool_, p_conv_, n_tokens):
            g_rows = filter_rows(hyena_filters(n_tokens, *filt_params), n_tokens)
            u = hyena_short_conv(p_hy_, hyena_short_w[i], hyena_short_b[i])
            o_hy = hyena_long_conv(u, g_rows, hyena_bias[i])
            o_pool = pool_mixer(p_pool_, pool_w[i], pool_scale[i])
            o_conv = conformer_mixer(p_conv_, conv_dw_w[i], conv_dw_b[i], conv_ln_g[i], conv_ln_b[i],
                                     conv_pw_w[i], conv_pw_b[i])
            return o_hy, o_pool, o_conv

        o_hy, o_pool, o_conv = local_groups(p_hy, p_pool, p_conv, n_lat)
        x = out_projection(x, lat[2], (o_attn, o_hy, o_pool, o_conv), w_out[i])
        if not last:
            oc_attn = diff_attention(qc, jnp.transpose(kc, (0, 2, 1)), vc, lam, attn_subln[i], 1.0 - lam_init)
            oc_hy, oc_pool, oc_conv = local_groups(pc_hy, pc_pool, pc_conv, n_ctx)
            xc = out_projection(xc, cm[2], (oc_attn, oc_hy, oc_pool, oc_conv), w_out[i])

        x = moe_block(x, lat[3], lat[4], lat[5], g2, router_w[i], router_b[i], w1b, moe_b1[i], w2b, moe_b2[i],
                      final_norm, last)
        if not last:
            xc = moe_block(xc, cm[3], cm[4], cm[5], g2, router_w[i], router_b[i], w1b, moe_b1[i], w2b, moe_b2[i],
                           final_norm, False)
    return x
```

```python
import functools
import math

import jax
import jax.numpy as jnp
from jax import lax
from jax.experimental import pallas as pl
from jax.experimental.pallas import tpu as pltpu

F32 = jnp.float32
BF16 = jnp.bfloat16

GRID_W = 64
N_HEADS = 4
HEAD_V = 64
HEAD_QK = 32
GROUP = 256
ROPE_BASE = 10000.0
SUBLN_EPS = 1e-5
NORM_EPS = 1e-6
LN_EPS = 1e-5
POOL_WINDOWS = (2, 4, 8, 16)
CONV_KERNEL = 31
HYENA_SHORT = 3
HYENA_EMB = 33
HYENA_TARGET = 1e-2
HYENA_FAST_PCT = 0.3
HYENA_SLOW_PCT = 1.5
N_EXPERTS = 32
TOP_K = 4
SWIGLU_ALPHA = 1.702
SWIGLU_LIMIT = 7.0

LANES = 128
SUBLANES = 8
VMEM_LIMIT = 56 * 1024 * 1024
HALO = 16


def _cparams(*sem):
    return pltpu.CompilerParams(dimension_semantics=sem, vmem_limit_bytes=VMEM_LIMIT)


def _split_bf16(a):
    hi = a.astype(BF16)
    lo = (a - hi.astype(F32)).astype(BF16)
    return hi, lo


def _dot(a, b):
    return jnp.dot(a, b, preferred_element_type=F32)


def _dot3(a, b):
    ah, al = _split_bf16(a)
    bh, bl = _split_bf16(b)
    return _dot(ah, bh) + (_dot(ah, bl) + _dot(al, bh))


def _sigmoid(x):
    return 1.0 / (1.0 + jnp.exp(-x))


def _adaln_kernel(c_ref, w_ref, b_ref, o_ref):
    cv = c_ref[...]
    s = cv * _sigmoid(cv)
    o_ref[0] = _dot3(s, w_ref[0]) + b_ref[0]


def adaln_all(cvec, w_mod, b_mod):
    depth, d, n = w_mod.shape
    r = cvec.shape[0]
    tn = 1536
    return pl.pallas_call(
        _adaln_kernel,
        grid=(depth, n // tn),
        in_specs=[pl.BlockSpec((r, d), lambda i, j: (0, 0)),
                  pl.BlockSpec((1, d, tn), lambda i, j: (i, 0, j)),
                  pl.BlockSpec((1, 1, tn), lambda i, j: (i, 0, j))],
        out_specs=pl.BlockSpec((1, r, tn), lambda i, j: (i, 0, j)),
        out_shape=jax.ShapeDtypeStruct((depth, r, n), F32),
        compiler_params=_cparams("arbitrary", "arbitrary"),
        name="adaln",
    )(cvec, w_mod, b_mod.reshape(depth, 1, n))


def _filter_kernel(feat_ref, w1_ref, b1_ref, w2_ref, b2_ref, w3_ref, fr_ref, dec_ref, o_ref):
    fr = fr_ref[...]
    z = jnp.sin(fr * (_dot3(feat_ref[...], w1_ref[...]) + b1_ref[...]))
    z = jnp.sin(fr * (_dot3(z, w2_ref[...]) + b2_ref[...]))
    o_ref[...] = _dot3(z, w3_ref[...]) * dec_ref[...]


def hyena_filters(n_tokens, w1, b1, w2, b2, w3, freq):
    bands = (HYENA_EMB - 1) // 2
    t = jnp.linspace(0.0, 1.0, n_tokens, dtype=F32)[:, None]
    w = 2.0 * math.pi * jnp.arange(n_tokens, dtype=F32)[:, None] / n_tokens
    f = jnp.linspace(1e-4, bands - 1, bands, dtype=F32)[None, :]
    feat = jnp.concatenate([t, jnp.cos(f * w), -jnp.sin(f * w)], axis=-1)
    kpad = 40
    feat = jnp.pad(feat, ((0, 0), (0, kpad - HYENA_EMB)))
    w1p = jnp.pad(w1, ((0, kpad - HYENA_EMB), (0, 0)))
    min_decay = math.log(HYENA_TARGET) / HYENA_SLOW_PCT
    max_decay = math.log(HYENA_TARGET) / HYENA_FAST_PCT
    deltas = jnp.abs(jnp.linspace(min_decay, max_decay, GROUP, dtype=F32))
    decay = jnp.exp(-t * deltas)
    nf = w3.shape[1]
    decay = jnp.tile(decay, (1, nf // GROUP))
    ffn = w2.shape[0]
    tl = min(n_tokens, 512)
    full = lambda shape: pl.BlockSpec(shape, lambda i: (0, 0))
    return pl.pallas_call(
        _filter_kernel,
        grid=(n_tokens // tl,),
        in_specs=[pl.BlockSpec((tl, kpad), lambda i: (i, 0)),
                  full((kpad, ffn)), full((1, ffn)), full((ffn, ffn)), full((1, ffn)),
                  full((ffn, nf)), full((1, ffn)),
                  pl.BlockSpec((tl, nf), lambda i: (i, 0))],
        out_specs=pl.BlockSpec((tl, nf), lambda i: (i, 0)),
        out_shape=jax.ShapeDtypeStruct((n_tokens, nf), F32),
        compiler_params=_cparams("arbitrary"),
        name="hyena_filter",
    )(feat, w1p, b1[None, :], w2, b2[None, :], w3, freq[None, :], decay)


def filter_rows(filt, n_tokens):
    order = filt.shape[1] // (2 * GROUP)
    f4 = filt.reshape(n_tokens, order, 2, GROUP)
    hf = f4[:, :, 0]
    hb = f4[:, :, 1]
    rows = jnp.concatenate([jnp.zeros_like(hf[:1]), hb[:0:-1], hf], axis=0)
    return jnp.transpose(rows, (1, 2, 0))[:, :, None, :]


def _proj_kernel(x_ref, sh_ref, sc_ref, g_ref, w_ref, cos_ref, sin_ref, *out_refs, segs):
    x = x_ref[0]
    ms = jnp.mean(x * x, axis=-1, keepdims=True)
    h = (x * lax.rsqrt(ms + NORM_EPS)) * g_ref[...] * (1.0 + sc_ref[0]) + sh_ref[0]
    hb = h.astype(BF16)
    for o_ref, (start, width, swap_start, scale) in zip(out_refs, segs):
        p = _dot(hb, w_ref[:, start:start + width])
        if swap_start is not None:
            ps = _dot(hb, w_ref[:, swap_start:swap_start + width])
            p = p * cos_ref[...] + ps * sin_ref[...]
        if scale != 1.0:
            p = p * scale
        o_ref[0] = p.astype(o_ref.dtype)


def modulated_projection(x, shift, scale, g, w, segs, out_dtypes, cos_t, sin_t):
    b, l, d = x.shape
    tm = min(l, 512)
    nw = w.shape[1]
    rw = cos_t.shape[1]
    outs = [jax.ShapeDtypeStruct((b, l, s[1]), dt) for s, dt in zip(segs, out_dtypes)]
    return pl.pallas_call(
        functools.partial(_proj_kernel, segs=tuple(segs)),
        grid=(b, l // tm),
        in_specs=[pl.BlockSpec((1, tm, d), lambda i, j: (i, j, 0)),
                  pl.BlockSpec((1, 1, d), lambda i, j: (i, 0, 0)),
                  pl.BlockSpec((1, 1, d), lambda i, j: (i, 0, 0)),
                  pl.BlockSpec((1, d), lambda i, j: (0, 0)),
                  pl.BlockSpec((d, nw), lambda i, j: (0, 0)),
                  pl.BlockSpec((tm, rw), lambda i, j: (j, 0)),
                  pl.BlockSpec((tm, rw), lambda i, j: (j, 0))],
        out_specs=[pl.BlockSpec((1, tm, s[1]), lambda i, j: (i, j, 0)) for s in segs],
        out_shape=outs,
        compiler_params=_cparams("arbitrary", "arbitrary"),
        name="mod_proj",
    )(x, shift, scale, g, w, cos_t, sin_t)


def rope_tables(n_tokens):
    rows = n_tokens // GRID_W
    row = jnp.repeat(jnp.arange(rows, dtype=F32), GRID_W)
    col = jnp.tile(jnp.arange(GRID_W, dtype=F32), rows)
    n_freq = HEAD_QK // 4
    inv = ROPE_BASE ** (-jnp.arange(n_freq, dtype=F32) / n_freq)
    lane = jnp.arange(GROUP)
    d = lane % HEAD_QK
    pos = jnp.where((d < 2 * n_freq)[None, :], row[:, None], col[:, None])
    ang = pos * inv[d % n_freq][None, :]
    first = (d % (2 * n_freq)) < n_freq
    cos_t = jnp.cos(ang)
    sin_t = jnp.where(first[None, :], -jnp.sin(ang), jnp.sin(ang))
    partner = jnp.where(first, lane + n_freq, lane - n_freq)
    return cos_t, sin_t, partner


ATTN_ROWS = 16


def _attn_kernel(lam_ref, q_ref, kt_ref, v_ref, g_ref, o_ref, s_even, s_odd, a_even, a_odd, *, out_scale):
    q = q_ref[0]
    lam = lam_ref[0]
    tq = q.shape[0]
    lane = lax.broadcasted_iota(jnp.int32, (1, GROUP), 1)
    map_of_lane = lane // HEAD_QK
    head_of_lane = lane // HEAD_V
    s_bufs = (s_even, s_odd)
    a_bufs = (a_even, a_odd)

    z = 0

    def scores(h):
        for m in range(2):
            keep = jnp.where(map_of_lane == 2 * h + m, 1.0, 0.0).astype(BF16)
            s_bufs[h % 2][z + m] = _dot(q * keep, kt_ref[0])

    def softmax(h):
        s_ref, a_ref = s_bufs[h % 2], a_bufs[h % 2]
        inv = []
        for r in range(0, tq, ATTN_ROWS):
            s1 = s_ref[z + 1, r:r + ATTN_ROWS, :]
            e1 = jnp.exp2(s1 - jnp.max(s1, axis=-1, keepdims=True))
            sum1 = jnp.sum(e1, axis=-1, keepdims=True)
            s0 = s_ref[z, r:r + ATTN_ROWS, :]
            e0 = jnp.exp2(s0 - jnp.max(s0, axis=-1, keepdims=True))
            sum0 = jnp.sum(e0, axis=-1, keepdims=True)
            a_ref[z, r:r + ATTN_ROWS, :] = (e0 - e1 * (lam * sum0 / sum1)).astype(BF16)
            inv.append(1.0 / sum0)
        return jnp.concatenate(inv, axis=0)

    acc = jnp.zeros((tq, GROUP), F32)
    scores(0)
    for h in range(N_HEADS):
        if h + 1 < N_HEADS:
            scores(h + 1)
        inv0 = softmax(h)
        o = _dot(a_bufs[h % 2][z], v_ref[0]) * inv0
        acc = jnp.where(head_of_lane == h, o, acc)
    sq = acc * acc
    ms = jnp.zeros((tq, GROUP), F32)
    for h in range(N_HEADS):
        hsel = head_of_lane == h
        mh = jnp.sum(jnp.where(hsel, sq, 0.0), axis=-1, keepdims=True) * (1.0 / HEAD_V)
        ms = jnp.where(hsel, mh, ms)
    y = acc * lax.rsqrt(ms + SUBLN_EPS) * g_ref[...] * out_scale
    o_ref[0] = y.astype(o_ref.dtype)


def diff_attention(q, kt, v, lam, subln, out_scale):
    b, nq, _ = q.shape
    nk = v.shape[1]
    tq = min(nq, 256)
    g = jnp.tile(subln, N_HEADS)[None, :]
    return pl.pallas_call(
        functools.partial(_attn_kernel, out_scale=out_scale),
        grid=(b, nq // tq),
        in_specs=[pl.BlockSpec(memory_space=pltpu.SMEM),
                  pl.BlockSpec((1, tq, GROUP), lambda i, j: (i, j, 0)),
                  pl.BlockSpec((1, GROUP, nk), lambda i, j: (i, 0, 0)),
                  pl.BlockSpec((1, nk, GROUP), lambda i, j: (i, 0, 0)),
                  pl.BlockSpec((1, GROUP), lambda i, j: (0, 0))],
        out_specs=pl.BlockSpec((1, tq, GROUP), lambda i, j: (i, j, 0)),
        out_shape=jax.ShapeDtypeStruct((b, nq, GROUP), BF16),
        scratch_shapes=[pltpu.VMEM((2, tq, nk), F32), pltpu.VMEM((2, tq, nk), F32),
                        pltpu.VMEM((1, tq, nk), BF16), pltpu.VMEM((1, tq, nk), BF16)],
        compiler_params=_cparams("arbitrary", "arbitrary"),
        name="diff_attn",
    )(lam.reshape(1), q, kt, v, g)


def _fill_padded(pad_ref, n, write_rows):
    zeros = jnp.zeros((HALO, pad_ref.shape[1]), F32)
    pad_ref[0:HALO, :] = zeros
    pad_ref[HALO + n:HALO + n + HALO, :] = zeros
    write_rows()


def _shifted_rows(pad_ref, r0, ch, offsets):
    rows = ch + 2 * HALO
    blk = pad_ref[pl.ds(r0, rows), :]
    rolled = {0: blk}
    out = {}
    for k in offsets:
        o = HALO + k
        r = o % SUBLANES
        if r not in rolled:
            rolled[r] = pltpu.roll(blk, rows - r, 0)
        out[k] = rolled[r][o - r:o - r + ch]
    return out


def _pool_kernel(u_ref, w_ref, sc_ref, o_ref, pad_ref, *, n, ch):
    def copy(i, c):
        r0 = pl.multiple_of(i * ch, ch)
        pad_ref[pl.ds(HALO + r0, ch), :] = u_ref[0, pl.ds(r0, ch), :]
        return c
    _fill_padded(pad_ref, n, lambda: lax.fori_loop(0, n // ch, copy, 0))
    lane = lax.broadcasted_iota(jnp.int32, (1, GROUP), 1)
    pg = GROUP // len(POOL_WINDOWS)
    reach = POOL_WINDOWS[-1] // 2

    def body(i, c):
        r0 = pl.multiple_of(i * ch, ch)
        taps = _shifted_rows(pad_ref, r0, ch, range(-reach, reach))
        ld = lambda k: taps[k]
        t = r0 + lax.broadcasted_iota(jnp.int32, (ch, 1), 0)
        u = ld(0)
        sums = []
        s = u + ld(-1)
        sums.append(s)
        half = 1
        for _ in POOL_WINDOWS[1:]:
            for k in range(half, 2 * half):
                s = s + ld(k) + ld(-k - 1)
            half *= 2
            sums.append(s)
        pooled = None
        for gi, win in reversed(list(enumerate(POOL_WINDOWS))):
            hi = jnp.minimum(t + (win - win // 2), n)
            lo = jnp.maximum(t - win // 2, 0)
            mean = sums[gi] / (hi - lo).astype(F32)
            pooled = mean if pooled is None else jnp.where(lane < (gi + 1) * pg, mean, pooled)
        pooled = pooled - u
        y = _dot(pooled.astype(BF16), w_ref[...]) * sc_ref[...]
        o_ref[0, pl.ds(r0, ch), :] = y.astype(o_ref.dtype)
        return c
    lax.fori_loop(0, n // ch, body, 0)


def pool_mixer(u, w_pool, scale):
    b, n, c = u.shape
    ng, pg, _ = w_pool.shape
    wbd = jnp.zeros((c, c), F32)
    for gi in range(ng):
        wbd = wbd.at[gi * pg:(gi + 1) * pg, gi * pg:(gi + 1) * pg].set(w_pool[gi])
    ch = min(n, 256)
    return pl.pallas_call(
        functools.partial(_pool_kernel, n=n, ch=ch),
        grid=(b,),
        in_specs=[pl.BlockSpec((1, n, c), lambda i: (i, 0, 0)),
                  pl.BlockSpec((c, c), lambda i: (0, 0)),
                  pl.BlockSpec((1, c), lambda i: (0, 0))],
        out_specs=pl.BlockSpec((1, n, c), lambda i: (i, 0, 0)),
        out_shape=jax.ShapeDtypeStruct((b, n, c), BF16),
        scratch_shapes=[pltpu.VMEM((n + 2 * HALO, c), F32)],
        compiler_params=_cparams("arbitrary"),
        name="pool_mixer",
    )(u, wbd.astype(BF16), scale[None, :])


def _conformer_kernel(p_ref, dw_ref, dwb_ref, lg_ref, lb_ref, pw_ref, pwb_ref, o_ref, pad_ref, *, n, ch):
    c = GROUP

    def glu(i, carry):
        r0 = pl.multiple_of(i * ch, ch)
        a = p_ref[0, pl.ds(r0, ch), 0:c]
        g = p_ref[0, pl.ds(r0, ch), c:2 * c]
        pad_ref[pl.ds(HALO + r0, ch), :] = a * _sigmoid(g)
        return carry
    _fill_padded(pad_ref, n, lambda: lax.fori_loop(0, n // ch, glu, 0))
    half = (CONV_KERNEL - 1) // 2

    def body(i, carry):
        r0 = pl.multiple_of(i * ch, ch)
        acc = jnp.zeros((ch, c), F32) + dwb_ref[...]
        taps = _shifted_rows(pad_ref, r0, ch, range(-half, half + 1))
        for k in range(CONV_KERNEL):
            acc = acc + taps[k - half] * dw_ref[k:k + 1, :]
        mu = jnp.mean(acc, axis=-1, keepdims=True)
        xc = acc - mu
        var = jnp.mean(xc * xc, axis=-1, keepdims=True)
        y = xc * lax.rsqrt(var + LN_EPS) * lg_ref[...] + lb_ref[...]
        y = y * _sigmoid(y)
        out = _dot(y.astype(BF16), pw_ref[...]) + pwb_ref[...]
        o_ref[0, pl.ds(r0, ch), :] = out.astype(o_ref.dtype)
        return carry
    lax.fori_loop(0, n // ch, body, 0)


def conformer_mixer(p, dw_w, dw_b, ln_g, ln_b, pw_w, pw_b):
    b, n, c2 = p.shape
    c = c2 // 2
    ch = min(n, 128)
    kp = 32
    dw = jnp.pad(dw_w, ((0, kp - CONV_KERNEL), (0, 0)))
    vec = lambda: pl.BlockSpec((1, c), lambda i: (0, 0))
    return pl.pallas_call(
        functools.partial(_conformer_kernel, n=n, ch=ch),
        grid=(b,),
        in_specs=[pl.BlockSpec((1, n, c2), lambda i: (i, 0, 0)),
                  pl.BlockSpec((kp, c), lambda i: (0, 0)),
                  vec(), vec(), vec(),
                  pl.BlockSpec((c, c), lambda i: (0, 0)),
                  vec()],
        out_specs=pl.BlockSpec((1, n, c), lambda i: (i, 0, 0)),
        out_shape=jax.ShapeDtypeStruct((b, n, c), BF16),
        scratch_shapes=[pltpu.VMEM((n + 2 * HALO, c), F32)],
        compiler_params=_cparams("arbitrary"),
        name="conformer_mixer",
    )(p, dw, dw_b[None, :], ln_g[None, :], ln_b[None, :], pw_w.astype(BF16), pw_b[None, :])


def _short_conv_kernel(p_ref, w_ref, b_ref, o_ref, pad_ref, *, n, ch):
    def copy(i, carry):
        r0 = pl.multiple_of(i * ch, ch)
        pad_ref[pl.ds(HALO + r0, ch), :] = p_ref[0, pl.ds(r0, ch), :]
        return carry
    _fill_padded(pad_ref, n, lambda: lax.fori_loop(0, n // ch, copy, 0))
    half = (HYENA_SHORT - 1) // 2

    def body(i, carry):
        r0 = pl.multiple_of(i * ch, ch)
        acc = jnp.zeros((ch, pad_ref.shape[1]), F32) + b_ref[...]
        taps = _shifted_rows(pad_ref, r0, ch, range(-half, half + 1))
        for k in range(HYENA_SHORT):
            acc = acc + taps[k - half] * w_ref[k:k + 1, :]
        o_ref[0, pl.ds(r0, ch), :] = acc.astype(o_ref.dtype)
        return carry
    lax.fori_loop(0, n // ch, body, 0)


def hyena_short_conv(p, w, bias):
    b, n, c = p.shape
    ch = min(n, 128)
    wp = jnp.pad(w, ((0, SUBLANES - HYENA_SHORT), (0, 0)))
    return pl.pallas_call(
        functools.partial(_short_conv_kernel, n=n, ch=ch),
        grid=(b,),
        in_specs=[pl.BlockSpec((1, n, c), lambda i: (i, 0, 0)),
                  pl.BlockSpec((SUBLANES, c), lambda i: (0, 0)),
                  pl.BlockSpec((1, c), lambda i: (0, 0))],
        out_specs=pl.BlockSpec((1, n, c), lambda i: (i, 0, 0)),
        out_shape=jax.ShapeDtypeStruct((b, n, c), BF16),
        scratch_shapes=[pltpu.VMEM((n + 2 * HALO, c), F32)],
        compiler_params=_cparams("arbitrary"),
        name="hyena_short_conv",
    )(p, wp, bias[None, :])


def _toeplitz_conv(g_ref, u, nb, tb, bsz):
    ys = [jnp.zeros((bsz, tb), F32) for _ in range(nb)]
    for d in range(-(nb - 1), nb):
        start = tb * (nb - 1 + d)
        win = jnp.broadcast_to(g_ref[:, start:start + 2 * tb], (tb, 2 * tb))
        rolled = pltpu.roll(win, 0, 1, stride=1, stride_axis=0)
        blk = rolled[:, tb:2 * tb].astype(BF16)
        j_lo, j_hi = max(0, -d), min(nb, nb - d)
        o = _dot(u[j_lo * bsz:j_hi * bsz], blk)
        for j in range(j_lo, j_hi):
            ys[j + d] = ys[j + d] + o[(j - j_lo) * bsz:(j - j_lo + 1) * bsz]
    return ys


def _hyena_kernel(bias_ref, g_ref, v_ref, x1_ref, x2_ref, o_ref, *, nb, tb, bsz, cb):
    base = pl.program_id(0) * cb

    def body(ci, carry):
        v = v_ref[ci]
        y1 = jnp.concatenate(_toeplitz_conv(g_ref.at[0, ci], v, nb, tb, bsz), axis=0)
        z = x1_ref[ci].astype(F32) * (y1 + bias_ref[0, base + ci] * v.astype(F32))
        y2 = jnp.concatenate(_toeplitz_conv(g_ref.at[1, ci], z.astype(BF16), nb, tb, bsz), axis=0)
        out = x2_ref[ci].astype(F32) * (y2 + bias_ref[1, base + ci] * z)
        o_ref[ci] = out.astype(o_ref.dtype)
        return carry
    lax.fori_loop(0, cb, body, 0, unroll=2)


def hyena_long_conv(u, g_rows, bias):
    b, n, _ = u.shape
    tb = min(n, 256)
    nb = n // tb
    nbb = nb * b
    ut = jnp.transpose(u.reshape(b, nb, tb, 3, GROUP), (3, 4, 1, 0, 2)).reshape(3, GROUP, nbb, tb)
    cb = 8
    blk = lambda part: pl.BlockSpec((None, cb, nbb, tb), lambda i, part=part: (part, i, 0, 0))
    out = pl.pallas_call(
        functools.partial(_hyena_kernel, nb=nb, tb=tb, bsz=b, cb=cb),
        grid=(GROUP // cb,),
        in_specs=[pl.BlockSpec(memory_space=pltpu.SMEM),
                  pl.BlockSpec((2, cb, 1, 2 * n), lambda i: (0, i, 0, 0)),
                  blk(0), blk(1), blk(2)],
        out_specs=pl.BlockSpec((cb, nbb, tb), lambda i: (i, 0, 0)),
        out_shape=jax.ShapeDtypeStruct((GROUP, nbb, tb), BF16),
        compiler_params=_cparams("arbitrary"),
        name="hyena_long_conv",
    )(bias, g_rows, ut, ut, ut)
    return jnp.transpose(out.reshape(GROUP, nb, b, tb), (2, 1, 3, 0)).reshape(b, n, GROUP)


def _out_proj_kernel(x_ref, gt_ref, a_ref, h_ref, p_ref, c_ref, w_ref, o_ref):
    acc = _dot(a_ref[0].astype(BF16), w_ref[0:GROUP, :])
    acc = acc + _dot(h_ref[0].astype(BF16), w_ref[GROUP:2 * GROUP, :])
    acc = acc + _dot(p_ref[0].astype(BF16), w_ref[2 * GROUP:3 * GROUP, :])
    acc = acc + _dot(c_ref[0].astype(BF16), w_ref[3 * GROUP:4 * GROUP, :])
    o_ref[0] = x_ref[0] + gt_ref[0] * acc


def out_projection(x, gate, parts, w_out):
    b, l, d = x.shape
    tm = min(l, 512)
    part = lambda: pl.BlockSpec((1, tm, GROUP), lambda i, j: (i, j, 0))
    return pl.pallas_call(
        _out_proj_kernel,
        grid=(b, l // tm),
        in_specs=[pl.BlockSpec((1, tm, d), lambda i, j: (i, j, 0)),
                  pl.BlockSpec((1, 1, d), lambda i, j: (i, 0, 0)),
                  part(), part(), part(), part(),
                  pl.BlockSpec(w_out.shape, lambda i, j: (0, 0))],
        out_specs=pl.BlockSpec((1, tm, d), lambda i, j: (i, j, 0)),
        out_shape=jax.ShapeDtypeStruct((b, l, d), F32),
        compiler_params=_cparams("arbitrary", "arbitrary"),
        name="out_proj",
    )(x, gate, *parts, w_out.astype(BF16))


SLAB = 8


def _store_slab(ref, val, start=0):
    rows = val.shape[0]
    for s in range(SLAB):
        ref[pl.ds(start * SLAB + s, rows, stride=SLAB), :] = val[:, s * LANES:(s + 1) * LANES]


def _load_slab(ref, start, rows):
    return jnp.concatenate([ref[pl.ds(start * SLAB + s, rows, stride=SLAB), :] for s in range(SLAB)], axis=1)


def _router_kernel(x_ref, sh_ref, sc_ref, g_ref, wh_ref, wl_ref, rb_ref, h_ref, idx_ref, gate_ref):
    x = x_ref[0]
    ms = jnp.mean(x * x, axis=-1, keepdims=True)
    h = (x * lax.rsqrt(ms + NORM_EPS)) * g_ref[...] * (1.0 + sc_ref[0]) + sh_ref[0]
    _store_slab(h_ref, h)
    hh, hl = _split_bf16(h)
    logits = _dot(hh, wh_ref[...]) + (_dot(hh, wl_ref[...]) + _dot(hl, wh_ref[...])) + rb_ref[...]
    tm = x.shape[0]
    lane = lax.broadcasted_iota(jnp.int32, (tm, N_EXPERTS), 1).astype(F32)
    out_lane = lax.broadcasted_iota(jnp.int32, (tm, LANES), 1)
    vals, idxs = [], []
    cur = logits
    for _ in range(TOP_K):
        mx = jnp.max(cur, axis=-1, keepdims=True)
        ix = jnp.min(jnp.where(cur == mx, lane, float(N_EXPERTS)), axis=-1, keepdims=True)
        vals.append(mx)
        idxs.append(ix)
        cur = jnp.where(lane == ix, -jnp.inf, cur)
    es = [jnp.exp(v - vals[0]) for v in vals]
    inv = 1.0 / (es[0] + es[1] + es[2] + es[3])
    idx_out = jnp.zeros((tm, LANES), F32)
    gate_out = jnp.zeros((tm, LANES), F32)
    for k in range(TOP_K):
        idx_out = jnp.where(out_lane == k, idxs[k], idx_out)
        gate_out = jnp.where(out_lane == k, es[k] * inv, gate_out)
    idx_ref[...] = idx_out.T[:SUBLANES].astype(jnp.int32)
    gate_ref[0] = gate_out


def router(x, shift, scale, g, router_w, router_b):
    b, l, d = x.shape
    assert d == SLAB * LANES
    tm = min(l, 512)
    per_seq = l // tm
    wh, wl = _split_bf16(router_w)
    vec3 = lambda: pl.BlockSpec((1, 1, d), lambda i, j: (i, 0, 0))
    tile = lambda w: pl.BlockSpec((1, tm, w), lambda i, j: (i, j, 0))
    h, idx, gates = pl.pallas_call(
        _router_kernel,
        grid=(b, per_seq),
        in_specs=[tile(d), vec3(), vec3(),
                  pl.BlockSpec((1, d), lambda i, j: (0, 0)),
                  pl.BlockSpec(wh.shape, lambda i, j: (0, 0)),
                  pl.BlockSpec(wl.shape, lambda i, j: (0, 0)),
                  pl.BlockSpec((1, N_EXPERTS), lambda i, j: (0, 0))],
        out_specs=[pl.BlockSpec((tm * SLAB, LANES), lambda i, j: (i * per_seq + j, 0)),
                   pl.BlockSpec((SUBLANES, tm), lambda i, j: (0, i * per_seq + j)), tile(LANES)],
        out_shape=[jax.ShapeDtypeStruct((b * l * SLAB, LANES), F32),
                   jax.ShapeDtypeStruct((SUBLANES, b * l), jnp.int32),
                   jax.ShapeDtypeStruct((b, l, LANES), F32)],
        compiler_params=_cparams("arbitrary", "arbitrary"),
        name="moe_router",
    )(x, shift, scale, g, wh, wl, router_b[None, :])
    return h, idx[:TOP_K], gates


DISPATCH_ROWS = 2048


def _dispatch_kernel(lo_ref, hi_ref, pos_ref, src_ref, dst_ref, zero_ref, sem, *, rows):
    i = pl.program_id(0)

    def copy(tok, k):
        src = src_ref.at[pl.ds(pl.multiple_of(tok * SLAB, SLAB), SLAB)]
        dst = dst_ref.at[pl.ds(pl.multiple_of(pos_ref[k * (rows // TOP_K) + tok], SLAB), SLAB)]
        return pltpu.make_async_copy(src, dst, sem)

    def start(tok, carry):
        for k in range(TOP_K):
            copy(tok, k).start(priority=k % 2)
        return carry
    lax.fori_loop(0, rows // TOP_K, start, 0, unroll=2)

    def wait(tok, carry):
        for k in range(TOP_K):
            copy(tok, k).wait()
        return carry
    lax.fori_loop(0, rows // TOP_K, wait, 0, unroll=2)

    @pl.when(i == pl.num_programs(0) - 1)
    def _():
        zero_ref[...] = jnp.zeros_like(zero_ref)

        def fill(r):
            return pltpu.make_async_copy(zero_ref, dst_ref.at[pl.ds(pl.multiple_of(r * SLAB, SLAB), SLAB)], sem)

        def per_expert(e, carry):
            def fill_start(r, c):
                fill(r).start()
                return c

            def fill_wait(r, c):
                fill(r).wait()
                return c
            lax.fori_loop(lo_ref[e], hi_ref[e], fill_start, 0)
            lax.fori_loop(lo_ref[e], hi_ref[e], fill_wait, 0)
            return carry
        lax.fori_loop(0, N_EXPERTS, per_expert, 0)


def dispatch_rows(h, pos, pad_lo, pad_hi, n_slots):
    rows = DISPATCH_ROWS
    tokens = rows // TOP_K
    pos_rows = _tiled_rows(pos, tokens)
    grid_spec = pltpu.PrefetchScalarGridSpec(
        num_scalar_prefetch=2,
        grid=(pos_rows.shape[0] // rows,),
        in_specs=[pl.BlockSpec((rows,), lambda i, lo, hi: (i,), memory_space=pltpu.SMEM),
                  pl.BlockSpec((tokens * SLAB, LANES), lambda i, lo, hi: (i, 0))],
        out_specs=pl.BlockSpec(memory_space=pl.ANY),
        scratch_shapes=[pltpu.VMEM((SLAB, LANES), h.dtype), pltpu.SemaphoreType.DMA(())])
    return pl.pallas_call(
        functools.partial(_dispatch_kernel, rows=rows),
        grid_spec=grid_spec,
        out_shape=jax.ShapeDtypeStruct((n_slots * SLAB, LANES), h.dtype),
        compiler_params=_cparams("arbitrary"),
        name="moe_dispatch",
    )(pad_lo, pad_hi, pos_rows, h)


def _expert_kernel(te_ref, tv_ref, x_ref, w1_ref, b1_ref, w2_ref, b2_ref, o_ref, w1_bf, w2_bf, *, tm):
    i = pl.program_id(0)
    ff = w2_ref.shape[1]
    new_expert = jnp.logical_or(i == 0, te_ref[i] != te_ref[jnp.maximum(i - 1, 0)])

    @pl.when(jnp.logical_and(tv_ref[i] > 0, new_expert))
    def _():
        w1_bf[...] = w1_ref[0].astype(BF16)
        w2_bf[...] = w2_ref[0].astype(BF16)

    @pl.when(tv_ref[i] > 0)
    def _():
        x = _load_slab(x_ref, 0, tm).astype(BF16)
        hid = _dot(x, w1_bf[...]) + b1_ref[0]
        gate = jnp.minimum(hid[:, :ff], SWIGLU_LIMIT)
        up = jnp.clip(hid[:, ff:], -SWIGLU_LIMIT, SWIGLU_LIMIT)
        act = gate * _sigmoid(SWIGLU_ALPHA * gate) * (up + 1.0)
        _store_slab(o_ref, _dot(act.astype(BF16), w2_bf[...]) + b2_ref[0])

    @pl.when(tv_ref[i] == 0)
    def _():
        o_ref[...] = jnp.zeros_like(o_ref)


def expert_ffn(xs, tile_expert, tile_valid, w1, b1, w2, b2, tm):
    n_slots = xs.shape[0] // SLAB
    ne, d, ff2 = w1.shape
    ff = w2.shape[1]
    slab = lambda: pl.BlockSpec((tm * SLAB, LANES), lambda i, te, tv: (i, 0))
    grid_spec = pltpu.PrefetchScalarGridSpec(
        num_scalar_prefetch=2,
        grid=(n_slots // tm,),
        in_specs=[pl.BlockSpec((tm * SLAB, LANES), lambda i, te, tv: (i * tv[i], 0)),
                  pl.BlockSpec((1, d, ff2), lambda i, te, tv: (te[i], 0, 0)),
                  pl.BlockSpec((1, 1, ff2), lambda i, te, tv: (te[i], 0, 0)),
                  pl.BlockSpec((1, ff, d), lambda i, te, tv: (te[i], 0, 0)),
                  pl.BlockSpec((1, 1, d), lambda i, te, tv: (te[i], 0, 0))],
        out_specs=slab(),
        scratch_shapes=[pltpu.VMEM((d, ff2), BF16), pltpu.VMEM((ff, d), BF16)])
    return pl.pallas_call(
        functools.partial(_expert_kernel, tm=tm),
        grid_spec=grid_spec,
        out_shape=jax.ShapeDtypeStruct((n_slots * SLAB, LANES), F32),
        compiler_params=_cparams("arbitrary"),
        name="moe_experts",
    )(tile_expert, tile_valid, xs, w1, b1.reshape(ne, 1, ff2), w2, b2.reshape(ne, 1, d))


def _combine_kernel(pos_ref, nxt_ref, x_ref, gt_ref, gates_ref, fin_ref, ys_ref, o_ref, buf, sems, *, tc, final):
    i = pl.program_id(0)
    n = tc * TOP_K
    slot = i % 2

    def copy(idx_ref, r, s):
        src = ys_ref.at[pl.ds(pl.multiple_of(idx_ref[r], SLAB), SLAB)]
        return pltpu.make_async_copy(src, buf.at[s, pl.ds(pl.multiple_of(r * SLAB, SLAB), SLAB)], sems.at[s])

    def issue(idx_ref, s):
        def body(r2, carry):
            for j in range(2):
                copy(idx_ref, 2 * r2 + j, s).start(priority=j)
            return carry
        lax.fori_loop(0, n // 2, body, 0, unroll=4)

    @pl.when(i == 0)
    def _():
        issue(pos_ref, 0)

    @pl.when(i + 1 < pl.num_programs(0))
    def _():
        issue(nxt_ref, 1 - slot)

    def drain(r, carry):
        copy(pos_ref, r, slot).wait()
        return carry
    lax.fori_loop(0, n, drain, 0, unroll=8)
    gates = gates_ref[...]
    y = jnp.zeros((tc, SLAB * LANES), F32)
    for k in range(TOP_K):
        y = y + gates[:, k:k + 1] * _load_slab(buf.at[slot], k * tc, tc)
    out = x_ref[...] + gt_ref[0] * y
    if final:
        ms = jnp.mean(out * out, axis=-1, keepdims=True)
        out = out * lax.rsqrt(ms + NORM_EPS) * fin_ref[...]
    o_ref[...] = out


def combine_rows(x, gate_vec, gates, pos, ys, final_g, final):
    b, l, d = x.shape
    t = b * l
    tc = min(l, 256)
    steps = t // tc
    per_seq = l // tc
    n = tc * TOP_K
    pos_t = _tiled_rows(pos, tc)
    gates_p = gates
    out = pl.pallas_call(
        functools.partial(_combine_kernel, tc=tc, final=final),
        grid=(steps,),
        in_specs=[pl.BlockSpec((n,), lambda i: (i,), memory_space=pltpu.SMEM),
                  pl.BlockSpec((n,), lambda i: (jnp.minimum(i + 1, steps - 1),), memory_space=pltpu.SMEM),
                  pl.BlockSpec((tc, d), lambda i: (i, 0)),
                  pl.BlockSpec((1, 1, d), lambda i: (i // per_seq, 0, 0)),
                  pl.BlockSpec((tc, LANES), lambda i: (i, 0)),
                  pl.BlockSpec((1, d), lambda i: (0, 0)),
                  pl.BlockSpec(memory_space=pl.ANY)],
        out_specs=pl.BlockSpec((tc, d), lambda i: (i, 0)),
        out_shape=jax.ShapeDtypeStruct((t, d), F32),
        scratch_shapes=[pltpu.VMEM((2, n * SLAB, LANES), F32), pltpu.SemaphoreType.DMA((2,))],
        compiler_params=_cparams("arbitrary"),
        name="moe_combine",
    )(pos_t, pos_t, x.reshape(t, d), gate_vec, gates_p, final_g[None, :], ys)
    return out.reshape(b, l, d)


def routing_plan(e, tm):
    k, t = e.shape
    n = k * t
    ef = e.reshape(n)
    onehot = (ef[:, None] == jnp.arange(N_EXPERTS, dtype=ef.dtype)[None, :]).astype(jnp.int32)
    csum = jnp.cumsum(onehot, axis=0)
    counts = csum[-1]
    rank = jnp.take_along_axis(csum, ef[:, None], axis=1)[:, 0] - 1
    padded = ((counts + tm - 1) // tm) * tm
    ends = jnp.cumsum(padded)
    starts = ends - padded
    pos = starts[ef] + rank
    n_slots = -(-(n + N_EXPERTS * (tm - 1)) // tm) * tm
    tile_start = jnp.arange(n_slots // tm, dtype=jnp.int32) * tm
    tile_expert = jnp.minimum(jnp.sum((tile_start[:, None] >= ends[None, :]).astype(jnp.int32), axis=1), N_EXPERTS - 1)
    tile_valid = (tile_start < ends[-1]).astype(jnp.int32)
    return (pos.reshape(k, t).astype(jnp.int32), (starts + counts).astype(jnp.int32), ends.astype(jnp.int32),
            n_slots, tile_expert.astype(jnp.int32), tile_valid)


def _tiled_rows(pos, tile):
    k, t = pos.shape
    return jnp.transpose(pos.reshape(k, t // tile, tile), (1, 0, 2)).reshape(k * t) * SLAB


MOE_TILE = 512


def moe_block(x, shift, scale, gate_vec, g, router_w, router_b, w1, b1, w2, b2, final_g, final):
    b, l, d = x.shape
    t = b * l
    h, e, gates = router(x, shift, scale, g, router_w, router_b)
    pos, pad_lo, pad_hi, n_slots, tile_expert, tile_valid = routing_plan(e, MOE_TILE)
    xs = dispatch_rows(h, pos, pad_lo, pad_hi, n_slots)
    ys = expert_ffn(xs, tile_expert, tile_valid, w1, b1, w2, b2, MOE_TILE)
    return combine_rows(x, gate_vec, gates.reshape(t, LANES), pos, ys, final_g, final)


def _diff_lambda(lp, lam_init):
    lp = lp.astype(F32)
    return jnp.exp(jnp.sum(lp[0] * lp[1])) - jnp.exp(jnp.sum(lp[2] * lp[3])) + lam_init


def kernel(x, c, ctx, c_ctx, norm1, norm2, w_mod, b_mod, w_in, w_out, attn_lambda, attn_subln,
           hyena_short_w, hyena_short_b, hyena_w1, hyena_b1, hyena_w2, hyena_b2, hyena_w3, hyena_freq,
           hyena_bias, pool_w, pool_scale, conv_dw_w, conv_dw_b, conv_ln_g, conv_ln_b, conv_pw_w,
           conv_pw_b, router_w, router_b, moe_w1, moe_b1, moe_w2, moe_b2, final_norm):
    bsz, n_lat, d = x.shape
    n_ctx = ctx.shape[1]
    depth = w_mod.shape[0]
    g = GROUP
    off_hy, off_pool, off_conv = 3 * g, 6 * g, 7 * g

    rows = -(-(bsz + 1) // SUBLANES) * SUBLANES
    cvec = jnp.zeros((rows, d), F32).at[:bsz].set(c).at[bsz].set(c_ctx)
    mods = adaln_all(cvec, w_mod, b_mod)

    cos_t, sin_t, partner = rope_tables(n_lat)
    ones_t = jnp.ones((n_ctx, g), F32)
    zeros_t = jnp.zeros((n_ctx, g), F32)
    qk_scale = HEAD_QK ** -0.5 * math.log2(math.e)

    xc = ctx
    for i in range(depth):
        last = i == depth - 1
        m = mods[i]
        lat = [m[:bsz, None, k * d:(k + 1) * d] for k in range(6)]
        cm = [jnp.broadcast_to(m[bsz, k * d:(k + 1) * d][None, None, :], (bsz, 1, d)) for k in range(6)]
        lam_init = 0.8 - 0.6 * math.exp(-0.3 * i)
        lam = _diff_lambda(attn_lambda[i], lam_init)
        wi = w_in[i]
        g1 = norm1[i][None, :]
        g2 = norm2[i][None, :]
        w1b = moe_w1[i]
        w2b = moe_w2[i]

        w_lat = jnp.concatenate([wi, wi[:, partner], wi[:, g + partner]], axis=1).astype(BF16)
        nin = wi.shape[1]
        segs = [(0, g, nin, qk_scale), (g, g, nin + g, 1.0), (2 * g, g, None, 1.0),
                (off_hy, 3 * g, None, 1.0), (off_pool, g, None, 1.0), (off_conv, 2 * g, None, 1.0)]
        q, k, v, p_hy, p_pool, p_conv = modulated_projection(
            x, lat[0], lat[1], g1, w_lat, segs, [BF16, BF16, BF16, F32, F32, F32], cos_t, sin_t)

        if last:
            w_ctx = wi[:, g:3 * g].astype(BF16)
            kc, vc = modulated_projection(xc, cm[0], cm[1], g1, w_ctx,
                                          [(0, g, None, 1.0), (g, g, None, 1.0)], [BF16, BF16], ones_t, zeros_t)
        else:
            csegs = [(0, g, None, qk_scale), (g, g, None, 1.0), (2 * g, g, None, 1.0),
                     (off_hy, 3 * g, None, 1.0), (off_pool, g, None, 1.0), (off_conv, 2 * g, None, 1.0)]
            qc, kc, vc, pc_hy, pc_pool, pc_conv = modulated_projection(
                xc, cm[0], cm[1], g1, wi.astype(BF16), csegs, [BF16, BF16, BF16, F32, F32, F32], ones_t, zeros_t)

        kt_all = jnp.transpose(jnp.concatenate([kc, k], axis=1), (0, 2, 1))
        v_all = jnp.concatenate([vc, v], axis=1)
        o_attn = diff_attention(q, kt_all, v_all, lam, attn_subln[i], 1.0 - lam_init)

        filt_params = (hyena_w1[i], hyena_b1[i], hyena_w2[i], hyena_b2[i], hyena_w3[i], hyena_freq[i])

        def local_groups(p_hy_, p_pool_, p_conv_, n_tokens):
            g_rows = filter_rows(hyena_filters(n_tokens, *filt_params), n_tokens)
            u = hyena_short_conv(p_hy_, hyena_short_w[i], hyena_short_b[i])
            o_hy = hyena_long_conv(u, g_rows, hyena_bias[i])
            o_pool = pool_mixer(p_pool_, pool_w[i], pool_scale[i])
            o_conv = conformer_mixer(p_conv_, conv_dw_w[i], conv_dw_b[i], conv_ln_g[i], conv_ln_b[i],
                                     conv_pw_w[i], conv_pw_b[i])
            return o_hy, o_pool, o_conv

        o_hy, o_pool, o_conv = local_groups(p_hy, p_pool, p_conv, n_lat)
        x = out_projection(x, lat[2], (o_attn, o_hy, o_pool, o_conv), w_out[i])
        if not last:
            oc_attn = diff_attention(qc, jnp.transpose(kc, (0, 2, 1)), vc, lam, attn_subln[i], 1.0 - lam_init)
            oc_hy, oc_pool, oc_conv = local_groups(pc_hy, pc_pool, pc_conv, n_ctx)
            xc = out_projection(xc, cm[2], (oc_attn, oc_hy, oc_pool, oc_conv), w_out[i])

        x = moe_block(x, lat[3], lat[4], lat[5], g2, router_w[i], router_b[i], w1b, moe_b1[i], w2b, moe_b2[i],
                      final_norm, last)
        if not last:
            xc = moe_block(xc, cm[3], cm[4], cm[5], g2, router_w[i], router_b[i], w1b, moe_b1[i], w2b, moe_b2[i],
                           final_norm, False)
    return x
```

```python
import functools
import math

import jax
import jax.numpy as jnp
from jax import lax
from jax.experimental import pallas as pl
from jax.experimental.pallas import tpu as pltpu

F32 = jnp.float32
BF16 = jnp.bfloat16

GRID_W = 64
N_HEADS = 4
HEAD_V = 64
HEAD_QK = 32
GROUP = 256
ROPE_BASE = 10000.0
SUBLN_EPS = 1e-5
NORM_EPS = 1e-6
LN_EPS = 1e-5
POOL_WINDOWS = (2, 4, 8, 16)
CONV_KERNEL = 31
HYENA_SHORT = 3
HYENA_EMB = 33
HYENA_TARGET = 1e-2
HYENA_FAST_PCT = 0.3
HYENA_SLOW_PCT = 1.5
N_EXPERTS = 32
TOP_K = 4
SWIGLU_ALPHA = 1.702
SWIGLU_LIMIT = 7.0

LANES = 128
SUBLANES = 8
VMEM_LIMIT = 56 * 1024 * 1024
HALO = 16


def _cparams(*sem):
    return pltpu.CompilerParams(dimension_semantics=sem, vmem_limit_bytes=VMEM_LIMIT)


def _split_bf16(a):
    hi = a.astype(BF16)
    lo = (a - hi.astype(F32)).astype(BF16)
    return hi, lo


def _dot(a, b):
    return jnp.dot(a, b, preferred_element_type=F32)


def _dot3(a, b):
    ah, al = _split_bf16(a)
    bh, bl = _split_bf16(b)
    return _dot(ah, bh) + (_dot(ah, bl) + _dot(al, bh))


def _sigmoid(x):
    return 1.0 / (1.0 + jnp.exp(-x))


def _adaln_kernel(c_ref, w_ref, b_ref, o_ref):
    cv = c_ref[...]
    s = cv * _sigmoid(cv)
    o_ref[0] = _dot3(s, w_ref[0]) + b_ref[0]


def adaln_all(cvec, w_mod, b_mod):
    depth, d, n = w_mod.shape
    r = cvec.shape[0]
    tn = 1536
    return pl.pallas_call(
        _adaln_kernel,
        grid=(depth, n // tn),
        in_specs=[pl.BlockSpec((r, d), lambda i, j: (0, 0)),
                  pl.BlockSpec((1, d, tn), lambda i, j: (i, 0, j)),
                  pl.BlockSpec((1, 1, tn), lambda i, j: (i, 0, j))],
        out_specs=pl.BlockSpec((1, r, tn), lambda i, j: (i, 0, j)),
        out_shape=jax.ShapeDtypeStruct((depth, r, n), F32),
        compiler_params=_cparams("arbitrary", "arbitrary"),
        name="adaln",
    )(cvec, w_mod, b_mod.reshape(depth, 1, n))


def _filter_kernel(feat_ref, w1_ref, b1_ref, w2_ref, b2_ref, w3_ref, fr_ref, dec_ref, o_ref):
    fr = fr_ref[...]
    z = jnp.sin(fr * (_dot3(feat_ref[...], w1_ref[...]) + b1_ref[...]))
    z = jnp.sin(fr * (_dot3(z, w2_ref[...]) + b2_ref[...]))
    o_ref[...] = _dot3(z, w3_ref[...]) * dec_ref[...]


def hyena_filters(n_tokens, w1, b1, w2, b2, w3, freq):
    bands = (HYENA_EMB - 1) // 2
    t = jnp.linspace(0.0, 1.0, n_tokens, dtype=F32)[:, None]
    w = 2.0 * math.pi * jnp.arange(n_tokens, dtype=F32)[:, None] / n_tokens
    f = jnp.linspace(1e-4, bands - 1, bands, dtype=F32)[None, :]
    feat = jnp.concatenate([t, jnp.cos(f * w), -jnp.sin(f * w)], axis=-1)
    kpad = 40
    feat = jnp.pad(feat, ((0, 0), (0, kpad - HYENA_EMB)))
    w1p = jnp.pad(w1, ((0, kpad - HYENA_EMB), (0, 0)))
    min_decay = math.log(HYENA_TARGET) / HYENA_SLOW_PCT
    max_decay = math.log(HYENA_TARGET) / HYENA_FAST_PCT
    deltas = jnp.abs(jnp.linspace(min_decay, max_decay, GROUP, dtype=F32))
    decay = jnp.exp(-t * deltas)
    nf = w3.shape[1]
    decay = jnp.tile(decay, (1, nf // GROUP))
    ffn = w2.shape[0]
    tl = min(n_tokens, 512)
    full = lambda shape: pl.BlockSpec(shape, lambda i: (0, 0))
    return pl.pallas_call(
        _filter_kernel,
        grid=(n_tokens // tl,),
        in_specs=[pl.BlockSpec((tl, kpad), lambda i: (i, 0)),
                  full((kpad, ffn)), full((1, ffn)), full((ffn, ffn)), full((1, ffn)),
                  full((ffn, nf)), full((1, ffn)),
                  pl.BlockSpec((tl, nf), lambda i: (i, 0))],
        out_specs=pl.BlockSpec((tl, nf), lambda i: (i, 0)),
        out_shape=jax.ShapeDtypeStruct((n_tokens, nf), F32),
        compiler_params=_cparams("arbitrary"),
        name="hyena_filter",
    )(feat, w1p, b1[None, :], w2, b2[None, :], w3, freq[None, :], decay)


def filter_rows(filt, n_tokens):
    order = filt.shape[1] // (2 * GROUP)
    f4 = filt.reshape(n_tokens, order, 2, GROUP)
    hf = f4[:, :, 0]
    hb = f4[:, :, 1]
    rows = jnp.concatenate([jnp.zeros_like(hf[:1]), hb[:0:-1], hf], axis=0)
    return jnp.transpose(rows, (1, 2, 0))[:, :, None, :]


def _proj_kernel(x_ref, sh_ref, sc_ref, g_ref, w_ref, cos_ref, sin_ref, *out_refs, segs):
    x = x_ref[0]
    ms = jnp.mean(x * x, axis=-1, keepdims=True)
    h = (x * lax.rsqrt(ms + NORM_EPS)) * g_ref[...] * (1.0 + sc_ref[0]) + sh_ref[0]
    hb = h.astype(BF16)
    for o_ref, (start, width, swap_start, scale) in zip(out_refs, segs):
        p = _dot(hb, w_ref[:, start:start + width])
        if swap_start is not None:
            ps = _dot(hb, w_ref[:, swap_start:swap_start + width])
            p = p * cos_ref[...] + ps * sin_ref[...]
        if scale != 1.0:
            p = p * scale
        o_ref[0] = p.astype(o_ref.dtype)


def modulated_projection(x, shift, scale, g, w, segs, out_dtypes, cos_t, sin_t):
    b, l, d = x.shape
    tm = min(l, 512)
    nw = w.shape[1]
    rw = cos_t.shape[1]
    outs = [jax.ShapeDtypeStruct((b, l, s[1]), dt) for s, dt in zip(segs, out_dtypes)]
    return pl.pallas_call(
        functools.partial(_proj_kernel, segs=tuple(segs)),
        grid=(b, l // tm),
        in_specs=[pl.BlockSpec((1, tm, d), lambda i, j: (i, j, 0)),
                  pl.BlockSpec((1, 1, d), lambda i, j: (i, 0, 0)),
                  pl.BlockSpec((1, 1, d), lambda i, j: (i, 0, 0)),
                  pl.BlockSpec((1, d), lambda i, j: (0, 0)),
                  pl.BlockSpec((d, nw), lambda i, j: (0, 0)),
                  pl.BlockSpec((tm, rw), lambda i, j: (j, 0)),
                  pl.BlockSpec((tm, rw), lambda i, j: (j, 0))],
        out_specs=[pl.BlockSpec((1, tm, s[1]), lambda i, j: (i, j, 0)) for s in segs],
        out_shape=outs,
        compiler_params=_cparams("arbitrary", "arbitrary"),
        name="mod_proj",
    )(x, shift, scale, g, w, cos_t, sin_t)


def rope_tables(n_tokens):
    rows = n_tokens // GRID_W
    row = jnp.repeat(jnp.arange(rows, dtype=F32), GRID_W)
    col = jnp.tile(jnp.arange(GRID_W, dtype=F32), rows)
    n_freq = HEAD_QK // 4
    inv = ROPE_BASE ** (-jnp.arange(n_freq, dtype=F32) / n_freq)
    lane = jnp.arange(GROUP)
    d = lane % HEAD_QK
    pos = jnp.where((d < 2 * n_freq)[None, :], row[:, None], col[:, None])
    ang = pos * inv[d % n_freq][None, :]
    first = (d % (2 * n_freq)) < n_freq
    cos_t = jnp.cos(ang)
    sin_t = jnp.where(first[None, :], -jnp.sin(ang), jnp.sin(ang))
    partner = jnp.where(first, lane + n_freq, lane - n_freq)
    return cos_t, sin_t, partner


ATTN_ROWS = 16


def _attn_kernel(lam_ref, q_ref, kt_ref, v_ref, g_ref, o_ref, s_even, s_odd, a_even, a_odd, *, out_scale):
    q = q_ref[0]
    lam = lam_ref[0]
    tq = q.shape[0]
    lane = lax.broadcasted_iota(jnp.int32, (1, GROUP), 1)
    map_of_lane = lane // HEAD_QK
    head_of_lane = lane // HEAD_V
    s_bufs = (s_even, s_odd)
    a_bufs = (a_even, a_odd)

    z = 0

    def scores(h):
        for m in range(2):
            keep = jnp.where(map_of_lane == 2 * h + m, 1.0, 0.0).astype(BF16)
            s_bufs[h % 2][z + m] = _dot(q * keep, kt_ref[0])

    def softmax(h):
        s_ref, a_ref = s_bufs[h % 2], a_bufs[h % 2]
        inv = []
        for r in range(0, tq, ATTN_ROWS):
            s1 = s_ref[z + 1, r:r + ATTN_ROWS, :]
            e1 = jnp.exp2(s1 - jnp.max(s1, axis=-1, keepdims=True))
            sum1 = jnp.sum(e1, axis=-1, keepdims=True)
            s0 = s_ref[z, r:r + ATTN_ROWS, :]
            e0 = jnp.exp2(s0 - jnp.max(s0, axis=-1, keepdims=True))
            sum0 = jnp.sum(e0, axis=-1, keepdims=True)
            a_ref[z, r:r + ATTN_ROWS, :] = (e0 - e1 * (lam * sum0 / sum1)).astype(BF16)
            inv.append(1.0 / sum0)
        return jnp.concatenate(inv, axis=0)

    acc = jnp.zeros((tq, GROUP), F32)
    scores(0)
    for h in range(N_HEADS):
        if h + 1 < N_HEADS:
            scores(h + 1)
        inv0 = softmax(h)
        o = _dot(a_bufs[h % 2][z], v_ref[0]) * inv0
        acc = jnp.where(head_of_lane == h, o, acc)
    sq = acc * acc
    ms = jnp.zeros((tq, GROUP), F32)
    for h in range(N_HEADS):
        hsel = head_of_lane == h
        mh = jnp.sum(jnp.where(hsel, sq, 0.0), axis=-1, keepdims=True) * (1.0 / HEAD_V)
        ms = jnp.where(hsel, mh, ms)
    y = acc * lax.rsqrt(ms + SUBLN_EPS) * g_ref[...] * out_scale
    o_ref[0] = y.astype(o_ref.dtype)


def diff_attention(q, kt, v, lam, subln, out_scale):
    b, nq, _ = q.shape
    nk = v.shape[1]
    tq = min(nq, 512)
    g = jnp.tile(subln, N_HEADS)[None, :]
    return pl.pallas_call(
        functools.partial(_attn_kernel, out_scale=out_scale),
        grid=(b, nq // tq),
        in_specs=[pl.BlockSpec(memory_space=pltpu.SMEM),
                  pl.BlockSpec((1, tq, GROUP), lambda i, j: (i, j, 0)),
                  pl.BlockSpec((1, GROUP, nk), lambda i, j: (i, 0, 0)),
                  pl.BlockSpec((1, nk, GROUP), lambda i, j: (i, 0, 0)),
                  pl.BlockSpec((1, GROUP), lambda i, j: (0, 0))],
        out_specs=pl.BlockSpec((1, tq, GROUP), lambda i, j: (i, j, 0)),
        out_shape=jax.ShapeDtypeStruct((b, nq, GROUP), BF16),
        scratch_shapes=[pltpu.VMEM((2, tq, nk), F32), pltpu.VMEM((2, tq, nk), F32),
                        pltpu.VMEM((1, tq, nk), BF16), pltpu.VMEM((1, tq, nk), BF16)],
        compiler_params=_cparams("arbitrary", "arbitrary"),
        name="diff_attn",
    )(lam.reshape(1), q, kt, v, g)


def _fill_padded(pad_ref, n, write_rows):
    zeros = jnp.zeros((HALO, pad_ref.shape[1]), F32)
    pad_ref[0:HALO, :] = zeros
    pad_ref[HALO + n:HALO + n + HALO, :] = zeros
    write_rows()


def _shifted_rows(pad_ref, r0, ch, offsets):
    rows = ch + 2 * HALO
    blk = pad_ref[pl.ds(r0, rows), :]
    rolled = {0: blk}
    out = {}
    for k in offsets:
        o = HALO + k
        r = o % SUBLANES
        if r not in rolled:
            rolled[r] = pltpu.roll(blk, rows - r, 0)
        out[k] = rolled[r][o - r:o - r + ch]
    return out


def _pool_kernel(u_ref, w_ref, sc_ref, o_ref, pad_ref, *, n, ch):
    def copy(i, c):
        r0 = pl.multiple_of(i * ch, ch)
        pad_ref[pl.ds(HALO + r0, ch), :] = u_ref[0, pl.ds(r0, ch), :]
        return c
    _fill_padded(pad_ref, n, lambda: lax.fori_loop(0, n // ch, copy, 0))
    lane = lax.broadcasted_iota(jnp.int32, (1, GROUP), 1)
    pg = GROUP // len(POOL_WINDOWS)
    reach = POOL_WINDOWS[-1] // 2

    def body(i, c):
        r0 = pl.multiple_of(i * ch, ch)
        taps = _shifted_rows(pad_ref, r0, ch, range(-reach, reach))
        ld = lambda k: taps[k]
        t = r0 + lax.broadcasted_iota(jnp.int32, (ch, 1), 0)
        u = ld(0)
        sums = []
        s = u + ld(-1)
        sums.append(s)
        half = 1
        for _ in POOL_WINDOWS[1:]:
            for k in range(half, 2 * half):
                s = s + ld(k) + ld(-k - 1)
            half *= 2
            sums.append(s)
        pooled = None
        for gi, win in reversed(list(enumerate(POOL_WINDOWS))):
            hi = jnp.minimum(t + (win - win // 2), n)
            lo = jnp.maximum(t - win // 2, 0)
            mean = sums[gi] / (hi - lo).astype(F32)
            pooled = mean if pooled is None else jnp.where(lane < (gi + 1) * pg, mean, pooled)
        pooled = pooled - u
        y = _dot(pooled.astype(BF16), w_ref[...]) * sc_ref[...]
        o_ref[0, pl.ds(r0, ch), :] = y.astype(o_ref.dtype)
        return c
    lax.fori_loop(0, n // ch, body, 0)


def pool_mixer(u, w_pool, scale):
    b, n, c = u.shape
    ng, pg, _ = w_pool.shape
    wbd = jnp.zeros((c, c), F32)
    for gi in range(ng):
        wbd = wbd.at[gi * pg:(gi + 1) * pg, gi * pg:(gi + 1) * pg].set(w_pool[gi])
    ch = min(n, 256)
    return pl.pallas_call(
        functools.partial(_pool_kernel, n=n, ch=ch),
        grid=(b,),
        in_specs=[pl.BlockSpec((1, n, c), lambda i: (i, 0, 0)),
                  pl.BlockSpec((c, c), lambda i: (0, 0)),
                  pl.BlockSpec((1, c), lambda i: (0, 0))],
        out_specs=pl.BlockSpec((1, n, c), lambda i: (i, 0, 0)),
        out_shape=jax.ShapeDtypeStruct((b, n, c), BF16),
        scratch_shapes=[pltpu.VMEM((n + 2 * HALO, c), F32)],
        compiler_params=_cparams("arbitrary"),
        name="pool_mixer",
    )(u, wbd.astype(BF16), scale[None, :])


def _conformer_kernel(p_ref, dw_ref, dwb_ref, lg_ref, lb_ref, pw_ref, pwb_ref, o_ref, pad_ref, *, n, ch):
    c = GROUP

    def glu(i, carry):
        r0 = pl.multiple_of(i * ch, ch)
        a = p_ref[0, pl.ds(r0, ch), 0:c]
        g = p_ref[0, pl.ds(r0, ch), c:2 * c]
        pad_ref[pl.ds(HALO + r0, ch), :] = a * _sigmoid(g)
        return carry
    _fill_padded(pad_ref, n, lambda: lax.fori_loop(0, n // ch, glu, 0))
    half = (CONV_KERNEL - 1) // 2

    def body(i, carry):
        r0 = pl.multiple_of(i * ch, ch)
        acc = jnp.zeros((ch, c), F32) + dwb_ref[...]
        taps = _shifted_rows(pad_ref, r0, ch, range(-half, half + 1))
        for k in range(CONV_KERNEL):
            acc = acc + taps[k - half] * dw_ref[k:k + 1, :]
        mu = jnp.mean(acc, axis=-1, keepdims=True)
        xc = acc - mu
        var = jnp.mean(xc * xc, axis=-1, keepdims=True)
        y = xc * lax.rsqrt(var + LN_EPS) * lg_ref[...] + lb_ref[...]
        y = y * _sigmoid(y)
        out = _dot(y.astype(BF16), pw_ref[...]) + pwb_ref[...]
        o_ref[0, pl.ds(r0, ch), :] = out.astype(o_ref.dtype)
        return carry
    lax.fori_loop(0, n // ch, body, 0)


def conformer_mixer(p, dw_w, dw_b, ln_g, ln_b, pw_w, pw_b):
    b, n, c2 = p.shape
    c = c2 // 2
    ch = min(n, 128)
    kp = 32
    dw = jnp.pad(dw_w, ((0, kp - CONV_KERNEL), (0, 0)))
    vec = lambda: pl.BlockSpec((1, c), lambda i: (0, 0))
    return pl.pallas_call(
        functools.partial(_conformer_kernel, n=n, ch=ch),
        grid=(b,),
        in_specs=[pl.BlockSpec((1, n, c2), lambda i: (i, 0, 0)),
                  pl.BlockSpec((kp, c), lambda i: (0, 0)),
                  vec(), vec(), vec(),
                  pl.BlockSpec((c, c), lambda i: (0, 0)),
                  vec()],
        out_specs=pl.BlockSpec((1, n, c), lambda i: (i, 0, 0)),
        out_shape=jax.ShapeDtypeStruct((b, n, c), BF16),
        scratch_shapes=[pltpu.VMEM((n + 2 * HALO, c), F32)],
        compiler_params=_cparams("arbitrary"),
        name="conformer_mixer",
    )(p, dw, dw_b[None, :], ln_g[None, :], ln_b[None, :], pw_w.astype(BF16), pw_b[None, :])


def _short_conv_kernel(p_ref, w_ref, b_ref, o_ref, pad_ref, *, n, ch):
    def copy(i, carry):
        r0 = pl.multiple_of(i * ch, ch)
        pad_ref[pl.ds(HALO + r0, ch), :] = p_ref[0, pl.ds(r0, ch), :]
        return carry
    _fill_padded(pad_ref, n, lambda: lax.fori_loop(0, n // ch, copy, 0))
    half = (HYENA_SHORT - 1) // 2

    def body(i, carry):
        r0 = pl.multiple_of(i * ch, ch)
        acc = jnp.zeros((ch, pad_ref.shape[1]), F32) + b_ref[...]
        taps = _shifted_rows(pad_ref, r0, ch, range(-half, half + 1))
        for k in range(HYENA_SHORT):
            acc = acc + taps[k - half] * w_ref[k:k + 1, :]
        o_ref[0, pl.ds(r0, ch), :] = acc.astype(o_ref.dtype)
        return carry
    lax.fori_loop(0, n // ch, body, 0)


def hyena_short_conv(p, w, bias):
    b, n, c = p.shape
    ch = min(n, 128)
    wp = jnp.pad(w, ((0, SUBLANES - HYENA_SHORT), (0, 0)))
    return pl.pallas_call(
        functools.partial(_short_conv_kernel, n=n, ch=ch),
        grid=(b,),
        in_specs=[pl.BlockSpec((1, n, c), lambda i: (i, 0, 0)),
                  pl.BlockSpec((SUBLANES, c), lambda i: (0, 0)),
                  pl.BlockSpec((1, c), lambda i: (0, 0))],
        out_specs=pl.BlockSpec((1, n, c), lambda i: (i, 0, 0)),
        out_shape=jax.ShapeDtypeStruct((b, n, c), BF16),
        scratch_shapes=[pltpu.VMEM((n + 2 * HALO, c), F32)],
        compiler_params=_cparams("arbitrary"),
        name="hyena_short_conv",
    )(p, wp, bias[None, :])


def _toeplitz_conv(g_ref, u, nb, tb, bsz):
    ys = [jnp.zeros((bsz, tb), F32) for _ in range(nb)]
    for d in range(-(nb - 1), nb):
        start = tb * (nb - 1 + d)
        win = jnp.broadcast_to(g_ref[:, start:start + 2 * tb], (tb, 2 * tb))
        rolled = pltpu.roll(win, 0, 1, stride=1, stride_axis=0)
        blk = rolled[:, tb:2 * tb].astype(BF16)
        j_lo, j_hi = max(0, -d), min(nb, nb - d)
        o = _dot(u[j_lo * bsz:j_hi * bsz], blk)
        for j in range(j_lo, j_hi):
            ys[j + d] = ys[j + d] + o[(j - j_lo) * bsz:(j - j_lo + 1) * bsz]
    return ys


def _hyena_kernel(bias_ref, g_ref, v_ref, x1_ref, x2_ref, o_ref, *, nb, tb, bsz, cb):
    base = pl.program_id(0) * cb

    def body(ci, carry):
        v = v_ref[ci]
        y1 = jnp.concatenate(_toeplitz_conv(g_ref.at[0, ci], v, nb, tb, bsz), axis=0)
        z = x1_ref[ci].astype(F32) * (y1 + bias_ref[0, base + ci] * v.astype(F32))
        y2 = jnp.concatenate(_toeplitz_conv(g_ref.at[1, ci], z.astype(BF16), nb, tb, bsz), axis=0)
        out = x2_ref[ci].astype(F32) * (y2 + bias_ref[1, base + ci] * z)
        o_ref[ci] = out.astype(o_ref.dtype)
        return carry
    lax.fori_loop(0, cb, body, 0, unroll=2)


def hyena_long_conv(u, g_rows, bias):
    b, n, _ = u.shape
    tb = min(n, 256)
    nb = n // tb
    nbb = nb * b
    ut = jnp.transpose(u.reshape(b, nb, tb, 3, GROUP), (3, 4, 1, 0, 2)).reshape(3, GROUP, nbb, tb)
    cb = 8
    blk = lambda part: pl.BlockSpec((None, cb, nbb, tb), lambda i, part=part: (part, i, 0, 0))
    out = pl.pallas_call(
        functools.partial(_hyena_kernel, nb=nb, tb=tb, bsz=b, cb=cb),
        grid=(GROUP // cb,),
        in_specs=[pl.BlockSpec(memory_space=pltpu.SMEM),
                  pl.BlockSpec((2, cb, 1, 2 * n), lambda i: (0, i, 0, 0)),
                  blk(0), blk(1), blk(2)],
        out_specs=pl.BlockSpec((cb, nbb, tb), lambda i: (i, 0, 0)),
        out_shape=jax.ShapeDtypeStruct((GROUP, nbb, tb), BF16),
        compiler_params=_cparams("arbitrary"),
        name="hyena_long_conv",
    )(bias, g_rows, ut, ut, ut)
    return jnp.transpose(out.reshape(GROUP, nb, b, tb), (2, 1, 3, 0)).reshape(b, n, GROUP)


def _out_proj_kernel(x_ref, gt_ref, a_ref, h_ref, p_ref, c_ref, w_ref, o_ref):
    acc = _dot(a_ref[0].astype(BF16), w_ref[0:GROUP, :])
    acc = acc + _dot(h_ref[0].astype(BF16), w_ref[GROUP:2 * GROUP, :])
    acc = acc + _dot(p_ref[0].astype(BF16), w_ref[2 * GROUP:3 * GROUP, :])
    acc = acc + _dot(c_ref[0].astype(BF16), w_ref[3 * GROUP:4 * GROUP, :])
    o_ref[0] = x_ref[0] + gt_ref[0] * acc


def out_projection(x, gate, parts, w_out):
    b, l, d = x.shape
    tm = min(l, 512)
    part = lambda: pl.BlockSpec((1, tm, GROUP), lambda i, j: (i, j, 0))
    return pl.pallas_call(
        _out_proj_kernel,
        grid=(b, l // tm),
        in_specs=[pl.BlockSpec((1, tm, d), lambda i, j: (i, j, 0)),
                  pl.BlockSpec((1, 1, d), lambda i, j: (i, 0, 0)),
                  part(), part(), part(), part(),
                  pl.BlockSpec(w_out.shape, lambda i, j: (0, 0))],
        out_specs=pl.BlockSpec((1, tm, d), lambda i, j: (i, j, 0)),
        out_shape=jax.ShapeDtypeStruct((b, l, d), F32),
        compiler_params=_cparams("arbitrary", "arbitrary"),
        name="out_proj",
    )(x, gate, *parts, w_out.astype(BF16))


SLAB = 8


def _store_slab(ref, val, start=0):
    rows = val.shape[0]
    for s in range(SLAB):
        ref[pl.ds(start * SLAB + s, rows, stride=SLAB), :] = val[:, s * LANES:(s + 1) * LANES]


def _load_slab(ref, start, rows):
    return jnp.concatenate([ref[pl.ds(start * SLAB + s, rows, stride=SLAB), :] for s in range(SLAB)], axis=1)


def _router_kernel(x_ref, sh_ref, sc_ref, g_ref, wh_ref, wl_ref, rb_ref, h_ref, idx_ref, gate_ref):
    x = x_ref[0]
    ms = jnp.mean(x * x, axis=-1, keepdims=True)
    h = (x * lax.rsqrt(ms + NORM_EPS)) * g_ref[...] * (1.0 + sc_ref[0]) + sh_ref[0]
    _store_slab(h_ref, h)
    hh, hl = _split_bf16(h)
    logits = _dot(hh, wh_ref[...]) + (_dot(hh, wl_ref[...]) + _dot(hl, wh_ref[...])) + rb_ref[...]
    tm = x.shape[0]
    lane = lax.broadcasted_iota(jnp.int32, (tm, N_EXPERTS), 1).astype(F32)
    out_lane = lax.broadcasted_iota(jnp.int32, (tm, LANES), 1)
    vals, idxs = [], []
    cur = logits
    for _ in range(TOP_K):
        mx = jnp.max(cur, axis=-1, keepdims=True)
        ix = jnp.min(jnp.where(cur == mx, lane, float(N_EXPERTS)), axis=-1, keepdims=True)
        vals.append(mx)
        idxs.append(ix)
        cur = jnp.where(lane == ix, -jnp.inf, cur)
    es = [jnp.exp(v - vals[0]) for v in vals]
    inv = 1.0 / (es[0] + es[1] + es[2] + es[3])
    idx_out = jnp.zeros((tm, LANES), F32)
    gate_out = jnp.zeros((tm, LANES), F32)
    for k in range(TOP_K):
        idx_out = jnp.where(out_lane == k, idxs[k], idx_out)
        gate_out = jnp.where(out_lane == k, es[k] * inv, gate_out)
    idx_ref[...] = idx_out.T[:SUBLANES].astype(jnp.int32)
    gate_ref[0] = gate_out


def router(x, shift, scale, g, router_w, router_b):
    b, l, d = x.shape
    assert d == SLAB * LANES
    tm = min(l, 512)
    per_seq = l // tm
    wh, wl = _split_bf16(router_w)
    vec3 = lambda: pl.BlockSpec((1, 1, d), lambda i, j: (i, 0, 0))
    tile = lambda w: pl.BlockSpec((1, tm, w), lambda i, j: (i, j, 0))
    h, idx, gates = pl.pallas_call(
        _router_kernel,
        grid=(b, per_seq),
        in_specs=[tile(d), vec3(), vec3(),
                  pl.BlockSpec((1, d), lambda i, j: (0, 0)),
                  pl.BlockSpec(wh.shape, lambda i, j: (0, 0)),
                  pl.BlockSpec(wl.shape, lambda i, j: (0, 0)),
                  pl.BlockSpec((1, N_EXPERTS), lambda i, j: (0, 0))],
        out_specs=[pl.BlockSpec((tm * SLAB, LANES), lambda i, j: (i * per_seq + j, 0)),
                   pl.BlockSpec((SUBLANES, tm), lambda i, j: (0, i * per_seq + j)), tile(LANES)],
        out_shape=[jax.ShapeDtypeStruct((b * l * SLAB, LANES), F32),
                   jax.ShapeDtypeStruct((SUBLANES, b * l), jnp.int32),
                   jax.ShapeDtypeStruct((b, l, LANES), F32)],
        compiler_params=_cparams("arbitrary", "arbitrary"),
        name="moe_router",
    )(x, shift, scale, g, wh, wl, router_b[None, :])
    return h, idx[:TOP_K], gates


DISPATCH_ROWS = 2048


def _dispatch_kernel(lo_ref, hi_ref, pos_ref, src_ref, dst_ref, zero_ref, sem, *, rows):
    i = pl.program_id(0)

    def copy(tok, k):
        src = src_ref.at[pl.ds(pl.multiple_of(tok * SLAB, SLAB), SLAB)]
        dst = dst_ref.at[pl.ds(pl.multiple_of(pos_ref[k * (rows // TOP_K) + tok], SLAB), SLAB)]
        return pltpu.make_async_copy(src, dst, sem)

    def start(tok, carry):
        for k in range(TOP_K):
            copy(tok, k).start(priority=k % 2)
        return carry
    lax.fori_loop(0, rows // TOP_K, start, 0, unroll=2)

    def wait(tok, carry):
        for k in range(TOP_K):
            copy(tok, k).wait()
        return carry
    lax.fori_loop(0, rows // TOP_K, wait, 0, unroll=2)

    @pl.when(i == pl.num_programs(0) - 1)
    def _():
        zero_ref[...] = jnp.zeros_like(zero_ref)

        def fill(r):
            return pltpu.make_async_copy(zero_ref, dst_ref.at[pl.ds(pl.multiple_of(r * SLAB, SLAB), SLAB)], sem)

        def per_expert(e, carry):
            def fill_start(r, c):
                fill(r).start()
                return c

            def fill_wait(r, c):
                fill(r).wait()
                return c
            lax.fori_loop(lo_ref[e], hi_ref[e], fill_start, 0)
            lax.fori_loop(lo_ref[e], hi_ref[e], fill_wait, 0)
            return carry
        lax.fori_loop(0, N_EXPERTS, per_expert, 0)


def dispatch_rows(h, pos, pad_lo, pad_hi, n_slots):
    rows = DISPATCH_ROWS
    tokens = rows // TOP_K
    pos_rows = _tiled_rows(pos, tokens)
    grid_spec = pltpu.PrefetchScalarGridSpec(
        num_scalar_prefetch=2,
        grid=(pos_rows.shape[0] // rows,),
        in_specs=[pl.BlockSpec((rows,), lambda i, lo, hi: (i,), memory_space=pltpu.SMEM),
                  pl.BlockSpec((tokens * SLAB, LANES), lambda i, lo, hi: (i, 0))],
        out_specs=pl.BlockSpec(memory_space=pl.ANY),
        scratch_shapes=[pltpu.VMEM((SLAB, LANES), h.dtype), pltpu.SemaphoreType.DMA(())])
    return pl.pallas_call(
        functools.partial(_dispatch_kernel, rows=rows),
        grid_spec=grid_spec,
        out_shape=jax.ShapeDtypeStruct((n_slots * SLAB, LANES), h.dtype),
        compiler_params=_cparams("arbitrary"),
        name="moe_dispatch",
    )(pad_lo, pad_hi, pos_rows, h)


def _expert_kernel(te_ref, tv_ref, x_ref, w1_ref, b1_ref, w2_ref, b2_ref, o_ref, w1_bf, w2_bf, *, tm):
    i = pl.program_id(0)
    ff = w2_ref.shape[1]
    new_expert = jnp.logical_or(i == 0, te_ref[i] != te_ref[jnp.maximum(i - 1, 0)])

    @pl.when(jnp.logical_and(tv_ref[i] > 0, new_expert))
    def _():
        w1_bf[...] = w1_ref[0].astype(BF16)
        w2_bf[...] = w2_ref[0].astype(BF16)

    @pl.when(tv_ref[i] > 0)
    def _():
        x = _load_slab(x_ref, 0, tm).astype(BF16)
        hid = _dot(x, w1_bf[...]) + b1_ref[0]
        gate = jnp.minimum(hid[:, :ff], SWIGLU_LIMIT)
        up = jnp.clip(hid[:, ff:], -SWIGLU_LIMIT, SWIGLU_LIMIT)
        act = gate * _sigmoid(SWIGLU_ALPHA * gate) * (up + 1.0)
        _store_slab(o_ref, _dot(act.astype(BF16), w2_bf[...]) + b2_ref[0])

    @pl.when(tv_ref[i] == 0)
    def _():
        o_ref[...] = jnp.zeros_like(o_ref)


def expert_ffn(xs, tile_expert, tile_valid, w1, b1, w2, b2, tm, layer):
    n_slots = xs.shape[0] // SLAB
    _, ne, d, ff2 = w1.shape
    ff = w2.shape[2]
    slab = lambda: pl.BlockSpec((tm * SLAB, LANES), lambda i, te, tv: (i, 0))
    grid_spec = pltpu.PrefetchScalarGridSpec(
        num_scalar_prefetch=2,
        grid=(n_slots // tm,),
        in_specs=[pl.BlockSpec((tm * SLAB, LANES), lambda i, te, tv: (i * tv[i], 0)),
                  pl.BlockSpec((None, 1, d, ff2), lambda i, te, tv: (layer, te[i], 0, 0)),
                  pl.BlockSpec((1, 1, ff2), lambda i, te, tv: (te[i], 0, 0)),
                  pl.BlockSpec((None, 1, ff, d), lambda i, te, tv: (layer, te[i], 0, 0)),
                  pl.BlockSpec((1, 1, d), lambda i, te, tv: (te[i], 0, 0))],
        out_specs=slab(),
        scratch_shapes=[pltpu.VMEM((d, ff2), BF16), pltpu.VMEM((ff, d), BF16)])
    return pl.pallas_call(
        functools.partial(_expert_kernel, tm=tm),
        grid_spec=grid_spec,
        out_shape=jax.ShapeDtypeStruct((n_slots * SLAB, LANES), F32),
        compiler_params=_cparams("arbitrary"),
        name="moe_experts",
    )(tile_expert, tile_valid, xs, w1, b1.reshape(ne, 1, ff2), w2, b2.reshape(ne, 1, d))


def _combine_kernel(pos_ref, nxt_ref, x_ref, gt_ref, gates_ref, fin_ref, ys_ref, o_ref, buf, sems, *, tc, final):
    i = pl.program_id(0)
    n = tc * TOP_K
    slot = i % 2

    def copy(idx_ref, r, s):
        src = ys_ref.at[pl.ds(pl.multiple_of(idx_ref[r], SLAB), SLAB)]
        return pltpu.make_async_copy(src, buf.at[s, pl.ds(pl.multiple_of(r * SLAB, SLAB), SLAB)], sems.at[s])

    def issue(idx_ref, s):
        def body(r2, carry):
            for j in range(2):
                copy(idx_ref, 2 * r2 + j, s).start(priority=j)
            return carry
        lax.fori_loop(0, n // 2, body, 0, unroll=4)

    @pl.when(i == 0)
    def _():
        issue(pos_ref, 0)

    @pl.when(i + 1 < pl.num_programs(0))
    def _():
        issue(nxt_ref, 1 - slot)

    def drain(r, carry):
        copy(pos_ref, r, slot).wait()
        return carry
    lax.fori_loop(0, n, drain, 0, unroll=8)
    gates = gates_ref[...]
    y = jnp.zeros((tc, SLAB * LANES), F32)
    for k in range(TOP_K):
        y = y + gates[:, k:k + 1] * _load_slab(buf.at[slot], k * tc, tc)
    out = x_ref[...] + gt_ref[0] * y
    if final:
        ms = jnp.mean(out * out, axis=-1, keepdims=True)
        out = out * lax.rsqrt(ms + NORM_EPS) * fin_ref[...]
    o_ref[...] = out


def combine_rows(x, gate_vec, gates, pos, ys, final_g, final):
    b, l, d = x.shape
    t = b * l
    tc = min(l, 256)
    steps = t // tc
    per_seq = l // tc
    n = tc * TOP_K
    pos_t = _tiled_rows(pos, tc)
    gates_p = gates
    out = pl.pallas_call(
        functools.partial(_combine_kernel, tc=tc, final=final),
        grid=(steps,),
        in_specs=[pl.BlockSpec((n,), lambda i: (i,), memory_space=pltpu.SMEM),
                  pl.BlockSpec((n,), lambda i: (jnp.minimum(i + 1, steps - 1),), memory_space=pltpu.SMEM),
                  pl.BlockSpec((tc, d), lambda i: (i, 0)),
                  pl.BlockSpec((1, 1, d), lambda i: (i // per_seq, 0, 0)),
                  pl.BlockSpec((tc, LANES), lambda i: (i, 0)),
                  pl.BlockSpec((1, d), lambda i: (0, 0)),
                  pl.BlockSpec(memory_space=pl.ANY)],
        out_specs=pl.BlockSpec((tc, d), lambda i: (i, 0)),
        out_shape=jax.ShapeDtypeStruct((t, d), F32),
        scratch_shapes=[pltpu.VMEM((2, n * SLAB, LANES), F32), pltpu.SemaphoreType.DMA((2,))],
        compiler_params=_cparams("arbitrary"),
        name="moe_combine",
    )(pos_t, pos_t, x.reshape(t, d), gate_vec, gates_p, final_g[None, :], ys)
    return out.reshape(b, l, d)


def routing_plan(e, tm):
    k, t = e.shape
    n = k * t
    ef = e.reshape(n)
    onehot = (ef[:, None] == jnp.arange(N_EXPERTS, dtype=ef.dtype)[None, :]).astype(jnp.int32)
    csum = jnp.cumsum(onehot, axis=0)
    counts = csum[-1]
    rank = jnp.take_along_axis(csum, ef[:, None], axis=1)[:, 0] - 1
    padded = ((counts + tm - 1) // tm) * tm
    ends = jnp.cumsum(padded)
    starts = ends - padded
    pos = starts[ef] + rank
    n_slots = -(-(n + N_EXPERTS * (tm - 1)) // tm) * tm
    tile_start = jnp.arange(n_slots // tm, dtype=jnp.int32) * tm
    tile_expert = jnp.minimum(jnp.sum((tile_start[:, None] >= ends[None, :]).astype(jnp.int32), axis=1), N_EXPERTS - 1)
    tile_valid = (tile_start < ends[-1]).astype(jnp.int32)
    return (pos.reshape(k, t).astype(jnp.int32), (starts + counts).astype(jnp.int32), ends.astype(jnp.int32),
            n_slots, tile_expert.astype(jnp.int32), tile_valid)


def _tiled_rows(pos, tile):
    k, t = pos.shape
    return jnp.transpose(pos.reshape(k, t // tile, tile), (1, 0, 2)).reshape(k * t) * SLAB


MOE_TILE = 512


def moe_block(x, shift, scale, gate_vec, g, router_w, router_b, w1, b1, w2, b2, layer, final_g, final):
    b, l, d = x.shape
    t = b * l
    h, e, gates = router(x, shift, scale, g, router_w, router_b)
    pos, pad_lo, pad_hi, n_slots, tile_expert, tile_valid = routing_plan(e, MOE_TILE)
    xs = dispatch_rows(h, pos, pad_lo, pad_hi, n_slots)
    ys = expert_ffn(xs, tile_expert, tile_valid, w1, b1, w2, b2, MOE_TILE, layer)
    return combine_rows(x, gate_vec, gates.reshape(t, LANES), pos, ys, final_g, final)


def _diff_lambda(lp, lam_init):
    lp = lp.astype(F32)
    return jnp.exp(jnp.sum(lp[0] * lp[1])) - jnp.exp(jnp.sum(lp[2] * lp[3])) + lam_init


def kernel(x, c, ctx, c_ctx, norm1, norm2, w_mod, b_mod, w_in, w_out, attn_lambda, attn_subln,
           hyena_short_w, hyena_short_b, hyena_w1, hyena_b1, hyena_w2, hyena_b2, hyena_w3, hyena_freq,
           hyena_bias, pool_w, pool_scale, conv_dw_w, conv_dw_b, conv_ln_g, conv_ln_b, conv_pw_w,
           conv_pw_b, router_w, router_b, moe_w1, moe_b1, moe_w2, moe_b2, final_norm):
    bsz, n_lat, d = x.shape
    n_ctx = ctx.shape[1]
    depth = w_mod.shape[0]
    g = GROUP
    off_hy, off_pool, off_conv = 3 * g, 6 * g, 7 * g

    rows = -(-(bsz + 1) // SUBLANES) * SUBLANES
    cvec = jnp.zeros((rows, d), F32).at[:bsz].set(c).at[bsz].set(c_ctx)
    mods = adaln_all(cvec, w_mod, b_mod)

    cos_t, sin_t, partner = rope_tables(n_lat)
    ones_t = jnp.ones((n_ctx, g), F32)
    zeros_t = jnp.zeros((n_ctx, g), F32)
    qk_scale = HEAD_QK ** -0.5 * math.log2(math.e)

    xc = ctx
    for i in range(depth):
        last = i == depth - 1
        m = mods[i]
        lat = [m[:bsz, None, k * d:(k + 1) * d] for k in range(6)]
        cm = [jnp.broadcast_to(m[bsz, k * d:(k + 1) * d][None, None, :], (bsz, 1, d)) for k in range(6)]
        lam_init = 0.8 - 0.6 * math.exp(-0.3 * i)
        lam = _diff_lambda(attn_lambda[i], lam_init)
        wi = w_in[i]
        g1 = norm1[i][None, :]
        g2 = norm2[i][None, :]

        w_lat = jnp.concatenate([wi, wi[:, partner], wi[:, g + partner]], axis=1).astype(BF16)
        nin = wi.shape[1]
        segs = [(0, g, nin, qk_scale), (g, g, nin + g, 1.0), (2 * g, g, None, 1.0),
                (off_hy, 3 * g, None, 1.0), (off_pool, g, None, 1.0), (off_conv, 2 * g, None, 1.0)]
        q, k, v, p_hy, p_pool, p_conv = modulated_projection(
            x, lat[0], lat[1], g1, w_lat, segs, [BF16, BF16, BF16, F32, F32, F32], cos_t, sin_t)

        if last:
            w_ctx = wi[:, g:3 * g].astype(BF16)
            kc, vc = modulated_projection(xc, cm[0], cm[1], g1, w_ctx,
                                          [(0, g, None, 1.0), (g, g, None, 1.0)], [BF16, BF16], ones_t, zeros_t)
        else:
            csegs = [(0, g, None, qk_scale), (g, g, None, 1.0), (2 * g, g, None, 1.0),
                     (off_hy, 3 * g, None, 1.0), (off_pool, g, None, 1.0), (off_conv, 2 * g, None, 1.0)]
            qc, kc, vc, pc_hy, pc_pool, pc_conv = modulated_projection(
                xc, cm[0], cm[1], g1, wi.astype(BF16), csegs, [BF16, BF16, BF16, F32, F32, F32], ones_t, zeros_t)

        kt_all = jnp.transpose(jnp.concatenate([kc, k], axis=1), (0, 2, 1))
        v_all = jnp.concatenate([vc, v], axis=1)
        o_attn = diff_attention(q, kt_all, v_all, lam, attn_subln[i], 1.0 - lam_init)

        filt_params = (hyena_w1[i], hyena_b1[i], hyena_w2[i], hyena_b2[i], hyena_w3[i], hyena_freq[i])

        def local_groups(p_hy_, p_pool_, p_conv_, n_tokens):
            g_rows = filter_rows(hyena_filters(n_tokens, *filt_params), n_tokens)
            u = hyena_short_conv(p_hy_, hyena_short_w[i], hyena_short_b[i])
            o_hy = hyena_long_conv(u, g_rows, hyena_bias[i])
            o_pool = pool_mixer(p_pool_, pool_w[i], pool_scale[i])
            o_conv = conformer_mixer(p_conv_, conv_dw_w[i], conv_dw_b[i], conv_ln_g[i], conv_ln_b[i],
                                     conv_pw_w[i], conv_pw_b[i])
            return o_hy, o_pool, o_conv

        o_hy, o_pool, o_conv = local_groups(p_hy, p_pool, p_conv, n_lat)
        x = out_projection(x, lat[2], (o_attn, o_hy, o_pool, o_conv), w_out[i])
        if not last:
            oc_attn = diff_attention(qc, jnp.transpose(kc, (0, 2, 1)), vc, lam, attn_subln[i], 1.0 - lam_init)
            oc_hy, oc_pool, oc_conv = local_groups(pc_hy, pc_pool, pc_conv, n_ctx)
            xc = out_projection(xc, cm[2], (oc_attn, oc_hy, oc_pool, oc_conv), w_out[i])

        x = moe_block(x, lat[3], lat[4], lat[5], g2, router_w[i], router_b[i], moe_w1, moe_b1[i], moe_w2, moe_b2[i],
                      i, final_norm, last)
        if not last:
            xc = moe_block(xc, cm[3], cm[4], cm[5], g2, router_w[i], router_b[i], moe_w1, moe_b1[i], moe_w2,
                           moe_b2[i], i, final_norm, False)
    return x
```

```python
import functools
import math

import jax
import jax.numpy as jnp
from jax import lax
from jax.experimental import pallas as pl
from jax.experimental.pallas import tpu as pltpu

F32 = jnp.float32
BF16 = jnp.bfloat16

GRID_W = 64
N_HEADS = 4
HEAD_V = 64
HEAD_QK = 32
GROUP = 256
ROPE_BASE = 10000.0
SUBLN_EPS = 1e-5
NORM_EPS = 1e-6
LN_EPS = 1e-5
POOL_WINDOWS = (2, 4, 8, 16)
CONV_KERNEL = 31
HYENA_SHORT = 3
HYENA_EMB = 33
HYENA_TARGET = 1e-2
HYENA_FAST_PCT = 0.3
HYENA_SLOW_PCT = 1.5
N_EXPERTS = 32
TOP_K = 4
SWIGLU_ALPHA = 1.702
SWIGLU_LIMIT = 7.0

LANES = 128
SUBLANES = 8
VMEM_LIMIT = 56 * 1024 * 1024
HALO = 16


def _cparams(*sem):
    return pltpu.CompilerParams(dimension_semantics=sem, vmem_limit_bytes=VMEM_LIMIT)


def _split_bf16(a):
    hi = a.astype(BF16)
    lo = (a - hi.astype(F32)).astype(BF16)
    return hi, lo


def _dot(a, b):
    return jnp.dot(a, b, preferred_element_type=F32)


def _dot3(a, b):
    ah, al = _split_bf16(a)
    bh, bl = _split_bf16(b)
    return _dot(ah, bh) + (_dot(ah, bl) + _dot(al, bh))


def _sigmoid(x):
    return 1.0 / (1.0 + jnp.exp(-x))


def _adaln_kernel(c_ref, w_ref, b_ref, o_ref):
    cv = c_ref[...]
    s = cv * _sigmoid(cv)
    o_ref[0] = _dot3(s, w_ref[0]) + b_ref[0]


def adaln_all(cvec, w_mod, b_mod):
    depth, d, n = w_mod.shape
    r = cvec.shape[0]
    tn = 1536
    return pl.pallas_call(
        _adaln_kernel,
        grid=(depth, n // tn),
        in_specs=[pl.BlockSpec((r, d), lambda i, j: (0, 0)),
                  pl.BlockSpec((1, d, tn), lambda i, j: (i, 0, j)),
                  pl.BlockSpec((1, 1, tn), lambda i, j: (i, 0, j))],
        out_specs=pl.BlockSpec((1, r, tn), lambda i, j: (i, 0, j)),
        out_shape=jax.ShapeDtypeStruct((depth, r, n), F32),
        compiler_params=_cparams("arbitrary", "arbitrary"),
        name="adaln",
    )(cvec, w_mod, b_mod.reshape(depth, 1, n))


def _filter_kernel(feat_ref, w1_ref, b1_ref, w2_ref, b2_ref, w3_ref, fr_ref, dec_ref, o_ref):
    fr = fr_ref[...]
    z = jnp.sin(fr * (_dot3(feat_ref[...], w1_ref[...]) + b1_ref[...]))
    z = jnp.sin(fr * (_dot3(z, w2_ref[...]) + b2_ref[...]))
    o_ref[...] = _dot3(z, w3_ref[...]) * dec_ref[...]


def hyena_filters(n_tokens, w1, b1, w2, b2, w3, freq):
    bands = (HYENA_EMB - 1) // 2
    t = jnp.linspace(0.0, 1.0, n_tokens, dtype=F32)[:, None]
    w = 2.0 * math.pi * jnp.arange(n_tokens, dtype=F32)[:, None] / n_tokens
    f = jnp.linspace(1e-4, bands - 1, bands, dtype=F32)[None, :]
    feat = jnp.concatenate([t, jnp.cos(f * w), -jnp.sin(f * w)], axis=-1)
    kpad = 40
    feat = jnp.pad(feat, ((0, 0), (0, kpad - HYENA_EMB)))
    w1p = jnp.pad(w1, ((0, kpad - HYENA_EMB), (0, 0)))
    min_decay = math.log(HYENA_TARGET) / HYENA_SLOW_PCT
    max_decay = math.log(HYENA_TARGET) / HYENA_FAST_PCT
    deltas = jnp.abs(jnp.linspace(min_decay, max_decay, GROUP, dtype=F32))
    decay = jnp.exp(-t * deltas)
    nf = w3.shape[1]
    decay = jnp.tile(decay, (1, nf // GROUP))
    ffn = w2.shape[0]
    tl = min(n_tokens, 512)
    full = lambda shape: pl.BlockSpec(shape, lambda i: (0, 0))
    return pl.pallas_call(
        _filter_kernel,
        grid=(n_tokens // tl,),
        in_specs=[pl.BlockSpec((tl, kpad), lambda i: (i, 0)),
                  full((kpad, ffn)), full((1, ffn)), full((ffn, ffn)), full((1, ffn)),
                  full((ffn, nf)), full((1, ffn)),
                  pl.BlockSpec((tl, nf), lambda i: (i, 0))],
        out_specs=pl.BlockSpec((tl, nf), lambda i: (i, 0)),
        out_shape=jax.ShapeDtypeStruct((n_tokens, nf), F32),
        compiler_params=_cparams("arbitrary"),
        name="hyena_filter",
    )(feat, w1p, b1[None, :], w2, b2[None, :], w3, freq[None, :], decay)


def filter_rows(filt, n_tokens):
    order = filt.shape[1] // (2 * GROUP)
    f4 = filt.reshape(n_tokens, order, 2, GROUP)
    hf = f4[:, :, 0]
    hb = f4[:, :, 1]
    rows = jnp.concatenate([jnp.zeros_like(hf[:1]), hb[:0:-1], hf], axis=0)
    return jnp.transpose(rows, (1, 2, 0))[:, :, None, :]


def _proj_kernel(x_ref, sh_ref, sc_ref, g_ref, w_ref, cos_ref, sin_ref, *out_refs, segs):
    x = x_ref[0]
    ms = jnp.mean(x * x, axis=-1, keepdims=True)
    h = (x * lax.rsqrt(ms + NORM_EPS)) * g_ref[...] * (1.0 + sc_ref[0]) + sh_ref[0]
    hb = h.astype(BF16)
    for o_ref, (start, width, swap_start, scale) in zip(out_refs, segs):
        p = _dot(hb, w_ref[:, start:start + width])
        if swap_start is not None:
            ps = _dot(hb, w_ref[:, swap_start:swap_start + width])
            p = p * cos_ref[...] + ps * sin_ref[...]
        if scale != 1.0:
            p = p * scale
        o_ref[0] = p.astype(o_ref.dtype)


def modulated_projection(x, shift, scale, g, w, segs, out_dtypes, cos_t, sin_t):
    b, l, d = x.shape
    tm = min(l, 512)
    nw = w.shape[1]
    rw = cos_t.shape[1]
    outs = [jax.ShapeDtypeStruct((b, l, s[1]), dt) for s, dt in zip(segs, out_dtypes)]
    return pl.pallas_call(
        functools.partial(_proj_kernel, segs=tuple(segs)),
        grid=(b, l // tm),
        in_specs=[pl.BlockSpec((1, tm, d), lambda i, j: (i, j, 0)),
                  pl.BlockSpec((1, 1, d), lambda i, j: (i, 0, 0)),
                  pl.BlockSpec((1, 1, d), lambda i, j: (i, 0, 0)),
                  pl.BlockSpec((1, d), lambda i, j: (0, 0)),
                  pl.BlockSpec((d, nw), lambda i, j: (0, 0)),
                  pl.BlockSpec((tm, rw), lambda i, j: (j, 0)),
                  pl.BlockSpec((tm, rw), lambda i, j: (j, 0))],
        out_specs=[pl.BlockSpec((1, tm, s[1]), lambda i, j: (i, j, 0)) for s in segs],
        out_shape=outs,
        compiler_params=_cparams("arbitrary", "arbitrary"),
        name="mod_proj",
    )(x, shift, scale, g, w, cos_t, sin_t)


def rope_tables(n_tokens):
    rows = n_tokens // GRID_W
    row = jnp.repeat(jnp.arange(rows, dtype=F32), GRID_W)
    col = jnp.tile(jnp.arange(GRID_W, dtype=F32), rows)
    n_freq = HEAD_QK // 4
    inv = ROPE_BASE ** (-jnp.arange(n_freq, dtype=F32) / n_freq)
    lane = jnp.arange(GROUP)
    d = lane % HEAD_QK
    pos = jnp.where((d < 2 * n_freq)[None, :], row[:, None], col[:, None])
    ang = pos * inv[d % n_freq][None, :]
    first = (d % (2 * n_freq)) < n_freq
    cos_t = jnp.cos(ang)
    sin_t = jnp.where(first[None, :], -jnp.sin(ang), jnp.sin(ang))
    partner = jnp.where(first, lane + n_freq, lane - n_freq)
    return cos_t, sin_t, partner


ATTN_ROWS = 16


def _attn_kernel(lam_ref, q_ref, kt_ref, v_ref, g_ref, o_ref, s_even, s_odd, a_even, a_odd, *, out_scale):
    q = q_ref[0]
    lam = lam_ref[0]
    tq = q.shape[0]
    lane = lax.broadcasted_iota(jnp.int32, (1, GROUP), 1)
    map_of_lane = lane // HEAD_QK
    head_of_lane = lane // HEAD_V
    s_bufs = (s_even, s_odd)
    a_bufs = (a_even, a_odd)

    def scores(h):
        for m in range(2):
            keep = jnp.where(map_of_lane == 2 * h + m, 1.0, 0.0).astype(BF16)
            s_bufs[h % 2][m] = _dot(q * keep, kt_ref[0])

    def softmax(h):
        s_ref, a_ref = s_bufs[h % 2], a_bufs[h % 2]
        inv = []
        for r in range(0, tq, ATTN_ROWS):
            s1 = s_ref[1, r:r + ATTN_ROWS, :]
            e1 = jnp.exp2(s1 - jnp.max(s1, axis=-1, keepdims=True))
            sum1 = jnp.sum(e1, axis=-1, keepdims=True)
            s0 = s_ref[0, r:r + ATTN_ROWS, :]
            e0 = jnp.exp2(s0 - jnp.max(s0, axis=-1, keepdims=True))
            sum0 = jnp.sum(e0, axis=-1, keepdims=True)
            a_ref[0, r:r + ATTN_ROWS, :] = (e0 - e1 * (lam * sum0 / sum1)).astype(BF16)
            inv.append(1.0 / sum0)
        return jnp.concatenate(inv, axis=0)

    acc = jnp.zeros((tq, GROUP), F32)
    scores(0)
    for h in range(N_HEADS):
        if h + 1 < N_HEADS:
            scores(h + 1)
        inv0 = softmax(h)
        o = _dot(a_bufs[h % 2][0], v_ref[0]) * inv0
        acc = jnp.where(head_of_lane == h, o, acc)
    sq = acc * acc
    ms = jnp.zeros((tq, GROUP), F32)
    for h in range(N_HEADS):
        hsel = head_of_lane == h
        mh = jnp.sum(jnp.where(hsel, sq, 0.0), axis=-1, keepdims=True) * (1.0 / HEAD_V)
        ms = jnp.where(hsel, mh, ms)
    y = acc * lax.rsqrt(ms + SUBLN_EPS) * g_ref[...] * out_scale
    o_ref[0] = y.astype(o_ref.dtype)


def diff_attention(q, kt, v, lam, subln, out_scale):
    b, nq, _ = q.shape
    nk = v.shape[1]
    tq = min(nq, 512)
    g = jnp.tile(subln, N_HEADS)[None, :]
    return pl.pallas_call(
        functools.partial(_attn_kernel, out_scale=out_scale),
        grid=(b, nq // tq),
        in_specs=[pl.BlockSpec(memory_space=pltpu.SMEM),
                  pl.BlockSpec((1, tq, GROUP), lambda i, j: (i, j, 0)),
                  pl.BlockSpec((1, GROUP, nk), lambda i, j: (i, 0, 0)),
                  pl.BlockSpec((1, nk, GROUP), lambda i, j: (i, 0, 0)),
                  pl.BlockSpec((1, GROUP), lambda i, j: (0, 0))],
        out_specs=pl.BlockSpec((1, tq, GROUP), lambda i, j: (i, j, 0)),
        out_shape=jax.ShapeDtypeStruct((b, nq, GROUP), BF16),
        scratch_shapes=[pltpu.VMEM((2, tq, nk), F32), pltpu.VMEM((2, tq, nk), F32),
                        pltpu.VMEM((1, tq, nk), BF16), pltpu.VMEM((1, tq, nk), BF16)],
        compiler_params=_cparams("arbitrary", "arbitrary"),
        name="diff_attn",
    )(lam.reshape(1), q, kt, v, g)


def _fill_padded(pad_ref, n, write_rows):
    zeros = jnp.zeros((HALO, pad_ref.shape[1]), F32)
    pad_ref[0:HALO, :] = zeros
    pad_ref[HALO + n:HALO + n + HALO, :] = zeros
    write_rows()


def _shifted_rows(pad_ref, r0, ch, offsets):
    rows = ch + 2 * HALO
    blk = pad_ref[pl.ds(r0, rows), :]
    rolled = {0: blk}
    out = {}
    for k in offsets:
        o = HALO + k
        r = o % SUBLANES
        if r not in rolled:
            rolled[r] = pltpu.roll(blk, rows - r, 0)
        out[k] = rolled[r][o - r:o - r + ch]
    return out


def _pool_kernel(u_ref, w_ref, sc_ref, o_ref, pad_ref, *, n, ch):
    def copy(i, c):
        r0 = pl.multiple_of(i * ch, ch)
        pad_ref[pl.ds(HALO + r0, ch), :] = u_ref[0, pl.ds(r0, ch), :]
        return c
    _fill_padded(pad_ref, n, lambda: lax.fori_loop(0, n // ch, copy, 0))
    lane = lax.broadcasted_iota(jnp.int32, (1, GROUP), 1)
    pg = GROUP // len(POOL_WINDOWS)
    reach = POOL_WINDOWS[-1] // 2

    def body(i, c):
        r0 = pl.multiple_of(i * ch, ch)
        taps = _shifted_rows(pad_ref, r0, ch, range(-reach, reach))
        ld = lambda k: taps[k]
        t = r0 + lax.broadcasted_iota(jnp.int32, (ch, 1), 0)
        u = ld(0)
        sums = []
        s = u + ld(-1)
        sums.append(s)
        half = 1
        for _ in POOL_WINDOWS[1:]:
            for k in range(half, 2 * half):
                s = s + ld(k) + ld(-k - 1)
            half *= 2
            sums.append(s)
        pooled = None
        for gi, win in reversed(list(enumerate(POOL_WINDOWS))):
            hi = jnp.minimum(t + (win - win // 2), n)
            lo = jnp.maximum(t - win // 2, 0)
            mean = sums[gi] / (hi - lo).astype(F32)
            pooled = mean if pooled is None else jnp.where(lane < (gi + 1) * pg, mean, pooled)
        pooled = pooled - u
        y = _dot(pooled.astype(BF16), w_ref[...]) * sc_ref[...]
        o_ref[0, pl.ds(r0, ch), :] = y.astype(o_ref.dtype)
        return c
    lax.fori_loop(0, n // ch, body, 0)


def pool_mixer(u, w_pool, scale):
    b, n, c = u.shape
    ng, pg, _ = w_pool.shape
    wbd = jnp.zeros((c, c), F32)
    for gi in range(ng):
        wbd = wbd.at[gi * pg:(gi + 1) * pg, gi * pg:(gi + 1) * pg].set(w_pool[gi])
    ch = min(n, 256)
    return pl.pallas_call(
        functools.partial(_pool_kernel, n=n, ch=ch),
        grid=(b,),
        in_specs=[pl.BlockSpec((1, n, c), lambda i: (i, 0, 0)),
                  pl.BlockSpec((c, c), lambda i: (0, 0)),
                  pl.BlockSpec((1, c), lambda i: (0, 0))],
        out_specs=pl.BlockSpec((1, n, c), lambda i: (i, 0, 0)),
        out_shape=jax.ShapeDtypeStruct((b, n, c), BF16),
        scratch_shapes=[pltpu.VMEM((n + 2 * HALO, c), F32)],
        compiler_params=_cparams("arbitrary"),
        name="pool_mixer",
    )(u, wbd.astype(BF16), scale[None, :])


def _conformer_kernel(p_ref, dw_ref, dwb_ref, lg_ref, lb_ref, pw_ref, pwb_ref, o_ref, pad_ref, *, n, ch):
    c = GROUP

    def glu(i, carry):
        r0 = pl.multiple_of(i * ch, ch)
        a = p_ref[0, pl.ds(r0, ch), 0:c]
        g = p_ref[0, pl.ds(r0, ch), c:2 * c]
        pad_ref[pl.ds(HALO + r0, ch), :] = a * _sigmoid(g)
        return carry
    _fill_padded(pad_ref, n, lambda: lax.fori_loop(0, n // ch, glu, 0))
    half = (CONV_KERNEL - 1) // 2

    def body(i, carry):
        r0 = pl.multiple_of(i * ch, ch)
        acc = jnp.zeros((ch, c), F32) + dwb_ref[...]
        taps = _shifted_rows(pad_ref, r0, ch, range(-half, half + 1))
        for k in range(CONV_KERNEL):
            acc = acc + taps[k - half] * dw_ref[k:k + 1, :]
        mu = jnp.mean(acc, axis=-1, keepdims=True)
        xc = acc - mu
        var = jnp.mean(xc * xc, axis=-1, keepdims=True)
        y = xc * lax.rsqrt(var + LN_EPS) * lg_ref[...] + lb_ref[...]
        y = y * _sigmoid(y)
        out = _dot(y.astype(BF16), pw_ref[...]) + pwb_ref[...]
        o_ref[0, pl.ds(r0, ch), :] = out.astype(o_ref.dtype)
        return carry
    lax.fori_loop(0, n // ch, body, 0)


def conformer_mixer(p, dw_w, dw_b, ln_g, ln_b, pw_w, pw_b):
    b, n, c2 = p.shape
    c = c2 // 2
    ch = min(n, 128)
    kp = 32
    dw = jnp.pad(dw_w, ((0, kp - CONV_KERNEL), (0, 0)))
    vec = lambda: pl.BlockSpec((1, c), lambda i: (0, 0))
    return pl.pallas_call(
        functools.partial(_conformer_kernel, n=n, ch=ch),
        grid=(b,),
        in_specs=[pl.BlockSpec((1, n, c2), lambda i: (i, 0, 0)),
                  pl.BlockSpec((kp, c), lambda i: (0, 0)),
                  vec(), vec(), vec(),
                  pl.BlockSpec((c, c), lambda i: (0, 0)),
                  vec()],
        out_specs=pl.BlockSpec((1, n, c), lambda i: (i, 0, 0)),
        out_shape=jax.ShapeDtypeStruct((b, n, c), BF16),
        scratch_shapes=[pltpu.VMEM((n + 2 * HALO, c), F32)],
        compiler_params=_cparams("arbitrary"),
        name="conformer_mixer",
    )(p, dw, dw_b[None, :], ln_g[None, :], ln_b[None, :], pw_w.astype(BF16), pw_b[None, :])


def _short_conv_kernel(p_ref, w_ref, b_ref, o_ref, pad_ref, *, n, ch):
    def copy(i, carry):
        r0 = pl.multiple_of(i * ch, ch)
        pad_ref[pl.ds(HALO + r0, ch), :] = p_ref[0, pl.ds(r0, ch), :]
        return carry
    _fill_padded(pad_ref, n, lambda: lax.fori_loop(0, n // ch, copy, 0))
    half = (HYENA_SHORT - 1) // 2

    def body(i, carry):
        r0 = pl.multiple_of(i * ch, ch)
        acc = jnp.zeros((ch, pad_ref.shape[1]), F32) + b_ref[...]
        taps = _shifted_rows(pad_ref, r0, ch, range(-half, half + 1))
        for k in range(HYENA_SHORT):
            acc = acc + taps[k - half] * w_ref[k:k + 1, :]
        o_ref[0, pl.ds(r0, ch), :] = acc.astype(o_ref.dtype)
        return carry
    lax.fori_loop(0, n // ch, body, 0)


def hyena_short_conv(p, w, bias):
    b, n, c = p.shape
    ch = min(n, 128)
    wp = jnp.pad(w, ((0, SUBLANES - HYENA_SHORT), (0, 0)))
    return pl.pallas_call(
        functools.partial(_short_conv_kernel, n=n, ch=ch),
        grid=(b,),
        in_specs=[pl.BlockSpec((1, n, c), lambda i: (i, 0, 0)),
                  pl.BlockSpec((SUBLANES, c), lambda i: (0, 0)),
                  pl.BlockSpec((1, c), lambda i: (0, 0))],
        out_specs=pl.BlockSpec((1, n, c), lambda i: (i, 0, 0)),
        out_shape=jax.ShapeDtypeStruct((b, n, c), BF16),
        scratch_shapes=[pltpu.VMEM((n + 2 * HALO, c), F32)],
        compiler_params=_cparams("arbitrary"),
        name="hyena_short_conv",
    )(p, wp, bias[None, :])


def _toeplitz_conv(g_ref, u, nb, tb, bsz):
    ys = [jnp.zeros((bsz, tb), F32) for _ in range(nb)]
    for d in range(-(nb - 1), nb):
        start = tb * (nb - 1 + d)
        win = jnp.broadcast_to(g_ref[:, start:start + 2 * tb], (tb, 2 * tb))
        rolled = pltpu.roll(win, 0, 1, stride=1, stride_axis=0)
        blk = rolled[:, tb:2 * tb].astype(BF16)
        j_lo, j_hi = max(0, -d), min(nb, nb - d)
        o = _dot(u[j_lo * bsz:j_hi * bsz], blk)
        for j in range(j_lo, j_hi):
            ys[j + d] = ys[j + d] + o[(j - j_lo) * bsz:(j - j_lo + 1) * bsz]
    return ys


def _hyena_kernel(bias_ref, g_ref, v_ref, x1_ref, x2_ref, o_ref, *, nb, tb, bsz, cb):
    base = pl.program_id(0) * cb

    def body(ci, carry):
        v = v_ref[ci]
        y1 = jnp.concatenate(_toeplitz_conv(g_ref.at[0, ci], v, nb, tb, bsz), axis=0)
        z = x1_ref[ci].astype(F32) * (y1 + bias_ref[0, base + ci] * v.astype(F32))
        y2 = jnp.concatenate(_toeplitz_conv(g_ref.at[1, ci], z.astype(BF16), nb, tb, bsz), axis=0)
        out = x2_ref[ci].astype(F32) * (y2 + bias_ref[1, base + ci] * z)
        o_ref[ci] = out.astype(o_ref.dtype)
        return carry
    lax.fori_loop(0, cb, body, 0, unroll=2)


def hyena_long_conv(u, g_rows, bias):
    b, n, _ = u.shape
    tb = min(n, 256)
    nb = n // tb
    nbb = nb * b
    ut = jnp.transpose(u.reshape(b, nb, tb, 3, GROUP), (3, 4, 1, 0, 2)).reshape(3, GROUP, nbb, tb)
    cb = 8
    blk = lambda part: pl.BlockSpec((None, cb, nbb, tb), lambda i, part=part: (part, i, 0, 0))
    out = pl.pallas_call(
        functools.partial(_hyena_kernel, nb=nb, tb=tb, bsz=b, cb=cb),
        grid=(GROUP // cb,),
        in_specs=[pl.BlockSpec(memory_space=pltpu.SMEM),
                  pl.BlockSpec((2, cb, 1, 2 * n), lambda i: (0, i, 0, 0)),
                  blk(0), blk(1), blk(2)],
        out_specs=pl.BlockSpec((cb, nbb, tb), lambda i: (i, 0, 0)),
        out_shape=jax.ShapeDtypeStruct((GROUP, nbb, tb), BF16),
        compiler_params=_cparams("arbitrary"),
        name="hyena_long_conv",
    )(bias, g_rows, ut, ut, ut)
    return jnp.transpose(out.reshape(GROUP, nb, b, tb), (2, 1, 3, 0)).reshape(b, n, GROUP)


def _out_proj_kernel(x_ref, gt_ref, a_ref, h_ref, p_ref, c_ref, w_ref, o_ref):
    acc = _dot(a_ref[0].astype(BF16), w_ref[0:GROUP, :])
    acc = acc + _dot(h_ref[0].astype(BF16), w_ref[GROUP:2 * GROUP, :])
    acc = acc + _dot(p_ref[0].astype(BF16), w_ref[2 * GROUP:3 * GROUP, :])
    acc = acc + _dot(c_ref[0].astype(BF16), w_ref[3 * GROUP:4 * GROUP, :])
    o_ref[0] = x_ref[0] + gt_ref[0] * acc


def out_projection(x, gate, parts, w_out):
    b, l, d = x.shape
    tm = min(l, 512)
    part = lambda: pl.BlockSpec((1, tm, GROUP), lambda i, j: (i, j, 0))
    return pl.pallas_call(
        _out_proj_kernel,
        grid=(b, l // tm),
        in_specs=[pl.BlockSpec((1, tm, d), lambda i, j: (i, j, 0)),
                  pl.BlockSpec((1, 1, d), lambda i, j: (i, 0, 0)),
                  part(), part(), part(), part(),
                  pl.BlockSpec(w_out.shape, lambda i, j: (0, 0))],
        out_specs=pl.BlockSpec((1, tm, d), lambda i, j: (i, j, 0)),
        out_shape=jax.ShapeDtypeStruct((b, l, d), F32),
        compiler_params=_cparams("arbitrary", "arbitrary"),
        name="out_proj",
    )(x, gate, *parts, w_out.astype(BF16))


SLAB = 8


def _store_slab(ref, val, start=0):
    rows = val.shape[0]
    for s in range(SLAB):
        ref[pl.ds(start * SLAB + s, rows, stride=SLAB), :] = val[:, s * LANES:(s + 1) * LANES]


def _load_slab(ref, start, rows):
    return jnp.concatenate([ref[pl.ds(start * SLAB + s, rows, stride=SLAB), :] for s in range(SLAB)], axis=1)


def _router_kernel(x_ref, sh_ref, sc_ref, g_ref, wh_ref, wl_ref, rb_ref, h_ref, idx_ref, gate_ref):
    x = x_ref[0]
    ms = jnp.mean(x * x, axis=-1, keepdims=True)
    h = (x * lax.rsqrt(ms + NORM_EPS)) * g_ref[...] * (1.0 + sc_ref[0]) + sh_ref[0]
    _store_slab(h_ref, h)
    hh, hl = _split_bf16(h)
    logits = _dot(hh, wh_ref[...]) + (_dot(hh, wl_ref[...]) + _dot(hl, wh_ref[...])) + rb_ref[...]
    tm = x.shape[0]
    lane = lax.broadcasted_iota(jnp.int32, (tm, N_EXPERTS), 1).astype(F32)
    out_lane = lax.broadcasted_iota(jnp.int32, (tm, LANES), 1)
    vals, idxs = [], []
    cur = logits
    for _ in range(TOP_K):
        mx = jnp.max(cur, axis=-1, keepdims=True)
        ix = jnp.min(jnp.where(cur == mx, lane, float(N_EXPERTS)), axis=-1, keepdims=True)
        vals.append(mx)
        idxs.append(ix)
        cur = jnp.where(lane == ix, -jnp.inf, cur)
    es = [jnp.exp(v - vals[0]) for v in vals]
    inv = 1.0 / (es[0] + es[1] + es[2] + es[3])
    idx_out = jnp.zeros((tm, LANES), F32)
    gate_out = jnp.zeros((tm, LANES), F32)
    for k in range(TOP_K):
        idx_out = jnp.where(out_lane == k, idxs[k], idx_out)
        gate_out = jnp.where(out_lane == k, es[k] * inv, gate_out)
    idx_ref[...] = idx_out.T[:SUBLANES].astype(jnp.int32)
    gate_ref[0] = gate_out


def router(x, shift, scale, g, router_w, router_b):
    b, l, d = x.shape
    assert d == SLAB * LANES
    tm = min(l, 512)
    per_seq = l // tm
    wh, wl = _split_bf16(router_w)
    vec3 = lambda: pl.BlockSpec((1, 1, d), lambda i, j: (i, 0, 0))
    tile = lambda w: pl.BlockSpec((1, tm, w), lambda i, j: (i, j, 0))
    h, idx, gates = pl.pallas_call(
        _router_kernel,
        grid=(b, per_seq),
        in_specs=[tile(d), vec3(), vec3(),
                  pl.BlockSpec((1, d), lambda i, j: (0, 0)),
                  pl.BlockSpec(wh.shape, lambda i, j: (0, 0)),
                  pl.BlockSpec(wl.shape, lambda i, j: (0, 0)),
                  pl.BlockSpec((1, N_EXPERTS), lambda i, j: (0, 0))],
        out_specs=[pl.BlockSpec((tm * SLAB, LANES), lambda i, j: (i * per_seq + j, 0)),
                   pl.BlockSpec((SUBLANES, tm), lambda i, j: (0, i * per_seq + j)), tile(LANES)],
        out_shape=[jax.ShapeDtypeStruct((b * l * SLAB, LANES), F32),
                   jax.ShapeDtypeStruct((SUBLANES, b * l), jnp.int32),
                   jax.ShapeDtypeStruct((b, l, LANES), F32)],
        compiler_params=_cparams("arbitrary", "arbitrary"),
        name="moe_router",
    )(x, shift, scale, g, wh, wl, router_b[None, :])
    return h, idx[:TOP_K], gates


DISPATCH_ROWS = 2048


def _dispatch_kernel(lo_ref, hi_ref, pos_ref, src_ref, dst_ref, zero_ref, sem, *, rows):
    i = pl.program_id(0)

    def copy(tok, k):
        src = src_ref.at[pl.ds(pl.multiple_of(tok * SLAB, SLAB), SLAB)]
        dst = dst_ref.at[pl.ds(pl.multiple_of(pos_ref[k * (rows // TOP_K) + tok], SLAB), SLAB)]
        return pltpu.make_async_copy(src, dst, sem)

    def start(tok, carry):
        for k in range(TOP_K):
            copy(tok, k).start(priority=k % 2)
        return carry
    lax.fori_loop(0, rows // TOP_K, start, 0, unroll=2)

    def wait(tok, carry):
        for k in range(TOP_K):
            copy(tok, k).wait()
        return carry
    lax.fori_loop(0, rows // TOP_K, wait, 0, unroll=2)

    @pl.when(i == pl.num_programs(0) - 1)
    def _():
        zero_ref[...] = jnp.zeros_like(zero_ref)

        def fill(r):
            return pltpu.make_async_copy(zero_ref, dst_ref.at[pl.ds(pl.multiple_of(r * SLAB, SLAB), SLAB)], sem)

        def per_expert(e, carry):
            def fill_start(r, c):
                fill(r).start()
                return c

            def fill_wait(r, c):
                fill(r).wait()
                return c
            lax.fori_loop(lo_ref[e], hi_ref[e], fill_start, 0)
            lax.fori_loop(lo_ref[e], hi_ref[e], fill_wait, 0)
            return carry
        lax.fori_loop(0, N_EXPERTS, per_expert, 0)


def dispatch_rows(h, pos, pad_lo, pad_hi, n_slots):
    rows = DISPATCH_ROWS
    tokens = rows // TOP_K
    pos_rows = _tiled_rows(pos, tokens)
    grid_spec = pltpu.PrefetchScalarGridSpec(
        num_scalar_prefetch=2,
        grid=(pos_rows.shape[0] // rows,),
        in_specs=[pl.BlockSpec((rows,), lambda i, lo, hi: (i,), memory_space=pltpu.SMEM),
                  pl.BlockSpec((tokens * SLAB, LANES), lambda i, lo, hi: (i, 0))],
        out_specs=pl.BlockSpec(memory_space=pl.ANY),
        scratch_shapes=[pltpu.VMEM((SLAB, LANES), h.dtype), pltpu.SemaphoreType.DMA(())])
    return pl.pallas_call(
        functools.partial(_dispatch_kernel, rows=rows),
        grid_spec=grid_spec,
        out_shape=jax.ShapeDtypeStruct((n_slots * SLAB, LANES), h.dtype),
        compiler_params=_cparams("arbitrary"),
        name="moe_dispatch",
    )(pad_lo, pad_hi, pos_rows, h)


def _expert_kernel(te_ref, tv_ref, x_ref, w1_ref, b1_ref, w2_ref, b2_ref, o_ref, w1_bf, w2_bf, *, tm):
    i = pl.program_id(0)
    ff = w2_ref.shape[1]
    new_expert = jnp.logical_or(i == 0, te_ref[i] != te_ref[jnp.maximum(i - 1, 0)])

    @pl.when(jnp.logical_and(tv_ref[i] > 0, new_expert))
    def _():
        w1_bf[...] = w1_ref[0].astype(BF16)
        w2_bf[...] = w2_ref[0].astype(BF16)

    @pl.when(tv_ref[i] > 0)
    def _():
        x = _load_slab(x_ref, 0, tm).astype(BF16)
        hid = _dot(x, w1_bf[...]) + b1_ref[0]
        gate = jnp.minimum(hid[:, :ff], SWIGLU_LIMIT)
        up = jnp.clip(hid[:, ff:], -SWIGLU_LIMIT, SWIGLU_LIMIT)
        act = gate * _sigmoid(SWIGLU_ALPHA * gate) * (up + 1.0)
        _store_slab(o_ref, _dot(act.astype(BF16), w2_bf[...]) + b2_ref[0])

    @pl.when(tv_ref[i] == 0)
    def _():
        o_ref[...] = jnp.zeros_like(o_ref)


def expert_ffn(xs, tile_expert, tile_valid, w1, b1, w2, b2, tm, layer):
    n_slots = xs.shape[0] // SLAB
    _, ne, d, ff2 = w1.shape
    ff = w2.shape[2]
    slab = lambda: pl.BlockSpec((tm * SLAB, LANES), lambda i, te, tv: (i, 0))
    grid_spec = pltpu.PrefetchScalarGridSpec(
        num_scalar_prefetch=2,
        grid=(n_slots // tm,),
        in_specs=[pl.BlockSpec((tm * SLAB, LANES), lambda i, te, tv: (i * tv[i], 0)),
                  pl.BlockSpec((None, 1, d, ff2), lambda i, te, tv: (layer, te[i], 0, 0)),
                  pl.BlockSpec((1, 1, ff2), lambda i, te, tv: (te[i], 0, 0)),
                  pl.BlockSpec((None, 1, ff, d), lambda i, te, tv: (layer, te[i], 0, 0)),
                  pl.BlockSpec((1, 1, d), lambda i, te, tv: (te[i], 0, 0))],
        out_specs=slab(),
        scratch_shapes=[pltpu.VMEM((d, ff2), BF16), pltpu.VMEM((ff, d), BF16)])
    return pl.pallas_call(
        functools.partial(_expert_kernel, tm=tm),
        grid_spec=grid_spec,
        out_shape=jax.ShapeDtypeStruct((n_slots * SLAB, LANES), F32),
        compiler_params=_cparams("arbitrary"),
        name="moe_experts",
    )(tile_expert, tile_valid, xs, w1, b1.reshape(ne, 1, ff2), w2, b2.reshape(ne, 1, d))


def _combine_kernel(pos_ref, nxt_ref, x_ref, gt_ref, gates_ref, fin_ref, ys_ref, o_ref, buf, sems, *, tc, final):
    i = pl.program_id(0)
    n = tc * TOP_K
    slot = i % 2

    def copy(idx_ref, r, s):
        src = ys_ref.at[pl.ds(pl.multiple_of(idx_ref[r], SLAB), SLAB)]
        return pltpu.make_async_copy(src, buf.at[s, pl.ds(pl.multiple_of(r * SLAB, SLAB), SLAB)], sems.at[s])

    def issue(idx_ref, s):
        def body(r2, carry):
            for j in range(2):
                copy(idx_ref, 2 * r2 + j, s).start(priority=j)
            return carry
        lax.fori_loop(0, n // 2, body, 0, unroll=4)

    @pl.when(i == 0)
    def _():
        issue(pos_ref, 0)

    @pl.when(i + 1 < pl.num_programs(0))
    def _():
        issue(nxt_ref, 1 - slot)

    def drain(r, carry):
        copy(pos_ref, r, slot).wait()
        return carry
    lax.fori_loop(0, n, drain, 0, unroll=8)
    gates = gates_ref[...]
    y = jnp.zeros((tc, SLAB * LANES), F32)
    for k in range(TOP_K):
        y = y + gates[:, k:k + 1] * _load_slab(buf.at[slot], k * tc, tc)
    out = x_ref[...] + gt_ref[0] * y
    if final:
        ms = jnp.mean(out * out, axis=-1, keepdims=True)
        out = out * lax.rsqrt(ms + NORM_EPS) * fin_ref[...]
    o_ref[...] = out


def combine_rows(x, gate_vec, gates, pos, ys, final_g, final):
    b, l, d = x.shape
    t = b * l
    tc = min(l, 512)
    steps = t // tc
    per_seq = l // tc
    n = tc * TOP_K
    pos_t = _tiled_rows(pos, tc)
    gates_p = gates
    out = pl.pallas_call(
        functools.partial(_combine_kernel, tc=tc, final=final),
        grid=(steps,),
        in_specs=[pl.BlockSpec((n,), lambda i: (i,), memory_space=pltpu.SMEM),
                  pl.BlockSpec((n,), lambda i: (jnp.minimum(i + 1, steps - 1),), memory_space=pltpu.SMEM),
                  pl.BlockSpec((tc, d), lambda i: (i, 0)),
                  pl.BlockSpec((1, 1, d), lambda i: (i // per_seq, 0, 0)),
                  pl.BlockSpec((tc, LANES), lambda i: (i, 0)),
                  pl.BlockSpec((1, d), lambda i: (0, 0)),
                  pl.BlockSpec(memory_space=pl.ANY)],
        out_specs=pl.BlockSpec((tc, d), lambda i: (i, 0)),
        out_shape=jax.ShapeDtypeStruct((t, d), F32),
        scratch_shapes=[pltpu.VMEM((2, n * SLAB, LANES), F32), pltpu.SemaphoreType.DMA((2,))],
        compiler_params=_cparams("arbitrary"),
        name="moe_combine",
    )(pos_t, pos_t, x.reshape(t, d), gate_vec, gates_p, final_g[None, :], ys)
    return out.reshape(b, l, d)


def routing_plan(e, tm):
    k, t = e.shape
    n = k * t
    ef = e.reshape(n)
    onehot = (ef[:, None] == jnp.arange(N_EXPERTS, dtype=ef.dtype)[None, :]).astype(jnp.int32)
    csum = jnp.cumsum(onehot, axis=0)
    counts = csum[-1]
    rank = jnp.take_along_axis(csum, ef[:, None], axis=1)[:, 0] - 1
    padded = ((counts + tm - 1) // tm) * tm
    ends = jnp.cumsum(padded)
    starts = ends - padded
    pos = starts[ef] + rank
    n_slots = -(-(n + N_EXPERTS * (tm - 1)) // tm) * tm
    tile_start = jnp.arange(n_slots // tm, dtype=jnp.int32) * tm
    tile_expert = jnp.minimum(jnp.sum((tile_start[:, None] >= ends[None, :]).astype(jnp.int32), axis=1), N_EXPERTS - 1)
    tile_valid = (tile_start < ends[-1]).astype(jnp.int32)
    return (pos.reshape(k, t).astype(jnp.int32), (starts + counts).astype(jnp.int32), ends.astype(jnp.int32),
            n_slots, tile_expert.astype(jnp.int32), tile_valid)


def _tiled_rows(pos, tile):
    k, t = pos.shape
    return jnp.transpose(pos.reshape(k, t // tile, tile), (1, 0, 2)).reshape(k * t) * SLAB


MOE_TILE = 512


def moe_block(x, shift, scale, gate_vec, g, router_w, router_b, w1, b1, w2, b2, layer, final_g, final):
    b, l, d = x.shape
    t = b * l
    h, e, gates = router(x, shift, scale, g, router_w, router_b)
    pos, pad_lo, pad_hi, n_slots, tile_expert, tile_valid = routing_plan(e, MOE_TILE)
    xs = dispatch_rows(h, pos, pad_lo, pad_hi, n_slots)
    ys = expert_ffn(xs, tile_expert, tile_valid, w1, b1, w2, b2, MOE_TILE, layer)
    return combine_rows(x, gate_vec, gates.reshape(t, LANES), pos, ys, final_g, final)


def _diff_lambda(lp, lam_init):
    lp = lp.astype(F32)
    return jnp.exp(jnp.sum(lp[0] * lp[1])) - jnp.exp(jnp.sum(lp[2] * lp[3])) + lam_init


def kernel(x, c, ctx, c_ctx, norm1, norm2, w_mod, b_mod, w_in, w_out, attn_lambda, attn_subln,
           hyena_short_w, hyena_short_b, hyena_w1, hyena_b1, hyena_w2, hyena_b2, hyena_w3, hyena_freq,
           hyena_bias, pool_w, pool_scale, conv_dw_w, conv_dw_b, conv_ln_g, conv_ln_b, conv_pw_w,
           conv_pw_b, router_w, router_b, moe_w1, moe_b1, moe_w2, moe_b2, final_norm):
    bsz, n_lat, d = x.shape
    n_ctx = ctx.shape[1]
    depth = w_mod.shape[0]
    g = GROUP
    off_hy, off_pool, off_conv = 3 * g, 6 * g, 7 * g

    rows = -(-(bsz + 1) // SUBLANES) * SUBLANES
    cvec = jnp.zeros((rows, d), F32).at[:bsz].set(c).at[bsz].set(c_ctx)
    mods = adaln_all(cvec, w_mod, b_mod)

    cos_t, sin_t, partner = rope_tables(n_lat)
    ones_t = jnp.ones((n_ctx, g), F32)
    zeros_t = jnp.zeros((n_ctx, g), F32)
    qk_scale = HEAD_QK ** -0.5 * math.log2(math.e)

    xc = ctx
    for i in range(depth):
        last = i == depth - 1
        m = mods[i]
        lat = [m[:bsz, None, k * d:(k + 1) * d] for k in range(6)]
        cm = [jnp.broadcast_to(m[bsz, k * d:(k + 1) * d][None, None, :], (bsz, 1, d)) for k in range(6)]
        lam_init = 0.8 - 0.6 * math.exp(-0.3 * i)
        lam = _diff_lambda(attn_lambda[i], lam_init)
        wi = w_in[i]
        g1 = norm1[i][None, :]
        g2 = norm2[i][None, :]

        w_lat = jnp.concatenate([wi, wi[:, partner], wi[:, g + partner]], axis=1).astype(BF16)
        nin = wi.shape[1]
        segs = [(0, g, nin, qk_scale), (g, g, nin + g, 1.0), (2 * g, g, None, 1.0),
                (off_hy, 3 * g, None, 1.0), (off_pool, g, None, 1.0), (off_conv, 2 * g, None, 1.0)]
        q, k, v, p_hy, p_pool, p_conv = modulated_projection(
            x, lat[0], lat[1], g1, w_lat, segs, [BF16, BF16, BF16, F32, F32, F32], cos_t, sin_t)

        if last:
            w_ctx = wi[:, g:3 * g].astype(BF16)
            kc, vc = modulated_projection(xc, cm[0], cm[1], g1, w_ctx,
                                          [(0, g, None, 1.0), (g, g, None, 1.0)], [BF16, BF16], ones_t, zeros_t)
        else:
            csegs = [(0, g, None, qk_scale), (g, g, None, 1.0), (2 * g, g, None, 1.0),
                     (off_hy, 3 * g, None, 1.0), (off_pool, g, None, 1.0), (off_conv, 2 * g, None, 1.0)]
            qc, kc, vc, pc_hy, pc_pool, pc_conv = modulated_projection(
                xc, cm[0], cm[1], g1, wi.astype(BF16), csegs, [BF16, BF16, BF16, F32, F32, F32], ones_t, zeros_t)

        kt_all = jnp.transpose(jnp.concatenate([kc, k], axis=1), (0, 2, 1))
        v_all = jnp.concatenate([vc, v], axis=1)
        o_attn = diff_attention(q, kt_all, v_all, lam, attn_subln[i], 1.0 - lam_init)

        filt_params = (hyena_w1[i], hyena_b1[i], hyena_w2[i], hyena_b2[i], hyena_w3[i], hyena_freq[i])

        def local_groups(p_hy_, p_pool_, p_conv_, n_tokens):
            g_rows = filter_rows(hyena_filters(n_tokens, *filt_params), n_tokens)
            u = hyena_short_conv(p_hy_, hyena_short_w[i], hyena_short_b[i])
            o_hy = hyena_long_conv(u, g_rows, hyena_bias[i])
            o_pool = pool_mixer(p_pool_, pool_w[i], pool_scale[i])
            o_conv = conformer_mixer(p_conv_, conv_dw_w[i], conv_dw_b[i], conv_ln_g[i], conv_ln_b[i],
                                     conv_pw_w[i], conv_pw_b[i])
            return o_hy, o_pool, o_conv

        o_hy, o_pool, o_conv = local_groups(p_hy, p_pool, p_conv, n_lat)
        x = out_projection(x, lat[2], (o_attn, o_hy, o_pool, o_conv), w_out[i])
        if not last:
            oc_attn = diff_attention(qc, jnp.transpose(kc, (0, 2, 1)), vc, lam, attn_subln[i], 1.0 - lam_init)
            oc_hy, oc_pool, oc_conv = local_groups(pc_hy, pc_pool, pc_conv, n_ctx)
            xc = out_projection(xc, cm[2], (oc_attn, oc_hy, oc_pool, oc_conv), w_out[i])

        x = moe_block(x, lat[3], lat[4], lat[5], g2, router_w[i], router_b[i], moe_w1, moe_b1[i], moe_w2, moe_b2[i],
                      i, final_norm, last)
        if not last:
            xc = moe_block(xc, cm[3], cm[4], cm[5], g2, router_w[i], router_b[i], moe_w1, moe_b1[i], moe_w2,
                           moe_b2[i], i, final_norm, False)
    return x
```

```python
import functools
import math

import jax
import jax.numpy as jnp
from jax import lax
from jax.experimental import pallas as pl
from jax.experimental.pallas import tpu as pltpu

F32 = jnp.float32
BF16 = jnp.bfloat16

GRID_W = 64
N_HEADS = 4
HEAD_V = 64
HEAD_QK = 32
GROUP = 256
ROPE_BASE = 10000.0
SUBLN_EPS = 1e-5
NORM_EPS = 1e-6
LN_EPS = 1e-5
POOL_WINDOWS = (2, 4, 8, 16)
CONV_KERNEL = 31
HYENA_SHORT = 3
HYENA_EMB = 33
HYENA_TARGET = 1e-2
HYENA_FAST_PCT = 0.3
HYENA_SLOW_PCT = 1.5
N_EXPERTS = 32
TOP_K = 4
SWIGLU_ALPHA = 1.702
SWIGLU_LIMIT = 7.0

LANES = 128
SUBLANES = 8
VMEM_LIMIT = 56 * 1024 * 1024
HALO = 16


def _cparams(*sem):
    return pltpu.CompilerParams(dimension_semantics=sem, vmem_limit_bytes=VMEM_LIMIT)


def _split_bf16(a):
    hi = a.astype(BF16)
    lo = (a - hi.astype(F32)).astype(BF16)
    return hi, lo


def _dot(a, b):
    return jnp.dot(a, b, preferred_element_type=F32)


def _dot3(a, b):
    ah, al = _split_bf16(a)
    bh, bl = _split_bf16(b)
    return _dot(ah, bh) + (_dot(ah, bl) + _dot(al, bh))


def _sigmoid(x):
    return 1.0 / (1.0 + jnp.exp(-x))


def _adaln_kernel(c_ref, w_ref, b_ref, o_ref):
    cv = c_ref[...]
    s = cv * _sigmoid(cv)
    o_ref[0] = _dot3(s, w_ref[0]) + b_ref[0]


def adaln_all(cvec, w_mod, b_mod):
    depth, d, n = w_mod.shape
    r = cvec.shape[0]
    tn = 1536
    return pl.pallas_call(
        _adaln_kernel,
        grid=(depth, n // tn),
        in_specs=[pl.BlockSpec((r, d), lambda i, j: (0, 0)),
                  pl.BlockSpec((1, d, tn), lambda i, j: (i, 0, j)),
                  pl.BlockSpec((1, 1, tn), lambda i, j: (i, 0, j))],
        out_specs=pl.BlockSpec((1, r, tn), lambda i, j: (i, 0, j)),
        out_shape=jax.ShapeDtypeStruct((depth, r, n), F32),
        compiler_params=_cparams("arbitrary", "arbitrary"),
        name="adaln",
    )(cvec, w_mod, b_mod.reshape(depth, 1, n))


def _filter_kernel(feat_ref, w1_ref, b1_ref, w2_ref, b2_ref, w3_ref, fr_ref, dec_ref, o_ref):
    fr = fr_ref[...]
    z = jnp.sin(fr * (_dot3(feat_ref[...], w1_ref[...]) + b1_ref[...]))
    z = jnp.sin(fr * (_dot3(z, w2_ref[...]) + b2_ref[...]))
    o_ref[...] = _dot3(z, w3_ref[...]) * dec_ref[...]


def hyena_filters(n_tokens, w1, b1, w2, b2, w3, freq):
    bands = (HYENA_EMB - 1) // 2
    t = jnp.linspace(0.0, 1.0, n_tokens, dtype=F32)[:, None]
    w = 2.0 * math.pi * jnp.arange(n_tokens, dtype=F32)[:, None] / n_tokens
    f = jnp.linspace(1e-4, bands - 1, bands, dtype=F32)[None, :]
    feat = jnp.concatenate([t, jnp.cos(f * w), -jnp.sin(f * w)], axis=-1)
    kpad = 40
    feat = jnp.pad(feat, ((0, 0), (0, kpad - HYENA_EMB)))
    w1p = jnp.pad(w1, ((0, kpad - HYENA_EMB), (0, 0)))
    min_decay = math.log(HYENA_TARGET) / HYENA_SLOW_PCT
    max_decay = math.log(HYENA_TARGET) / HYENA_FAST_PCT
    deltas = jnp.abs(jnp.linspace(min_decay, max_decay, GROUP, dtype=F32))
    decay = jnp.exp(-t * deltas)
    nf = w3.shape[1]
    decay = jnp.tile(decay, (1, nf // GROUP))
    ffn = w2.shape[0]
    tl = min(n_tokens, 512)
    full = lambda shape: pl.BlockSpec(shape, lambda i: (0, 0))
    return pl.pallas_call(
        _filter_kernel,
        grid=(n_tokens // tl,),
        in_specs=[pl.BlockSpec((tl, kpad), lambda i: (i, 0)),
                  full((kpad, ffn)), full((1, ffn)), full((ffn, ffn)), full((1, ffn)),
                  full((ffn, nf)), full((1, ffn)),
                  pl.BlockSpec((tl, nf), lambda i: (i, 0))],
        out_specs=pl.BlockSpec((tl, nf), lambda i: (i, 0)),
        out_shape=jax.ShapeDtypeStruct((n_tokens, nf), F32),
        compiler_params=_cparams("arbitrary"),
        name="hyena_filter",
    )(feat, w1p, b1[None, :], w2, b2[None, :], w3, freq[None, :], decay)


def filter_rows(filt, n_tokens):
    order = filt.shape[1] // (2 * GROUP)
    f4 = filt.reshape(n_tokens, order, 2, GROUP)
    hf = f4[:, :, 0]
    hb = f4[:, :, 1]
    rows = jnp.concatenate([jnp.zeros_like(hf[:1]), hb[:0:-1], hf], axis=0)
    return jnp.transpose(rows, (1, 2, 0))[:, :, None, :]


def _proj_kernel(x_ref, sh_ref, sc_ref, g_ref, w_ref, cos_ref, sin_ref, *out_refs, segs):
    x = x_ref[0]
    ms = jnp.mean(x * x, axis=-1, keepdims=True)
    h = (x * lax.rsqrt(ms + NORM_EPS)) * g_ref[...] * (1.0 + sc_ref[0]) + sh_ref[0]
    hb = h.astype(BF16)
    for o_ref, (start, width, swap_start, scale) in zip(out_refs, segs):
        p = _dot(hb, w_ref[:, start:start + width])
        if swap_start is not None:
            ps = _dot(hb, w_ref[:, swap_start:swap_start + width])
            p = p * cos_ref[...] + ps * sin_ref[...]
        if scale != 1.0:
            p = p * scale
        o_ref[0] = p.astype(o_ref.dtype)


def modulated_projection(x, shift, scale, g, w, segs, out_dtypes, cos_t, sin_t):
    b, l, d = x.shape
    tm = min(l, 512)
    nw = w.shape[1]
    rw = cos_t.shape[1]
    outs = [jax.ShapeDtypeStruct((b, l, s[1]), dt) for s, dt in zip(segs, out_dtypes)]
    return pl.pallas_call(
        functools.partial(_proj_kernel, segs=tuple(segs)),
        grid=(b, l // tm),
        in_specs=[pl.BlockSpec((1, tm, d), lambda i, j: (i, j, 0)),
                  pl.BlockSpec((1, 1, d), lambda i, j: (i, 0, 0)),
                  pl.BlockSpec((1, 1, d), lambda i, j: (i, 0, 0)),
                  pl.BlockSpec((1, d), lambda i, j: (0, 0)),
                  pl.BlockSpec((d, nw), lambda i, j: (0, 0)),
                  pl.BlockSpec((tm, rw), lambda i, j: (j, 0)),
                  pl.BlockSpec((tm, rw), lambda i, j: (j, 0))],
        out_specs=[pl.BlockSpec((1, tm, s[1]), lambda i, j: (i, j, 0)) for s in segs],
        out_shape=outs,
        compiler_params=_cparams("arbitrary", "arbitrary"),
        name="mod_proj",
    )(x, shift, scale, g, w, cos_t, sin_t)


def rope_tables(n_tokens):
    rows = n_tokens // GRID_W
    row = jnp.repeat(jnp.arange(rows, dtype=F32), GRID_W)
    col = jnp.tile(jnp.arange(GRID_W, dtype=F32), rows)
    n_freq = HEAD_QK // 4
    inv = ROPE_BASE ** (-jnp.arange(n_freq, dtype=F32) / n_freq)
    lane = jnp.arange(GROUP)
    d = lane % HEAD_QK
    pos = jnp.where((d < 2 * n_freq)[None, :], row[:, None], col[:, None])
    ang = pos * inv[d % n_freq][None, :]
    first = (d % (2 * n_freq)) < n_freq
    cos_t = jnp.cos(ang)
    sin_t = jnp.where(first[None, :], -jnp.sin(ang), jnp.sin(ang))
    partner = jnp.where(first, lane + n_freq, lane - n_freq)
    return cos_t, sin_t, partner


ATTN_ROWS = 16


def _attn_kernel(lam_ref, q_ref, kt_ref, v_ref, g_ref, o_ref, s_even, s_odd, a_even, a_odd, *, out_scale):
    q = q_ref[0]
    lam = lam_ref[0]
    tq = q.shape[0]
    lane = lax.broadcasted_iota(jnp.int32, (1, GROUP), 1)
    map_of_lane = lane // HEAD_QK
    head_of_lane = lane // HEAD_V
    s_bufs = (s_even, s_odd)
    a_bufs = (a_even, a_odd)

    def scores(h):
        for m in range(2):
            keep = jnp.where(map_of_lane == 2 * h + m, 1.0, 0.0).astype(BF16)
            s_bufs[h % 2][m] = _dot(q * keep, kt_ref[0])

    def softmax(h):
        s_ref, a_ref = s_bufs[h % 2], a_bufs[h % 2]
        inv = []
        for r in range(0, tq, ATTN_ROWS):
            s1 = s_ref[1, r:r + ATTN_ROWS, :]
            e1 = jnp.exp2(s1 - jnp.max(s1, axis=-1, keepdims=True))
            sum1 = jnp.sum(e1, axis=-1, keepdims=True)
            s0 = s_ref[0, r:r + ATTN_ROWS, :]
            e0 = jnp.exp2(s0 - jnp.max(s0, axis=-1, keepdims=True))
            sum0 = jnp.sum(e0, axis=-1, keepdims=True)
            a_ref[0, r:r + ATTN_ROWS, :] = (e0 - e1 * (lam * sum0 / sum1)).astype(BF16)
            inv.append(1.0 / sum0)
        return jnp.concatenate(inv, axis=0)

    acc = jnp.zeros((tq, GROUP), F32)
    scores(0)
    for h in range(N_HEADS):
        if h + 1 < N_HEADS:
            scores(h + 1)
        inv0 = softmax(h)
        o = _dot(a_bufs[h % 2][0], v_ref[0]) * inv0
        acc = jnp.where(head_of_lane == h, o, acc)
    sq = acc * acc
    ms = jnp.zeros((tq, GROUP), F32)
    for h in range(N_HEADS):
        hsel = head_of_lane == h
        mh = jnp.sum(jnp.where(hsel, sq, 0.0), axis=-1, keepdims=True) * (1.0 / HEAD_V)
        ms = jnp.where(hsel, mh, ms)
    y = acc * lax.rsqrt(ms + SUBLN_EPS) * g_ref[...] * out_scale
    o_ref[0] = y.astype(o_ref.dtype)


def diff_attention(q, kt, v, lam, subln, out_scale):
    b, nq, _ = q.shape
    nk = v.shape[1]
    tq = min(nq, 512)
    g = jnp.tile(subln, N_HEADS)[None, :]
    return pl.pallas_call(
        functools.partial(_attn_kernel, out_scale=out_scale),
        grid=(b, nq // tq),
        in_specs=[pl.BlockSpec(memory_space=pltpu.SMEM),
                  pl.BlockSpec((1, tq, GROUP), lambda i, j: (i, j, 0)),
                  pl.BlockSpec((1, GROUP, nk), lambda i, j: (i, 0, 0)),
                  pl.BlockSpec((1, nk, GROUP), lambda i, j: (i, 0, 0)),
                  pl.BlockSpec((1, GROUP), lambda i, j: (0, 0))],
        out_specs=pl.BlockSpec((1, tq, GROUP), lambda i, j: (i, j, 0)),
        out_shape=jax.ShapeDtypeStruct((b, nq, GROUP), BF16),
        scratch_shapes=[pltpu.VMEM((2, tq, nk), F32), pltpu.VMEM((2, tq, nk), F32),
                        pltpu.VMEM((1, tq, nk), BF16), pltpu.VMEM((1, tq, nk), BF16)],
        compiler_params=_cparams("arbitrary", "arbitrary"),
        name="diff_attn",
    )(lam.reshape(1), q, kt, v, g)


def _fill_padded(pad_ref, n, write_rows):
    zeros = jnp.zeros((HALO, pad_ref.shape[1]), F32)
    pad_ref[0:HALO, :] = zeros
    pad_ref[HALO + n:HALO + n + HALO, :] = zeros
    write_rows()


def _shifted_rows(pad_ref, r0, ch, offsets):
    rows = ch + 2 * HALO
    blk = pad_ref[pl.ds(r0, rows), :]
    rolled = {0: blk}
    out = {}
    for k in offsets:
        o = HALO + k
        r = o % SUBLANES
        if r not in rolled:
            rolled[r] = pltpu.roll(blk, rows - r, 0)
        out[k] = rolled[r][o - r:o - r + ch]
    return out


def _pool_kernel(u_ref, w_ref, sc_ref, o_ref, pad_ref, *, n, ch):
    def copy(i, c):
        r0 = pl.multiple_of(i * ch, ch)
        pad_ref[pl.ds(HALO + r0, ch), :] = u_ref[0, pl.ds(r0, ch), :]
        return c
    _fill_padded(pad_ref, n, lambda: lax.fori_loop(0, n // ch, copy, 0))
    lane = lax.broadcasted_iota(jnp.int32, (1, GROUP), 1)
    pg = GROUP // len(POOL_WINDOWS)
    reach = POOL_WINDOWS[-1] // 2

    def body(i, c):
        r0 = pl.multiple_of(i * ch, ch)
        taps = _shifted_rows(pad_ref, r0, ch, range(-reach, reach))
        ld = lambda k: taps[k]
        t = r0 + lax.broadcasted_iota(jnp.int32, (ch, 1), 0)
        u = ld(0)
        sums = []
        s = u + ld(-1)
        sums.append(s)
        half = 1
        for _ in POOL_WINDOWS[1:]:
            for k in range(half, 2 * half):
                s = s + ld(k) + ld(-k - 1)
            half *= 2
            sums.append(s)
        pooled = None
        for gi, win in reversed(list(enumerate(POOL_WINDOWS))):
            hi = jnp.minimum(t + (win - win // 2), n)
            lo = jnp.maximum(t - win // 2, 0)
            mean = sums[gi] / (hi - lo).astype(F32)
            pooled = mean if pooled is None else jnp.where(lane < (gi + 1) * pg, mean, pooled)
        pooled = pooled - u
        y = _dot(pooled.astype(BF16), w_ref[...]) * sc_ref[...]
        o_ref[0, pl.ds(r0, ch), :] = y.astype(o_ref.dtype)
        return c
    lax.fori_loop(0, n // ch, body, 0)


def pool_mixer(u, w_pool, scale):
    b, n, c = u.shape
    ng, pg, _ = w_pool.shape
    wbd = jnp.zeros((c, c), F32)
    for gi in range(ng):
        wbd = wbd.at[gi * pg:(gi + 1) * pg, gi * pg:(gi + 1) * pg].set(w_pool[gi])
    ch = min(n, 256)
    return pl.pallas_call(
        functools.partial(_pool_kernel, n=n, ch=ch),
        grid=(b,),
        in_specs=[pl.BlockSpec((1, n, c), lambda i: (i, 0, 0)),
                  pl.BlockSpec((c, c), lambda i: (0, 0)),
                  pl.BlockSpec((1, c), lambda i: (0, 0))],
        out_specs=pl.BlockSpec((1, n, c), lambda i: (i, 0, 0)),
        out_shape=jax.ShapeDtypeStruct((b, n, c), BF16),
        scratch_shapes=[pltpu.VMEM((n + 2 * HALO, c), F32)],
        compiler_params=_cparams("arbitrary"),
        name="pool_mixer",
    )(u, wbd.astype(BF16), scale[None, :])


def _conformer_kernel(p_ref, dw_ref, dwb_ref, lg_ref, lb_ref, pw_ref, pwb_ref, o_ref, pad_ref, *, n, ch):
    c = GROUP

    def glu(i, carry):
        r0 = pl.multiple_of(i * ch, ch)
        a = p_ref[0, pl.ds(r0, ch), 0:c]
        g = p_ref[0, pl.ds(r0, ch), c:2 * c]
        pad_ref[pl.ds(HALO + r0, ch), :] = a * _sigmoid(g)
        return carry
    _fill_padded(pad_ref, n, lambda: lax.fori_loop(0, n // ch, glu, 0))
    half = (CONV_KERNEL - 1) // 2

    def body(i, carry):
        r0 = pl.multiple_of(i * ch, ch)
        acc = jnp.zeros((ch, c), F32) + dwb_ref[...]
        taps = _shifted_rows(pad_ref, r0, ch, range(-half, half + 1))
        for k in range(CONV_KERNEL):
            acc = acc + taps[k - half] * dw_ref[k:k + 1, :]
        mu = jnp.mean(acc, axis=-1, keepdims=True)
        xc = acc - mu
        var = jnp.mean(xc * xc, axis=-1, keepdims=True)
        y = xc * lax.rsqrt(var + LN_EPS) * lg_ref[...] + lb_ref[...]
        y = y * _sigmoid(y)
        out = _dot(y.astype(BF16), pw_ref[...]) + pwb_ref[...]
        o_ref[0, pl.ds(r0, ch), :] = out.astype(o_ref.dtype)
        return carry
    lax.fori_loop(0, n // ch, body, 0)


def conformer_mixer(p, dw_w, dw_b, ln_g, ln_b, pw_w, pw_b):
    b, n, c2 = p.shape
    c = c2 // 2
    ch = min(n, 128)
    kp = 32
    dw = jnp.pad(dw_w, ((0, kp - CONV_KERNEL), (0, 0)))
    vec = lambda: pl.BlockSpec((1, c), lambda i: (0, 0))
    return pl.pallas_call(
        functools.partial(_conformer_kernel, n=n, ch=ch),
        grid=(b,),
        in_specs=[pl.BlockSpec((1, n, c2), lambda i: (i, 0, 0)),
                  pl.BlockSpec((kp, c), lambda i: (0, 0)),
                  vec(), vec(), vec(),
                  pl.BlockSpec((c, c), lambda i: (0, 0)),
                  vec()],
        out_specs=pl.BlockSpec((1, n, c), lambda i: (i, 0, 0)),
        out_shape=jax.ShapeDtypeStruct((b, n, c), BF16),
        scratch_shapes=[pltpu.VMEM((n + 2 * HALO, c), F32)],
        compiler_params=_cparams("arbitrary"),
        name="conformer_mixer",
    )(p, dw, dw_b[None, :], ln_g[None, :], ln_b[None, :], pw_w.astype(BF16), pw_b[None, :])


def _short_conv_kernel(p_ref, w_ref, b_ref, o_ref, pad_ref, *, n, ch):
    def copy(i, carry):
        r0 = pl.multiple_of(i * ch, ch)
        pad_ref[pl.ds(HALO + r0, ch), :] = p_ref[0, pl.ds(r0, ch), :]
        return carry
    _fill_padded(pad_ref, n, lambda: lax.fori_loop(0, n // ch, copy, 0))
    half = (HYENA_SHORT - 1) // 2

    def body(i, carry):
        r0 = pl.multiple_of(i * ch, ch)
        acc = jnp.zeros((ch, pad_ref.shape[1]), F32) + b_ref[...]
        taps = _shifted_rows(pad_ref, r0, ch, range(-half, half + 1))
        for k in range(HYENA_SHORT):
            acc = acc + taps[k - half] * w_ref[k:k + 1, :]
        o_ref[0, pl.ds(r0, ch), :] = acc.astype(o_ref.dtype)
        return carry
    lax.fori_loop(0, n // ch, body, 0)


def hyena_short_conv(p, w, bias):
    b, n, c = p.shape
    ch = min(n, 128)
    wp = jnp.pad(w, ((0, SUBLANES - HYENA_SHORT), (0, 0)))
    return pl.pallas_call(
        functools.partial(_short_conv_kernel, n=n, ch=ch),
        grid=(b,),
        in_specs=[pl.BlockSpec((1, n, c), lambda i: (i, 0, 0)),
                  pl.BlockSpec((SUBLANES, c), lambda i: (0, 0)),
                  pl.BlockSpec((1, c), lambda i: (0, 0))],
        out_specs=pl.BlockSpec((1, n, c), lambda i: (i, 0, 0)),
        out_shape=jax.ShapeDtypeStruct((b, n, c), BF16),
        scratch_shapes=[pltpu.VMEM((n + 2 * HALO, c), F32)],
        compiler_params=_cparams("arbitrary"),
        name="hyena_short_conv",
    )(p, wp, bias[None, :])


def _toeplitz_conv(g_ref, u, nb, tb, bsz):
    ys = [jnp.zeros((bsz, tb), F32) for _ in range(nb)]
    for d in range(-(nb - 1), nb):
        start = tb * (nb - 1 + d)
        win = jnp.broadcast_to(g_ref[:, start:start + 2 * tb], (tb, 2 * tb))
        rolled = pltpu.roll(win, 0, 1, stride=1, stride_axis=0)
        blk = rolled[:, tb:2 * tb].astype(BF16)
        j_lo, j_hi = max(0, -d), min(nb, nb - d)
        o = _dot(u[j_lo * bsz:j_hi * bsz], blk)
        for j in range(j_lo, j_hi):
            ys[j + d] = ys[j + d] + o[(j - j_lo) * bsz:(j - j_lo + 1) * bsz]
    return ys


def _hyena_kernel(bias_ref, g_ref, v_ref, x1_ref, x2_ref, o_ref, *, nb, tb, bsz, cb):
    base = pl.program_id(0) * cb

    def body(ci, carry):
        v = v_ref[ci]
        y1 = jnp.concatenate(_toeplitz_conv(g_ref.at[0, ci], v, nb, tb, bsz), axis=0)
        z = x1_ref[ci].astype(F32) * (y1 + bias_ref[0, base + ci] * v.astype(F32))
        y2 = jnp.concatenate(_toeplitz_conv(g_ref.at[1, ci], z.astype(BF16), nb, tb, bsz), axis=0)
        out = x2_ref[ci].astype(F32) * (y2 + bias_ref[1, base + ci] * z)
        o_ref[ci] = out.astype(o_ref.dtype)
        return carry
    lax.fori_loop(0, cb, body, 0, unroll=2)


def hyena_long_conv(u, g_rows, bias):
    b, n, _ = u.shape
    tb = min(n, 256)
    nb = n // tb
    nbb = nb * b
    ut = jnp.transpose(u.reshape(b, nb, tb, 3, GROUP), (3, 4, 1, 0, 2)).reshape(3, GROUP, nbb, tb)
    cb = 8
    blk = lambda part: pl.BlockSpec((None, cb, nbb, tb), lambda i, part=part: (part, i, 0, 0))
    out = pl.pallas_call(
        functools.partial(_hyena_kernel, nb=nb, tb=tb, bsz=b, cb=cb),
        grid=(GROUP // cb,),
        in_specs=[pl.BlockSpec(memory_space=pltpu.SMEM),
                  pl.BlockSpec((2, cb, 1, 2 * n), lambda i: (0, i, 0, 0)),
                  blk(0), blk(1), blk(2)],
        out_specs=pl.BlockSpec((cb, nbb, tb), lambda i: (i, 0, 0)),
        out_shape=jax.ShapeDtypeStruct((GROUP, nbb, tb), BF16),
        compiler_params=_cparams("arbitrary"),
        name="hyena_long_conv",
    )(bias, g_rows, ut, ut, ut)
    return jnp.transpose(out.reshape(GROUP, nb, b, tb), (2, 1, 3, 0)).reshape(b, n, GROUP)


def _out_proj_kernel(x_ref, gt_ref, a_ref, h_ref, p_ref, c_ref, w_ref, o_ref):
    acc = _dot(a_ref[0].astype(BF16), w_ref[0:GROUP, :])
    acc = acc + _dot(h_ref[0].astype(BF16), w_ref[GROUP:2 * GROUP, :])
    acc = acc + _dot(p_ref[0].astype(BF16), w_ref[2 * GROUP:3 * GROUP, :])
    acc = acc + _dot(c_ref[0].astype(BF16), w_ref[3 * GROUP:4 * GROUP, :])
    o_ref[0] = x_ref[0] + gt_ref[0] * acc


def out_projection(x, gate, parts, w_out):
    b, l, d = x.shape
    tm = min(l, 512)
    part = lambda: pl.BlockSpec((1, tm, GROUP), lambda i, j: (i, j, 0))
    return pl.pallas_call(
        _out_proj_kernel,
        grid=(b, l // tm),
        in_specs=[pl.BlockSpec((1, tm, d), lambda i, j: (i, j, 0)),
                  pl.BlockSpec((1, 1, d), lambda i, j: (i, 0, 0)),
                  part(), part(), part(), part(),
                  pl.BlockSpec(w_out.shape, lambda i, j: (0, 0))],
        out_specs=pl.BlockSpec((1, tm, d), lambda i, j: (i, j, 0)),
        out_shape=jax.ShapeDtypeStruct((b, l, d), F32),
        compiler_params=_cparams("arbitrary", "arbitrary"),
        name="out_proj",
    )(x, gate, *parts, w_out.astype(BF16))


SLAB = 8


def _store_slab(ref, val, start=0):
    rows = val.shape[0]
    for s in range(SLAB):
        ref[pl.ds(start * SLAB + s, rows, stride=SLAB), :] = val[:, s * LANES:(s + 1) * LANES]


def _load_slab(ref, start, rows):
    return jnp.concatenate([ref[pl.ds(start * SLAB + s, rows, stride=SLAB), :] for s in range(SLAB)], axis=1)


def _out_proj_router_kernel(x_ref, gt_ref, a_ref, hy_ref, p_ref, c_ref, w_ref, sh_ref, sc_ref, g_ref, wh_ref,
                            wl_ref, rb_ref, xo_ref, h_ref, idx_ref, gate_ref):
    acc = _dot(a_ref[0].astype(BF16), w_ref[0:GROUP, :])
    acc = acc + _dot(hy_ref[0].astype(BF16), w_ref[GROUP:2 * GROUP, :])
    acc = acc + _dot(p_ref[0].astype(BF16), w_ref[2 * GROUP:3 * GROUP, :])
    acc = acc + _dot(c_ref[0].astype(BF16), w_ref[3 * GROUP:4 * GROUP, :])
    x = x_ref[0] + gt_ref[0] * acc
    xo_ref[0] = x
    _route_tile(x, sh_ref, sc_ref, g_ref, wh_ref, wl_ref, rb_ref, h_ref, idx_ref, gate_ref)


def out_projection_router(x, gate, parts, w_out, shift, scale, g, router_w, router_b):
    b, l, d = x.shape
    tm = min(l, 512)
    per_seq = l // tm
    wh, wl = _split_bf16(router_w)
    vec3 = lambda: pl.BlockSpec((1, 1, d), lambda i, j: (i, 0, 0))
    tile = lambda w: pl.BlockSpec((1, tm, w), lambda i, j: (i, j, 0))
    const = lambda shape: pl.BlockSpec(shape, lambda i, j: (0, 0))
    xo, h, idx, gates = pl.pallas_call(
        _out_proj_router_kernel,
        grid=(b, per_seq),
        in_specs=[tile(d), vec3(), tile(GROUP), tile(GROUP), tile(GROUP), tile(GROUP), const(w_out.shape),
                  vec3(), vec3(), const((1, d)), const(wh.shape), const(wl.shape), const((1, N_EXPERTS))],
        out_specs=[tile(d), pl.BlockSpec((tm * SLAB, LANES), lambda i, j: (i * per_seq + j, 0)),
                   pl.BlockSpec((SUBLANES, tm), lambda i, j: (0, i * per_seq + j)), tile(LANES)],
        out_shape=[jax.ShapeDtypeStruct((b, l, d), F32),
                   jax.ShapeDtypeStruct((b * l * SLAB, LANES), F32),
                   jax.ShapeDtypeStruct((SUBLANES, b * l), jnp.int32),
                   jax.ShapeDtypeStruct((b, l, LANES), F32)],
        compiler_params=_cparams("arbitrary", "arbitrary"),
        name="out_proj_router",
    )(x, gate, *parts, w_out.astype(BF16), shift, scale, g, wh, wl, router_b[None, :])
    return xo, h, idx[:TOP_K], gates


def _router_kernel(x_ref, sh_ref, sc_ref, g_ref, wh_ref, wl_ref, rb_ref, h_ref, idx_ref, gate_ref):
    _route_tile(x_ref[0], sh_ref, sc_ref, g_ref, wh_ref, wl_ref, rb_ref, h_ref, idx_ref, gate_ref)


def _route_tile(x, sh_ref, sc_ref, g_ref, wh_ref, wl_ref, rb_ref, h_ref, idx_ref, gate_ref):
    ms = jnp.mean(x * x, axis=-1, keepdims=True)
    h = (x * lax.rsqrt(ms + NORM_EPS)) * g_ref[...] * (1.0 + sc_ref[0]) + sh_ref[0]
    _store_slab(h_ref, h)
    hh, hl = _split_bf16(h)
    logits = _dot(hh, wh_ref[...]) + (_dot(hh, wl_ref[...]) + _dot(hl, wh_ref[...])) + rb_ref[...]
    tm = x.shape[0]
    lane = lax.broadcasted_iota(jnp.int32, (tm, N_EXPERTS), 1).astype(F32)
    out_lane = lax.broadcasted_iota(jnp.int32, (tm, LANES), 1)
    vals, idxs = [], []
    cur = logits
    for _ in range(TOP_K):
        mx = jnp.max(cur, axis=-1, keepdims=True)
        ix = jnp.min(jnp.where(cur == mx, lane, float(N_EXPERTS)), axis=-1, keepdims=True)
        vals.append(mx)
        idxs.append(ix)
        cur = jnp.where(lane == ix, -jnp.inf, cur)
    es = [jnp.exp(v - vals[0]) for v in vals]
    inv = 1.0 / (es[0] + es[1] + es[2] + es[3])
    idx_out = jnp.zeros((tm, LANES), F32)
    gate_out = jnp.zeros((tm, LANES), F32)
    for k in range(TOP_K):
        idx_out = jnp.where(out_lane == k, idxs[k], idx_out)
        gate_out = jnp.where(out_lane == k, es[k] * inv, gate_out)
    idx_ref[...] = idx_out.T[:SUBLANES].astype(jnp.int32)
    gate_ref[0] = gate_out


def router(x, shift, scale, g, router_w, router_b):
    b, l, d = x.shape
    assert d == SLAB * LANES
    tm = min(l, 512)
    per_seq = l // tm
    wh, wl = _split_bf16(router_w)
    vec3 = lambda: pl.BlockSpec((1, 1, d), lambda i, j: (i, 0, 0))
    tile = lambda w: pl.BlockSpec((1, tm, w), lambda i, j: (i, j, 0))
    h, idx, gates = pl.pallas_call(
        _router_kernel,
        grid=(b, per_seq),
        in_specs=[tile(d), vec3(), vec3(),
                  pl.BlockSpec((1, d), lambda i, j: (0, 0)),
                  pl.BlockSpec(wh.shape, lambda i, j: (0, 0)),
                  pl.BlockSpec(wl.shape, lambda i, j: (0, 0)),
                  pl.BlockSpec((1, N_EXPERTS), lambda i, j: (0, 0))],
        out_specs=[pl.BlockSpec((tm * SLAB, LANES), lambda i, j: (i * per_seq + j, 0)),
                   pl.BlockSpec((SUBLANES, tm), lambda i, j: (0, i * per_seq + j)), tile(LANES)],
        out_shape=[jax.ShapeDtypeStruct((b * l * SLAB, LANES), F32),
                   jax.ShapeDtypeStruct((SUBLANES, b * l), jnp.int32),
                   jax.ShapeDtypeStruct((b, l, LANES), F32)],
        compiler_params=_cparams("arbitrary", "arbitrary"),
        name="moe_router",
    )(x, shift, scale, g, wh, wl, router_b[None, :])
    return h, idx[:TOP_K], gates


DISPATCH_ROWS = 2048


def _dispatch_kernel(lo_ref, hi_ref, pos_ref, src_ref, dst_ref, zero_ref, sem, *, rows):
    i = pl.program_id(0)

    def copy(tok, k):
        src = src_ref.at[pl.ds(pl.multiple_of(tok * SLAB, SLAB), SLAB)]
        dst = dst_ref.at[pl.ds(pl.multiple_of(pos_ref[k * (rows // TOP_K) + tok], SLAB), SLAB)]
        return pltpu.make_async_copy(src, dst, sem)

    def start(tok, carry):
        for k in range(TOP_K):
            copy(tok, k).start(priority=k % 2)
        return carry
    lax.fori_loop(0, rows // TOP_K, start, 0, unroll=2)

    def wait(tok, carry):
        for k in range(TOP_K):
            copy(tok, k).wait()
        return carry
    lax.fori_loop(0, rows // TOP_K, wait, 0, unroll=2)

    @pl.when(i == pl.num_programs(0) - 1)
    def _():
        zero_ref[...] = jnp.zeros_like(zero_ref)

        def fill(r):
            return pltpu.make_async_copy(zero_ref, dst_ref.at[pl.ds(pl.multiple_of(r * SLAB, SLAB), SLAB)], sem)

        def per_expert(e, carry):
            def fill_start(r, c):
                fill(r).start()
                return c

            def fill_wait(r, c):
                fill(r).wait()
                return c
            lax.fori_loop(lo_ref[e], hi_ref[e], fill_start, 0)
            lax.fori_loop(lo_ref[e], hi_ref[e], fill_wait, 0)
            return carry
        lax.fori_loop(0, N_EXPERTS, per_expert, 0)


def dispatch_rows(h, pos, pad_lo, pad_hi, n_slots):
    rows = DISPATCH_ROWS
    tokens = rows // TOP_K
    pos_rows = _tiled_rows(pos, tokens)
    grid_spec = pltpu.PrefetchScalarGridSpec(
        num_scalar_prefetch=2,
        grid=(pos_rows.shape[0] // rows,),
        in_specs=[pl.BlockSpec((rows,), lambda i, lo, hi: (i,), memory_space=pltpu.SMEM),
                  pl.BlockSpec((tokens * SLAB, LANES), lambda i, lo, hi: (i, 0))],
        out_specs=pl.BlockSpec(memory_space=pl.ANY),
        scratch_shapes=[pltpu.VMEM((SLAB, LANES), h.dtype), pltpu.SemaphoreType.DMA(())])
    return pl.pallas_call(
        functools.partial(_dispatch_kernel, rows=rows),
        grid_spec=grid_spec,
        out_shape=jax.ShapeDtypeStruct((n_slots * SLAB, LANES), h.dtype),
        compiler_params=_cparams("arbitrary"),
        name="moe_dispatch",
    )(pad_lo, pad_hi, pos_rows, h)


def _expert_kernel(te_ref, tv_ref, x_ref, w1_ref, b1_ref, w2_ref, b2_ref, o_ref, w1_bf, w2_bf, *, tm):
    i = pl.program_id(0)
    ff = w2_ref.shape[1]
    new_expert = jnp.logical_or(i == 0, te_ref[i] != te_ref[jnp.maximum(i - 1, 0)])

    @pl.when(jnp.logical_and(tv_ref[i] > 0, new_expert))
    def _():
        w1_bf[...] = w1_ref[0].astype(BF16)
        w2_bf[...] = w2_ref[0].astype(BF16)

    @pl.when(tv_ref[i] > 0)
    def _():
        x = _load_slab(x_ref, 0, tm).astype(BF16)
        hid = _dot(x, w1_bf[...]) + b1_ref[0]
        gate = jnp.minimum(hid[:, :ff], SWIGLU_LIMIT)
        up = jnp.clip(hid[:, ff:], -SWIGLU_LIMIT, SWIGLU_LIMIT)
        act = gate * _sigmoid(SWIGLU_ALPHA * gate) * (up + 1.0)
        _store_slab(o_ref, _dot(act.astype(BF16), w2_bf[...]) + b2_ref[0])

    @pl.when(tv_ref[i] == 0)
    def _():
        o_ref[...] = jnp.zeros_like(o_ref)


def expert_ffn(xs, tile_expert, tile_valid, w1, b1, w2, b2, tm, layer):
    n_slots = xs.shape[0] // SLAB
    _, ne, d, ff2 = w1.shape
    ff = w2.shape[2]
    slab = lambda: pl.BlockSpec((tm * SLAB, LANES), lambda i, te, tv: (i, 0))
    grid_spec = pltpu.PrefetchScalarGridSpec(
        num_scalar_prefetch=2,
        grid=(n_slots // tm,),
        in_specs=[pl.BlockSpec((tm * SLAB, LANES), lambda i, te, tv: (i * tv[i], 0)),
                  pl.BlockSpec((None, 1, d, ff2), lambda i, te, tv: (layer, te[i], 0, 0)),
                  pl.BlockSpec((1, 1, ff2), lambda i, te, tv: (te[i], 0, 0)),
                  pl.BlockSpec((None, 1, ff, d), lambda i, te, tv: (layer, te[i], 0, 0)),
                  pl.BlockSpec((1, 1, d), lambda i, te, tv: (te[i], 0, 0))],
        out_specs=slab(),
        scratch_shapes=[pltpu.VMEM((d, ff2), BF16), pltpu.VMEM((ff, d), BF16)])
    return pl.pallas_call(
        functools.partial(_expert_kernel, tm=tm),
        grid_spec=grid_spec,
        out_shape=jax.ShapeDtypeStruct((n_slots * SLAB, LANES), F32),
        compiler_params=_cparams("arbitrary"),
        name="moe_experts",
    )(tile_expert, tile_valid, xs, w1, b1.reshape(ne, 1, ff2), w2, b2.reshape(ne, 1, d))


def _combine_kernel(pos_ref, nxt_ref, x_ref, gt_ref, gates_ref, fin_ref, ys_ref, o_ref, buf, sems, *, tc, final):
    i = pl.program_id(0)
    n = tc * TOP_K
    slot = i % 2

    def copy(idx_ref, r, s):
        src = ys_ref.at[pl.ds(pl.multiple_of(idx_ref[r], SLAB), SLAB)]
        return pltpu.make_async_copy(src, buf.at[s, pl.ds(pl.multiple_of(r * SLAB, SLAB), SLAB)], sems.at[s])

    def issue(idx_ref, s):
        def body(r2, carry):
            for j in range(2):
                copy(idx_ref, 2 * r2 + j, s).start(priority=j)
            return carry
        lax.fori_loop(0, n // 2, body, 0, unroll=4)

    @pl.when(i == 0)
    def _():
        issue(pos_ref, 0)

    @pl.when(i + 1 < pl.num_programs(0))
    def _():
        issue(nxt_ref, 1 - slot)

    def drain(r, carry):
        copy(pos_ref, r, slot).wait()
        return carry
    lax.fori_loop(0, n, drain, 0, unroll=8)
    gates = gates_ref[...]
    y = jnp.zeros((tc, SLAB * LANES), F32)
    for k in range(TOP_K):
        y = y + gates[:, k:k + 1] * _load_slab(buf.at[slot], k * tc, tc)
    out = x_ref[...] + gt_ref[0] * y
    if final:
        ms = jnp.mean(out * out, axis=-1, keepdims=True)
        out = out * lax.rsqrt(ms + NORM_EPS) * fin_ref[...]
    o_ref[...] = out


def combine_rows(x, gate_vec, gates, pos, ys, final_g, final):
    b, l, d = x.shape
    t = b * l
    tc = min(l, 512)
    steps = t // tc
    per_seq = l // tc
    n = tc * TOP_K
    pos_t = _tiled_rows(pos, tc)
    gates_p = gates
    out = pl.pallas_call(
        functools.partial(_combine_kernel, tc=tc, final=final),
        grid=(steps,),
        in_specs=[pl.BlockSpec((n,), lambda i: (i,), memory_space=pltpu.SMEM),
                  pl.BlockSpec((n,), lambda i: (jnp.minimum(i + 1, steps - 1),), memory_space=pltpu.SMEM),
                  pl.BlockSpec((tc, d), lambda i: (i, 0)),
                  pl.BlockSpec((1, 1, d), lambda i: (i // per_seq, 0, 0)),
                  pl.BlockSpec((tc, LANES), lambda i: (i, 0)),
                  pl.BlockSpec((1, d), lambda i: (0, 0)),
                  pl.BlockSpec(memory_space=pl.ANY)],
        out_specs=pl.BlockSpec((tc, d), lambda i: (i, 0)),
        out_shape=jax.ShapeDtypeStruct((t, d), F32),
        scratch_shapes=[pltpu.VMEM((2, n * SLAB, LANES), F32), pltpu.SemaphoreType.DMA((2,))],
        compiler_params=_cparams("arbitrary"),
        name="moe_combine",
    )(pos_t, pos_t, x.reshape(t, d), gate_vec, gates_p, final_g[None, :], ys)
    return out.reshape(b, l, d)


def routing_plan(e, tm):
    k, t = e.shape
    n = k * t
    ef = e.reshape(n)
    onehot = (ef[:, None] == jnp.arange(N_EXPERTS, dtype=ef.dtype)[None, :]).astype(jnp.int32)
    csum = jnp.cumsum(onehot, axis=0)
    counts = csum[-1]
    rank = jnp.take_along_axis(csum, ef[:, None], axis=1)[:, 0] - 1
    padded = ((counts + tm - 1) // tm) * tm
    ends = jnp.cumsum(padded)
    starts = ends - padded
    pos = starts[ef] + rank
    n_slots = -(-(n + N_EXPERTS * (tm - 1)) // tm) * tm
    tile_start = jnp.arange(n_slots // tm, dtype=jnp.int32) * tm
    tile_expert = jnp.minimum(jnp.sum((tile_start[:, None] >= ends[None, :]).astype(jnp.int32), axis=1), N_EXPERTS - 1)
    tile_valid = (tile_start < ends[-1]).astype(jnp.int32)
    return (pos.reshape(k, t).astype(jnp.int32), (starts + counts).astype(jnp.int32), ends.astype(jnp.int32),
            n_slots, tile_expert.astype(jnp.int32), tile_valid)


def _tiled_rows(pos, tile):
    k, t = pos.shape
    return jnp.transpose(pos.reshape(k, t // tile, tile), (1, 0, 2)).reshape(k * t) * SLAB


MOE_TILE = 512


def moe_block(x, shift, scale, gate_vec, g, router_w, router_b, w1, b1, w2, b2, layer, final_g, final, routed=None):
    b, l, d = x.shape
    t = b * l
    h, e, gates = routed if routed is not None else router(x, shift, scale, g, router_w, router_b)
    pos, pad_lo, pad_hi, n_slots, tile_expert, tile_valid = routing_plan(e, MOE_TILE)
    xs = dispatch_rows(h, pos, pad_lo, pad_hi, n_slots)
    ys = expert_ffn(xs, tile_expert, tile_valid, w1, b1, w2, b2, MOE_TILE, layer)
    return combine_rows(x, gate_vec, gates.reshape(t, LANES), pos, ys, final_g, final)


def _diff_lambda(lp, lam_init):
    lp = lp.astype(F32)
    return jnp.exp(jnp.sum(lp[0] * lp[1])) - jnp.exp(jnp.sum(lp[2] * lp[3])) + lam_init


def kernel(x, c, ctx, c_ctx, norm1, norm2, w_mod, b_mod, w_in, w_out, attn_lambda, attn_subln,
           hyena_short_w, hyena_short_b, hyena_w1, hyena_b1, hyena_w2, hyena_b2, hyena_w3, hyena_freq,
           hyena_bias, pool_w, pool_scale, conv_dw_w, conv_dw_b, conv_ln_g, conv_ln_b, conv_pw_w,
           conv_pw_b, router_w, router_b, moe_w1, moe_b1, moe_w2, moe_b2, final_norm):
    bsz, n_lat, d = x.shape
    n_ctx = ctx.shape[1]
    depth = w_mod.shape[0]
    g = GROUP
    off_hy, off_pool, off_conv = 3 * g, 6 * g, 7 * g

    rows = -(-(bsz + 1) // SUBLANES) * SUBLANES
    cvec = jnp.zeros((rows, d), F32).at[:bsz].set(c).at[bsz].set(c_ctx)
    mods = adaln_all(cvec, w_mod, b_mod)

    cos_t, sin_t, partner = rope_tables(n_lat)
    ones_t = jnp.ones((n_ctx, g), F32)
    zeros_t = jnp.zeros((n_ctx, g), F32)
    qk_scale = HEAD_QK ** -0.5 * math.log2(math.e)

    xc = ctx
    for i in range(depth):
        last = i == depth - 1
        m = mods[i]
        lat = [m[:bsz, None, k * d:(k + 1) * d] for k in range(6)]
        cm = [jnp.broadcast_to(m[bsz, k * d:(k + 1) * d][None, None, :], (bsz, 1, d)) for k in range(6)]
        lam_init = 0.8 - 0.6 * math.exp(-0.3 * i)
        lam = _diff_lambda(attn_lambda[i], lam_init)
        wi = w_in[i]
        g1 = norm1[i][None, :]
        g2 = norm2[i][None, :]

        w_lat = jnp.concatenate([wi, wi[:, partner], wi[:, g + partner]], axis=1).astype(BF16)
        nin = wi.shape[1]
        segs = [(0, g, nin, qk_scale), (g, g, nin + g, 1.0), (2 * g, g, None, 1.0),
                (off_hy, 3 * g, None, 1.0), (off_pool, g, None, 1.0), (off_conv, 2 * g, None, 1.0)]
        q, k, v, p_hy, p_pool, p_conv = modulated_projection(
            x, lat[0], lat[1], g1, w_lat, segs, [BF16, BF16, BF16, F32, F32, F32], cos_t, sin_t)

        if last:
            w_ctx = wi[:, g:3 * g].astype(BF16)
            kc, vc = modulated_projection(xc, cm[0], cm[1], g1, w_ctx,
                                          [(0, g, None, 1.0), (g, g, None, 1.0)], [BF16, BF16], ones_t, zeros_t)
        else:
            csegs = [(0, g, None, qk_scale), (g, g, None, 1.0), (2 * g, g, None, 1.0),
                     (off_hy, 3 * g, None, 1.0), (off_pool, g, None, 1.0), (off_conv, 2 * g, None, 1.0)]
            qc, kc, vc, pc_hy, pc_pool, pc_conv = modulated_projection(
                xc, cm[0], cm[1], g1, wi.astype(BF16), csegs, [BF16, BF16, BF16, F32, F32, F32], ones_t, zeros_t)

        kt_all = jnp.transpose(jnp.concatenate([kc, k], axis=1), (0, 2, 1))
        v_all = jnp.concatenate([vc, v], axis=1)
        o_attn = diff_attention(q, kt_all, v_all, lam, attn_subln[i], 1.0 - lam_init)

        filt_params = (hyena_w1[i], hyena_b1[i], hyena_w2[i], hyena_b2[i], hyena_w3[i], hyena_freq[i])

        def local_groups(p_hy_, p_pool_, p_conv_, n_tokens):
            g_rows = filter_rows(hyena_filters(n_tokens, *filt_params), n_tokens)
            u = hyena_short_conv(p_hy_, hyena_short_w[i], hyena_short_b[i])
            o_hy = hyena_long_conv(u, g_rows, hyena_bias[i])
            o_pool = pool_mixer(p_pool_, pool_w[i], pool_scale[i])
            o_conv = conformer_mixer(p_conv_, conv_dw_w[i], conv_dw_b[i], conv_ln_g[i], conv_ln_b[i],
                                     conv_pw_w[i], conv_pw_b[i])
            return o_hy, o_pool, o_conv

        o_hy, o_pool, o_conv = local_groups(p_hy, p_pool, p_conv, n_lat)
        x, *routed = out_projection_router(x, lat[2], (o_attn, o_hy, o_pool, o_conv), w_out[i],
                                           lat[3], lat[4], g2, router_w[i], router_b[i])
        if not last:
            oc_attn = diff_attention(qc, jnp.transpose(kc, (0, 2, 1)), vc, lam, attn_subln[i], 1.0 - lam_init)
            oc_hy, oc_pool, oc_conv = local_groups(pc_hy, pc_pool, pc_conv, n_ctx)
            xc, *routed_c = out_projection_router(xc, cm[2], (oc_attn, oc_hy, oc_pool, oc_conv), w_out[i],
                                                  cm[3], cm[4], g2, router_w[i], router_b[i])

        x = moe_block(x, lat[3], lat[4], lat[5], g2, router_w[i], router_b[i], moe_w1, moe_b1[i], moe_w2, moe_b2[i],
                      i, final_norm, last, routed)
        if not last:
            xc = moe_block(xc, cm[3], cm[4], cm[5], g2, router_w[i], router_b[i], moe_w1, moe_b1[i], moe_w2,
                           moe_b2[i], i, final_norm, False, routed_c)
    return x
```
